```python
import jax, jax.numpy as jnp
from jax import lax
import numpy as np

D_MODEL = 1024
BATCH = 8
SEQ = 2048
DEPTH = 2
DEC_BATCH = 1
DEC_SEQ = 16384
PAST_LEN = 128

EPS = 1e-6
ROPE_THETA = 10000.0
NEG = -1e30
N_BRANCH = 3
MLA_HEADS = 8
MLA_NOPE = 64
MLA_ROPE = 32
MLA_V = 64
MLA_Q_LORA = 384
MLA_KV_LORA = 256
Q_BLOCK = 128
F_GROUPS = 4
F_GROUP_DIM = 64
F_WIDTH = F_GROUPS * F_GROUP_DIM
GQA_HEADS = 4
GQA_KV_HEADS = 2
GQA_GROUP = GQA_HEADS // GQA_KV_HEADS
GQA_HEAD_DIM = 64
WINDOW = 128
W_BLOCK = 128
PEER_HEADS = 8
PEER_N_KEYS = 128
PEER_N_EXPERTS = PEER_N_KEYS * PEER_N_KEYS
PEER_QDIM = 256
PEER_HALF = PEER_QDIM // 2
PEER_TOPK = 16
PEER_CHUNK = 128

IN_WIDTHS = (MLA_Q_LORA, MLA_KV_LORA, MLA_ROPE, F_WIDTH,
             GQA_HEADS * GQA_HEAD_DIM, GQA_KV_HEADS * GQA_HEAD_DIM, GQA_KV_HEADS * GQA_HEAD_DIM,
             N_BRANCH * D_MODEL)
IN_COLS = sum(IN_WIDTHS)

kernel_name = "hybrid_mla_fnet_swa_peer_encoder"


def rmsnorm(x, g):
    xf = x.astype(jnp.float32)
    y = xf * lax.rsqrt(jnp.mean(xf * xf, axis=-1, keepdims=True) + EPS)
    return (y * g.astype(jnp.float32)).astype(x.dtype)


def rope(x):
    S, d = x.shape[1], x.shape[-1]
    inv = 1.0 / (ROPE_THETA ** (jnp.arange(0, d, 2, dtype=jnp.float32) / d))
    ang = jnp.arange(S, dtype=jnp.float32)[:, None] * inv[None, :]
    cos = jnp.cos(ang)[None, :, None, :]
    sin = jnp.sin(ang)[None, :, None, :]
    xf = x.astype(jnp.float32)
    x1, x2 = xf[..., : d // 2], xf[..., d // 2:]
    return jnp.concatenate([x1 * cos - x2 * sin, x2 * cos + x1 * sin], axis=-1).astype(x.dtype)


def split_cols(z):
    idx = []
    acc = 0
    for w in IN_WIDTHS[:-1]:
        acc += w
        idx.append(acc)
    return jnp.split(z, idx, axis=-1)


def mla_attention(q_nope, q_rope, k_nope, k_rope, v):
    B, S, H, _ = q_nope.shape
    nb = S // Q_BLOCK
    scale = (MLA_NOPE + MLA_ROPE) ** -0.5
    qn = q_nope.reshape(B, nb, Q_BLOCK, H, MLA_NOPE).transpose(1, 0, 2, 3, 4)
    qr = q_rope.reshape(B, nb, Q_BLOCK, H, MLA_ROPE).transpose(1, 0, 2, 3, 4)

    def block(args):
        qn_b, qr_b = args
        s = (jnp.einsum('bqhd,bkhd->bhqk', qn_b, k_nope)
             + jnp.einsum('bqhd,bkd->bhqk', qr_b, k_rope)).astype(jnp.float32) * scale
        p = jax.nn.softmax(s, axis=-1).astype(v.dtype)
        return jnp.einsum('bhqk,bkhd->bqhd', p, v)

    o = lax.map(block, (qn, qr))
    return o.transpose(1, 0, 2, 3, 4).reshape(B, S, H * MLA_V)


def fourier_mix(xf):
    B, S, _ = xf.shape
    xg = xf.reshape(B, S, F_GROUPS, F_GROUP_DIM).astype(jnp.float32)
    y = jnp.fft.fftn(xg, axes=(1, 3), norm='ortho').real
    return y.reshape(B, S, F_WIDTH).astype(xf.dtype)


def windowed_gqa(q, k, v, sink):
    B, S, H, Dh = q.shape
    nb = S // W_BLOCK
    pad = ((0, 0), (W_BLOCK, W_BLOCK), (0, 0), (0, 0))
    kp = jnp.pad(k, pad).reshape(B, nb + 2, W_BLOCK, GQA_KV_HEADS, Dh)
    vp = jnp.pad(v, pad).reshape(B, nb + 2, W_BLOCK, GQA_KV_HEADS, Dh)
    kwin = jnp.concatenate([kp[:, :-2], kp[:, 1:-1], kp[:, 2:]], axis=2)
    vwin = jnp.concatenate([vp[:, :-2], vp[:, 1:-1], vp[:, 2:]], axis=2)
    qb = q.reshape(B, nb, W_BLOCK, GQA_KV_HEADS, GQA_GROUP, Dh)
    s = jnp.einsum('bnqkgd,bnckd->bnkgqc', qb, kwin).astype(jnp.float32) * (Dh ** -0.5)
    blk = jnp.arange(nb)[:, None, None] * W_BLOCK
    qpos = blk + jnp.arange(W_BLOCK)[None, :, None]
    kpos = blk - W_BLOCK + jnp.arange(3 * W_BLOCK)[None, None, :]
    mask = (jnp.abs(qpos - kpos) <= WINDOW) & (kpos >= 0) & (kpos < S)
    s = jnp.where(mask[None, :, None, None], s, NEG)
    sk = sink.astype(jnp.float32).reshape(GQA_KV_HEADS, GQA_GROUP)[None, None, :, :, None, None]
    m = jnp.maximum(jnp.max(s, axis=-1, keepdims=True), sk)
    p = jnp.exp(s - m)
    den = jnp.sum(p, axis=-1, keepdims=True) + jnp.exp(sk - m)
    o = jnp.einsum('bnkgqc,bnckd->bnqkgd', (p / den).astype(v.dtype), vwin)
    return o.reshape(B, S, H * Dh)


def token_mixer(h, w_in, q_norm_g, w_uq, kv_norm_g, w_ukv, mla_w_o, fnet_w, fnet_b,
                gqa_sink, gqa_w_o, w_out):
    B, S, D = h.shape
    z = h @ w_in
    cq, ckv, kr, xf, gq, gk, gv, gate = split_cols(z)
    q = (rmsnorm(cq, q_norm_g) @ w_uq).reshape(B, S, MLA_HEADS, MLA_NOPE + MLA_ROPE)
    q_nope, q_rope = q[..., :MLA_NOPE], rope(q[..., MLA_NOPE:])
    kv = (rmsnorm(ckv, kv_norm_g) @ w_ukv).reshape(B, S, MLA_HEADS, MLA_NOPE + MLA_V)
    k_nope, v_mla = kv[..., :MLA_NOPE], kv[..., MLA_NOPE:]
    k_rope = rope(kr[:, :, None, :])[:, :, 0, :]
    mla_o = mla_attention(q_nope, q_rope, k_nope, k_rope, v_mla) @ mla_w_o
    fnet_o = fourier_mix(xf) @ fnet_w + fnet_b
    qg = rope(gq.reshape(B, S, GQA_HEADS, GQA_HEAD_DIM))
    kg = rope(gk.reshape(B, S, GQA_KV_HEADS, GQA_HEAD_DIM))
    vg = gv.reshape(B, S, GQA_KV_HEADS, GQA_HEAD_DIM)
    gqa_o = windowed_gqa(qg, kg, vg, gqa_sink) @ gqa_w_o
    g = jax.nn.sigmoid(gate.astype(jnp.float32)).astype(h.dtype).reshape(B, S, N_BRANCH, D)
    merged = g[:, :, 0] * mla_o + g[:, :, 1] * fnet_o + g[:, :, 2] * gqa_o
    return merged @ w_out


def peer(h, w_q, keys, u, v):
    B, S, D = h.shape
    T = B * S
    hf = h.reshape(T, D)
    q = (hf @ w_q).reshape(T, PEER_HEADS, 2, PEER_HALF)
    s = jnp.einsum('thpd,hpnd->thpn', q, keys).astype(jnp.float32)
    s1, i1 = lax.top_k(s[:, :, 0], PEER_TOPK)
    s2, i2 = lax.top_k(s[:, :, 1], PEER_TOPK)
    cand = (s1[..., :, None] + s2[..., None, :]).reshape(T, PEER_HEADS, PEER_TOPK * PEER_TOPK)
    cidx = (i1[..., :, None] * PEER_N_KEYS + i2[..., None, :]).reshape(T, PEER_HEADS, PEER_TOPK * PEER_TOPK)
    top_s, pos = lax.top_k(cand, PEER_TOPK)
    idx = jnp.take_along_axis(cidx, pos, axis=-1)
    gsc = jax.nn.softmax(top_s, axis=-1)
    nc = T // PEER_CHUNK
    xc = hf.reshape(nc, PEER_CHUNK, D)
    ic = idx.reshape(nc, PEER_CHUNK, PEER_HEADS, PEER_TOPK)
    gc = gsc.reshape(nc, PEER_CHUNK, PEER_HEADS, PEER_TOPK)

    def chunk(args):
        x_c, i_c, g_c = args
        a = jax.nn.gelu(jnp.einsum('cd,chkd->chk', x_c, u[i_c]).astype(jnp.float32), approximate=False)
        wgt = (g_c * a).astype(v.dtype)
        return jnp.einsum('chk,chkd->cd', wgt, v[i_c])

    out = lax.map(chunk, (xc, ic, gc))
    return out.reshape(B, S, D)


def trunk(x, params):
    (norm_mix_g, w_in, mla_q_norm_g, mla_w_uq, mla_kv_norm_g, mla_w_ukv, mla_w_o,
     fnet_w, fnet_b, gqa_sink, gqa_w_o, w_out, norm_ffn_g, peer_w_q, peer_keys,
     peer_u, peer_v, final_norm_g) = params
    for l in range(DEPTH):
        x = x + token_mixer(rmsnorm(x, norm_mix_g[l]), w_in[l], mla_q_norm_g[l], mla_w_uq[l],
                            mla_kv_norm_g[l], mla_w_ukv[l], mla_w_o[l], fnet_w[l], fnet_b[l],
                            gqa_sink[l], gqa_w_o[l], w_out[l])
        x = x + peer(rmsnorm(x, norm_ffn_g[l]), peer_w_q[l], peer_keys[l], peer_u[l], peer_v[l])
    return rmsnorm(x, final_norm_g)


def setup_inputs(seed: int = 0) -> dict:
    key = jax.random.key(seed)
    ks = jax.random.split(key, 20)

    def nrm(k, shape, scale):
        return jax.random.normal(k, shape, jnp.float32) * scale

    def gain(k, shape):
        return 1.0 + 0.01 * jax.random.normal(k, shape, jnp.float32)

    return {
        "x_prompt": nrm(ks[0], (BATCH, SEQ, D_MODEL), 1.0),
        "x_sample": nrm(ks[1], (DEC_BATCH, DEC_SEQ, D_MODEL), 1.0),
        "norm_mix_g": gain(ks[2], (DEPTH, D_MODEL)),
        "w_in": nrm(ks[3], (DEPTH, D_MODEL, IN_COLS), D_MODEL ** -0.5),
        "mla_q_norm_g": gain(ks[4], (DEPTH, MLA_Q_LORA)),
        "mla_w_uq": nrm(ks[5], (DEPTH, MLA_Q_LORA, MLA_HEADS * (MLA_NOPE + MLA_ROPE)), MLA_Q_LORA ** -0.5),
        "mla_kv_norm_g": gain(ks[6], (DEPTH, MLA_KV_LORA)),
        "mla_w_ukv": nrm(ks[7], (DEPTH, MLA_KV_LORA, MLA_HEADS * (MLA_NOPE + MLA_V)), MLA_KV_LORA ** -0.5),
        "mla_w_o": nrm(ks[8], (DEPTH, MLA_HEADS * MLA_V, D_MODEL), (MLA_HEADS * MLA_V) ** -0.5),
        "fnet_w": nrm(ks[9], (DEPTH, F_WIDTH, D_MODEL), F_WIDTH ** -0.5),
        "fnet_b": nrm(ks[10], (DEPTH, D_MODEL), 0.01),
        "gqa_sink": nrm(ks[11], (DEPTH, GQA_HEADS), 0.5),
        "gqa_w_o": nrm(ks[12], (DEPTH, GQA_HEADS * GQA_HEAD_DIM, D_MODEL), (GQA_HEADS * GQA_HEAD_DIM) ** -0.5),
        "w_out": nrm(ks[13], (DEPTH, D_MODEL, D_MODEL), D_MODEL ** -0.5),
        "norm_ffn_g": gain(ks[14], (DEPTH, D_MODEL)),
        "peer_w_q": nrm(ks[15], (DEPTH, D_MODEL, PEER_HEADS * PEER_QDIM), D_MODEL ** -0.5),
        "peer_keys": nrm(ks[16], (DEPTH, PEER_HEADS, 2, PEER_N_KEYS, PEER_HALF), PEER_HALF ** -0.5),
        "peer_u": nrm(ks[17], (DEPTH, PEER_N_EXPERTS, D_MODEL), D_MODEL ** -0.5),
        "peer_v": nrm(ks[18], (DEPTH, PEER_N_EXPERTS, D_MODEL), PEER_HEADS ** -0.5),
        "final_norm_g": gain(ks[19], (D_MODEL,)),
    }


def reference(x_prompt, x_sample, norm_mix_g, w_in, mla_q_norm_g, mla_w_uq, mla_kv_norm_g,
              mla_w_ukv, mla_w_o, fnet_w, fnet_b, gqa_sink, gqa_w_o, w_out, norm_ffn_g,
              peer_w_q, peer_keys, peer_u, peer_v, final_norm_g):
    params = (norm_mix_g, w_in, mla_q_norm_g, mla_w_uq, mla_kv_norm_g, mla_w_ukv, mla_w_o,
              fnet_w, fnet_b, gqa_sink, gqa_w_o, w_out, norm_ffn_g, peer_w_q, peer_keys,
              peer_u, peer_v, final_norm_g)
    y_prompt = trunk(x_prompt, params)
    y_sample = trunk(x_sample, params)
    return (y_prompt, y_sample)
```

```python
import functools
import math

import jax
import jax.numpy as jnp
import numpy as np
from jax import lax
from jax.experimental import pallas as pl
from jax.experimental.pallas import tpu as pltpu

D_MODEL = 1024
EPS = 1e-6
ROPE_THETA = 10000.0
NEG = -1e30
LANES = 128

MLA_HEADS = 8
MLA_NOPE = 64
MLA_ROPE = 32
MLA_V = 64
MLA_Q_LORA = 384
MLA_KV_LORA = 256
F_GROUPS = 4
F_GROUP_DIM = 64
F_WIDTH = F_GROUPS * F_GROUP_DIM
GQA_HEADS = 4
GQA_KV_HEADS = 2
GQA_GROUP = GQA_HEADS // GQA_KV_HEADS
GQA_HEAD_DIM = 64
WINDOW = 128
PEER_HEADS = 8
PEER_N_KEYS = 128
PEER_N_EXPERTS = PEER_N_KEYS * PEER_N_KEYS
PEER_HALF = 128
PEER_TOPK = 16

VMEM_LIMIT = 56 * 1024 * 1024

BF16 = jnp.bfloat16
F32 = jnp.float32
HIGHEST = lax.Precision.HIGHEST


def _cparams(sem):
    return pltpu.CompilerParams(dimension_semantics=sem, vmem_limit_bytes=VMEM_LIMIT)


def _rms(x, g):
    return x * lax.rsqrt(jnp.mean(x * x, axis=-1, keepdims=True) + EPS) * g


def _dot(a, b):
    return jnp.dot(a, b, preferred_element_type=F32)


def _dot_nt(a, b):
    return lax.dot_general(a, b, (((1,), (1,)), ((), ())), preferred_element_type=F32)


_C_CQ = (0, 384)
_C_CKV = (384, 640)
_C_KR = (640, 768)
_C_KRR = (768, 896)
_C_XF = (896, 1152)
_C_GQ = (1152, 1664)
_C_GQR = (1664, 2176)
_C_GK = (2176, 2432)
_C_GKR = (2432, 2688)
_C_GV = (2688, 2944)
_C_GATE = (2944, 6016)
_W1_COLS = 6016


def _in_proj_kernel(x_ref, g_ref, w1_ref, qng_ref, wuq_ref, wuqr_ref, kvng_ref, wukv_ref,
                    cm_ref, sm_ref, cg_ref, sg_ref,
                    q_ref, k_ref, v_ref, xf_ref, qg_ref, kg_ref, vg_ref, gate_ref):
    h = _rms(x_ref[...], g_ref[...]).astype(BF16)

    def proj(c):
        return _dot(h, w1_ref[:, c[0]:c[1]])

    cm, sm = cm_ref[...], sm_ref[...]
    cg, sg = cg_ref[...], sg_ref[...]

    cqn = _rms(proj(_C_CQ), qng_ref[...]).astype(BF16)
    q = _dot(cqn, wuq_ref[...])
    qr = _dot(cqn, wuqr_ref[...])
    mla_scale = (MLA_NOPE + MLA_ROPE) ** -0.5
    for hd in range(MLA_HEADS):
        sl = slice(hd * LANES, (hd + 1) * LANES)
        q_ref[:, sl] = ((q[:, sl] * cm + qr[:, sl] * sm) * mla_scale).astype(BF16)

    ckvn = _rms(proj(_C_CKV), kvng_ref[...]).astype(BF16)
    kv = _dot(ckvn, wukv_ref[...])
    k_rope = proj(_C_KR) * cm + proj(_C_KRR) * sm
    ones_col = (lax.broadcasted_iota(jnp.int32, (1, LANES), 1) == MLA_V).astype(F32)
    for hd in range(MLA_HEADS):
        sl = slice(hd * LANES, (hd + 1) * LANES)
        k_ref[:, sl] = (kv[:, sl] + k_rope).astype(BF16)
        vs = slice(MLA_HEADS * LANES + hd * LANES, MLA_HEADS * LANES + (hd + 1) * LANES)
        v_ref[:, sl] = (kv[:, vs] + ones_col).astype(BF16)

    xf_ref[...] = proj(_C_XF)

    gq, gqr = proj(_C_GQ), proj(_C_GQR)
    gqa_scale = GQA_HEAD_DIM ** -0.5
    for hd in range(GQA_HEADS):
        sl = slice(hd * LANES, (hd + 1) * LANES)
        qg_ref[:, sl] = ((gq[:, sl] * cg + gqr[:, sl] * sg) * gqa_scale).astype(BF16)
    gk, gkr = proj(_C_GK), proj(_C_GKR)
    for hd in range(GQA_KV_HEADS):
        sl = slice(hd * LANES, (hd + 1) * LANES)
        kg_ref[:, sl] = (gk[:, sl] * cg + gkr[:, sl] * sg).astype(BF16)
    vg_ref[...] = proj(_C_GV).astype(BF16)

    gate = proj(_C_GATE)
    gate_ref[...] = 1.0 / (1.0 + jnp.exp(-gate))


def _in_proj(x2, lw, tabs, S, tm):
    T = x2.shape[0]
    nt = T // tm
    ns = S // tm
    full = lambda shape: pl.BlockSpec(shape, lambda i: (0,) * len(shape))
    row = lambda w: pl.BlockSpec((tm, w), lambda i: (i, 0))
    tab = pl.BlockSpec((tm, LANES), lambda i: (i % ns, 0))
    outs = [(1024, BF16), (1024, BF16), (1024, BF16), (F_WIDTH, F32), (512, BF16), (256, BF16),
            (256, BF16), (3 * D_MODEL, F32)]
    return pl.pallas_call(
        _in_proj_kernel,
        grid=(nt,),
        in_specs=[row(D_MODEL), full((1, D_MODEL)), full((D_MODEL, _W1_COLS)),
                  full((1, MLA_Q_LORA)), full((MLA_Q_LORA, 1024)), full((MLA_Q_LORA, 1024)),
                  full((1, MLA_KV_LORA)), full((MLA_KV_LORA, 2048)),
                  tab, tab, tab, tab],
        out_specs=[row(w) for w, _ in outs],
        out_shape=[jax.ShapeDtypeStruct((T, w), dt) for w, dt in outs],
        compiler_params=_cparams(("parallel",)),
        name="in_proj",
    )(x2, lw["norm_mix_g"], lw["w1"], lw["q_norm_g"], lw["w_uq"], lw["w_uq_rot"],
      lw["kv_norm_g"], lw["w_ukv"], tabs["cm"], tabs["sm"], tabs["cg"], tabs["sg"])


def _mla_attn_kernel(q_ref, k_ref, v_ref, o_ref, *, tk):
    q = q_ref[...]
    tq = q.shape[0]
    nk = k_ref.shape[0] // tk

    def body(c, carry):
        m, acc = carry
        off = pl.multiple_of(c * tk, tk)
        s = _dot_nt(q, k_ref[pl.ds(off, tk), :])
        m_new = jnp.maximum(m, jnp.max(s, axis=-1, keepdims=True))
        p = jnp.exp(s - m_new)
        acc = acc * jnp.exp(m - m_new) + _dot(p.astype(BF16), v_ref[pl.ds(off, tk), :])
        return m_new, acc

    m0 = jnp.full((tq, 1), NEG, F32)
    acc0 = jnp.zeros((tq, LANES), F32)
    _, acc = lax.fori_loop(0, nk, body, (m0, acc0))
    o_ref[...] = (acc / acc[:, MLA_V:MLA_V + 1]).astype(o_ref.dtype)


def _mla_attn(q, k, v, B, S, tq, tk):
    q3, k3, v3 = (a.reshape(B, S, MLA_HEADS * LANES) for a in (q, k, v))
    out = pl.pallas_call(
        functools.partial(_mla_attn_kernel, tk=tk),
        grid=(B, MLA_HEADS, S // tq),
        in_specs=[pl.BlockSpec((None, tq, LANES), lambda b, h, i: (b, i, h)),
                  pl.BlockSpec((None, S, LANES), lambda b, h, i: (b, 0, h)),
                  pl.BlockSpec((None, S, LANES), lambda b, h, i: (b, 0, h))],
        out_specs=pl.BlockSpec((None, tq, LANES), lambda b, h, i: (b, i, h)),
        out_shape=jax.ShapeDtypeStruct((B, S, MLA_HEADS * LANES), BF16),
        compiler_params=_cparams(("parallel", "parallel", "parallel")),
        name="mla_attn",
    )(q3, k3, v3)
    return out.reshape(B * S, MLA_HEADS * LANES)


def _fft1_kernel(w_ref, x_ref, o_ref):
    o_ref[...] = jnp.dot(w_ref[...], x_ref[...], preferred_element_type=F32, precision=HIGHEST)


def _fft2_kernel(m_ref, a_ref, cc_ref, cs_ref, o_ref, *, scale):
    aa = jnp.concatenate([a_ref[0], a_ref[1]], axis=0)
    y = jnp.dot(m_ref[...], aa, preferred_element_type=F32, precision=HIGHEST)
    n2 = y.shape[0] // 2
    out = (jnp.dot(y[:n2], cc_ref[...], preferred_element_type=F32, precision=HIGHEST)
           + jnp.dot(y[n2:], cs_ref[...], preferred_element_type=F32, precision=HIGHEST))
    o_ref[...] = out * scale


def _fourier(xf, ft, B, S):
    n1, n2 = ft["n1"], ft["n2"]
    cols = n2 * F_WIDTH
    tc = min(cols, 4096)
    x3 = xf.reshape(B, n1, cols)
    a = pl.pallas_call(
        _fft1_kernel,
        grid=(B, cols // tc),
        in_specs=[pl.BlockSpec((2 * n1, n1), lambda b, j: (0, 0)),
                  pl.BlockSpec((None, n1, tc), lambda b, j: (b, 0, j))],
        out_specs=pl.BlockSpec((None, 2 * n1, tc), lambda b, j: (b, 0, j)),
        out_shape=jax.ShapeDtypeStruct((B, 2 * n1, cols), F32),
        compiler_params=_cparams(("parallel", "parallel")),
        name="fourier_stage1",
    )(ft["w1"], x3)
    a5 = a.reshape(B, 2, n1, n2, F_WIDTH)
    y = pl.pallas_call(
        functools.partial(_fft2_kernel, scale=1.0 / math.sqrt(S * F_GROUP_DIM)),
        grid=(B, n1),
        in_specs=[pl.BlockSpec((None, 2 * n2, 2 * n2), lambda b, k: (k, 0, 0)),
                  pl.BlockSpec((None, 2, None, n2, F_WIDTH), lambda b, k: (b, 0, k, 0, 0)),
                  pl.BlockSpec((F_WIDTH, F_WIDTH), lambda b, k: (0, 0)),
                  pl.BlockSpec((F_WIDTH, F_WIDTH), lambda b, k: (0, 0))],
        out_specs=pl.BlockSpec((None, None, n2, F_WIDTH), lambda b, k: (b, k, 0, 0)),
        out_shape=jax.ShapeDtypeStruct((B, n1, n2, F_WIDTH), F32),
        compiler_params=_cparams(("parallel", "parallel")),
        name="fourier_stage2",
    )(ft["m2"], a5, ft["cc"], ft["cs"])
    return jnp.transpose(y, (0, 2, 1, 3)).reshape(B * S, F_WIDTH)


def _gqa_kernel(sink_ref, q_ref, kp_ref, kc_ref, kn_ref, vp_ref, vc_ref, vn_ref, o_ref, *, S):
    n = pl.program_id(1)
    blk = q_ref.shape[0]
    qpos = n * blk + lax.broadcasted_iota(jnp.int32, (blk, 3 * blk), 0)
    kpos = (n - 1) * blk + lax.broadcasted_iota(jnp.int32, (blk, 3 * blk), 1)
    mask = (jnp.abs(qpos - kpos) <= WINDOW) & (kpos >= 0) & (kpos < S)
    for hd in range(GQA_HEADS):
        kh = hd // GQA_GROUP
        qs = slice(hd * LANES, (hd + 1) * LANES)
        ks = slice(kh * LANES, (kh + 1) * LANES)
        q = q_ref[:, qs]
        s = jnp.concatenate([_dot_nt(q, kp_ref[:, ks]), _dot_nt(q, kc_ref[:, ks]),
                             _dot_nt(q, kn_ref[:, ks])], axis=-1)
        s = jnp.where(mask, s, NEG)
        sk = sink_ref[hd]
        m = jnp.maximum(jnp.max(s, axis=-1, keepdims=True), sk)
        p = jnp.exp(s - m)
        den = jnp.sum(p, axis=-1, keepdims=True) + jnp.exp(sk - m)
        pb = (p / den).astype(BF16)
        o = (_dot(pb[:, :blk], vp_ref[:, ks]) + _dot(pb[:, blk:2 * blk], vc_ref[:, ks])
             + _dot(pb[:, 2 * blk:], vn_ref[:, ks]))
        o_ref[:, qs] = o.astype(o_ref.dtype)


def _gqa(qg, kg, vg, sink, B, S):
    blk = WINDOW
    nb = S // blk
    q3 = qg.reshape(B, S, GQA_HEADS * LANES)
    k3 = kg.reshape(B, S, GQA_KV_HEADS * LANES)
    v3 = vg.reshape(B, S, GQA_KV_HEADS * LANES)
    kvw = GQA_KV_HEADS * LANES
    prev = pl.BlockSpec((None, blk, kvw), lambda b, n: (b, jnp.maximum(n - 1, 0), 0))
    cur = pl.BlockSpec((None, blk, kvw), lambda b, n: (b, n, 0))
    nxt = pl.BlockSpec((None, blk, kvw), lambda b, n: (b, jnp.minimum(n + 1, nb - 1), 0))
    out = pl.pallas_call(
        functools.partial(_gqa_kernel, S=S),
        grid=(B, nb),
        in_specs=[pl.BlockSpec(memory_space=pltpu.SMEM),
                  pl.BlockSpec((None, blk, GQA_HEADS * LANES), lambda b, n: (b, n, 0)),
                  prev, cur, nxt, prev, cur, nxt],
        out_specs=pl.BlockSpec((None, blk, GQA_HEADS * LANES), lambda b, n: (b, n, 0)),
        out_shape=jax.ShapeDtypeStruct((B, S, GQA_HEADS * LANES), BF16),
        compiler_params=_cparams(("parallel", "parallel")),
        name="gqa_window",
    )(sink, q3, k3, k3, k3, v3, v3, v3)
    return out.reshape(B * S, GQA_HEADS * LANES)


def _merge_kernel(x_ref, attn_ref, four_ref, og_ref, gate_ref, wo_ref, fw_ref, fb_ref, gwo_ref,
                  wout_ref, o_ref):
    mla_o = _dot(attn_ref[...], wo_ref[...])
    fnet_o = _dot(four_ref[...].astype(BF16), fw_ref[...]) + fb_ref[...]
    gqa_o = _dot(og_ref[...], gwo_ref[...])
    d = D_MODEL
    merged = (gate_ref[:, 0:d] * mla_o + gate_ref[:, d:2 * d] * fnet_o
              + gate_ref[:, 2 * d:3 * d] * gqa_o)
    o_ref[...] = x_ref[...] + _dot(merged.astype(BF16), wout_ref[...])


def _merge(x2, attn, four, og, gate, lw, tm):
    T = x2.shape[0]
    full = lambda shape: pl.BlockSpec(shape, lambda i: (0,) * len(shape))
    row = lambda w: pl.BlockSpec((tm, w), lambda i: (i, 0))
    return pl.pallas_call(
        _merge_kernel,
        grid=(T // tm,),
        in_specs=[row(D_MODEL), row(1024), row(F_WIDTH), row(512), row(3 * D_MODEL),
                  full((1024, D_MODEL)), full((F_WIDTH, D_MODEL)), full((1, D_MODEL)),
                  full((512, D_MODEL)), full((D_MODEL, D_MODEL))],
        out_specs=row(D_MODEL),
        out_shape=jax.ShapeDtypeStruct((T, D_MODEL), F32),
        compiler_params=_cparams(("parallel",)),
        name="merge",
    )(x2, attn, four, og, gate, lw["mla_w_o"], lw["fnet_w"], lw["fnet_b"], lw["gqa_w_o"],
      lw["w_out"])


def _top_sorted(s, k):
    vals = []
    for _ in range(k):
        mx = jnp.max(s, axis=0, keepdims=True)
        vals.append(mx)
        s = jnp.where(s == mx, -jnp.inf, s)
    return jnp.concatenate(vals, axis=0)


def _peer_prep_kernel(x_ref, g_ref, wq_ref, keys_ref, hn_ref, st_ref, tau_ref, mp_ref):
    hn = _rms(x_ref[...], g_ref[...]).astype(BF16)
    hn_ref[...] = hn
    q = _dot(hn, wq_ref[...]).astype(BF16)
    taus, mps = [], []
    for hd in range(PEER_HEADS):
        tops = []
        for p in range(2):
            hp = 2 * hd + p
            st = _dot_nt(keys_ref[hp], q[:, hp * PEER_HALF:(hp + 1) * PEER_HALF])
            st_ref[hp] = st
            tops.append(_top_sorted(st, PEER_TOPK))
        a, b = tops
        cand = jnp.concatenate([a + b[r:r + 1] for r in range(PEER_TOPK)], axis=0)
        top = _top_sorted(cand, PEER_TOPK)
        m = top[0:1]
        z = jnp.sum(jnp.exp(top - m), axis=0, keepdims=True)
        taus.append(top[PEER_TOPK - 1:PEER_TOPK])
        mps.append(m + jnp.log(z))
    tau_ref[...] = jnp.concatenate(taus, axis=0)
    mp_ref[...] = jnp.concatenate(mps, axis=0)


def _peer_prep(x2, lw, tm):
    T = x2.shape[0]
    return pl.pallas_call(
        _peer_prep_kernel,
        grid=(T // tm,),
        in_specs=[pl.BlockSpec((tm, D_MODEL), lambda i: (i, 0)),
                  pl.BlockSpec((1, D_MODEL), lambda i: (0, 0)),
                  pl.BlockSpec((D_MODEL, 2 * PEER_HEADS * PEER_HALF), lambda i: (0, 0)),
                  pl.BlockSpec((2 * PEER_HEADS, PEER_N_KEYS, PEER_HALF), lambda i: (0, 0, 0))],
        out_specs=[pl.BlockSpec((tm, D_MODEL), lambda i: (i, 0)),
                   pl.BlockSpec((2 * PEER_HEADS, PEER_N_KEYS, tm), lambda i: (0, 0, i)),
                   pl.BlockSpec((PEER_HEADS, tm), lambda i: (0, i)),
                   pl.BlockSpec((PEER_HEADS, tm), lambda i: (0, i))],
        out_shape=[jax.ShapeDtypeStruct((T, D_MODEL), BF16),
                   jax.ShapeDtypeStruct((2 * PEER_HEADS, PEER_N_KEYS, T), F32),
                   jax.ShapeDtypeStruct((PEER_HEADS, T), F32),
                   jax.ShapeDtypeStruct((PEER_HEADS, T), F32)],
        compiler_params=_cparams(("parallel",)),
        name="peer_prep",
    )(x2, lw["norm_ffn_g"], lw["peer_w_q"], lw["peer_keys"])


def _peer_main_kernel(x_ref, hn_ref, st_ref, tau_ref, mp_ref, u_ref, vt_ref, fg_ref, o_ref,
                      acc_ref, *, n_i, final_norm):
    e = pl.program_id(1)

    @pl.when(e == 0)
    def _():
        acc_ref[...] = jnp.zeros_like(acc_ref)

    a = _dot_nt(u_ref[...], hn_ref[...])
    act = 0.5 * a * (1.0 + lax.erf(a * (1.0 / math.sqrt(2.0))))
    ws = []
    for ii in range(n_i):
        i = e * n_i + ii
        g = jnp.zeros((PEER_N_KEYS, a.shape[1]), F32)
        for hd in range(PEER_HEADS):
            z = st_ref[2 * hd + 1] + st_ref[2 * hd, pl.ds(i, 1), :]
            g = g + jnp.where(z >= tau_ref[hd:hd + 1, :], jnp.exp(z - mp_ref[hd:hd + 1, :]), 0.0)
        ws.append((g * act[ii * PEER_N_KEYS:(ii + 1) * PEER_N_KEYS]).astype(BF16))
    w = jnp.concatenate(ws, axis=0)
    acc_ref[...] += _dot(vt_ref[...], w)

    @pl.when(e == pl.num_programs(1) - 1)
    def _():
        y = x_ref[...] + acc_ref[...].T
        if final_norm:
            y = _rms(y, fg_ref[...])
        o_ref[...] = y


def _peer_main(x2, hn, st, tau, mp, lw, final_g, tm, n_i, final_norm):
    T = x2.shape[0]
    et = n_i * PEER_N_KEYS
    return pl.pallas_call(
        functools.partial(_peer_main_kernel, n_i=n_i, final_norm=final_norm),
        grid=(T // tm, PEER_N_EXPERTS // et),
        in_specs=[pl.BlockSpec((tm, D_MODEL), lambda t, e: (t, 0)),
                  pl.BlockSpec((tm, D_MODEL), lambda t, e: (t, 0)),
                  pl.BlockSpec((2 * PEER_HEADS, PEER_N_KEYS, tm), lambda t, e: (0, 0, t)),
                  pl.BlockSpec((PEER_HEADS, tm), lambda t, e: (0, t)),
                  pl.BlockSpec((PEER_HEADS, tm), lambda t, e: (0, t)),
                  pl.BlockSpec((et, D_MODEL), lambda t, e: (e, 0)),
                  pl.BlockSpec((D_MODEL, et), lambda t, e: (0, e)),
                  pl.BlockSpec((1, D_MODEL), lambda t, e: (0, 0))],
        out_specs=pl.BlockSpec((tm, D_MODEL), lambda t, e: (t, 0)),
        out_shape=jax.ShapeDtypeStruct((T, D_MODEL), F32),
        scratch_shapes=[pltpu.VMEM((D_MODEL, tm), F32)],
        compiler_params=_cparams(("parallel", "arbitrary")),
        name="peer_main",
    )(x2, hn, st, tau, mp, lw["peer_u"], lw["peer_vt"], final_g)


def _rot_half_cols(w):
    d = w.shape[-1]
    return jnp.concatenate([-w[..., d // 2:], w[..., :d // 2]], axis=-1)


def _pad_cols(w, left, total):
    return jnp.pad(w, ((0, 0), (left, total - left - w.shape[-1])))


def _head_groups(w, n_heads, width, left=0):
    return jnp.concatenate(
        [_pad_cols(w[:, h * width:(h + 1) * width], left, LANES) for h in range(n_heads)], axis=-1)


def _layer_weights(p, l):
    w_in = p["w_in"][l]
    widths = (MLA_Q_LORA, MLA_KV_LORA, MLA_ROPE, F_WIDTH, GQA_HEADS * GQA_HEAD_DIM,
              GQA_KV_HEADS * GQA_HEAD_DIM, GQA_KV_HEADS * GQA_HEAD_DIM, 3 * D_MODEL)
    offs = np.cumsum((0,) + widths)
    wcq, wckv, wkr, wxf, wgq, wgk, wgv, wgate = (w_in[:, offs[i]:offs[i + 1]] for i in range(8))

    def rot_heads(w, n_heads, width):
        return jnp.concatenate(
            [_rot_half_cols(w[:, h * width:(h + 1) * width]) for h in range(n_heads)], axis=-1)

    w1 = jnp.concatenate([
        wcq, wckv,
        _pad_cols(wkr, MLA_NOPE, LANES), _pad_cols(_rot_half_cols(wkr), MLA_NOPE, LANES),
        wxf,
        _head_groups(wgq, GQA_HEADS, GQA_HEAD_DIM),
        _head_groups(rot_heads(wgq, GQA_HEADS, GQA_HEAD_DIM), GQA_HEADS, GQA_HEAD_DIM),
        _head_groups(wgk, GQA_KV_HEADS, GQA_HEAD_DIM),
        _head_groups(rot_heads(wgk, GQA_KV_HEADS, GQA_HEAD_DIM), GQA_KV_HEADS, GQA_HEAD_DIM),
        _head_groups(wgv, GQA_KV_HEADS, GQA_HEAD_DIM),
        wgate], axis=-1).astype(BF16)
    assert w1.shape[-1] == _W1_COLS

    w_uq = p["mla_w_uq"][l]
    qd = MLA_NOPE + MLA_ROPE
    uq, uqr = [], []
    for h in range(MLA_HEADS):
        blk = w_uq[:, h * qd:(h + 1) * qd]
        uq.append(_pad_cols(blk, 0, LANES))
        uqr.append(_pad_cols(_rot_half_cols(blk[:, MLA_NOPE:]), MLA_NOPE, LANES))
    w_ukv = p["mla_w_ukv"][l]
    kd = MLA_NOPE + MLA_V
    uk = [_pad_cols(w_ukv[:, h * kd:h * kd + MLA_NOPE], 0, LANES) for h in range(MLA_HEADS)]
    uv = [_pad_cols(w_ukv[:, h * kd + MLA_NOPE:(h + 1) * kd], 0, LANES) for h in range(MLA_HEADS)]

    def pad_rows(w, n_heads, width):
        return jnp.concatenate(
            [jnp.pad(w[h * width:(h + 1) * width], ((0, LANES - width), (0, 0)))
             for h in range(n_heads)], axis=0)

    return {
        "norm_mix_g": p["norm_mix_g"][l][None, :],
        "w1": w1,
        "q_norm_g": p["mla_q_norm_g"][l][None, :],
        "w_uq": jnp.concatenate(uq, axis=-1).astype(BF16),
        "w_uq_rot": jnp.concatenate(uqr, axis=-1).astype(BF16),
        "kv_norm_g": p["mla_kv_norm_g"][l][None, :],
        "w_ukv": jnp.concatenate(uk + uv, axis=-1).astype(BF16),
        "mla_w_o": pad_rows(p["mla_w_o"][l], MLA_HEADS, MLA_V).astype(BF16),
        "fnet_w": p["fnet_w"][l].astype(BF16),
        "fnet_b": p["fnet_b"][l][None, :],
        "gqa_sink": p["gqa_sink"][l],
        "gqa_w_o": pad_rows(p["gqa_w_o"][l], GQA_HEADS, GQA_HEAD_DIM).astype(BF16),
        "w_out": p["w_out"][l].astype(BF16),
        "norm_ffn_g": p["norm_ffn_g"][l][None, :],
        "peer_w_q": p["peer_w_q"][l].astype(BF16),
        "peer_keys": p["peer_keys"][l].reshape(2 * PEER_HEADS, PEER_N_KEYS, PEER_HALF).astype(BF16),
        "peer_u": p["peer_u"][l].astype(BF16),
        "peer_vt": p["peer_v"][l].astype(BF16).T,
    }


def _rope_tables(S):
    pos = jnp.arange(S, dtype=F32)[:, None]

    def cs(d):
        inv = 1.0 / (ROPE_THETA ** (jnp.arange(0, d, 2, dtype=F32) / d))
        ang = pos * inv[None, :]
        return (jnp.concatenate([jnp.cos(ang)] * 2, axis=-1),
                jnp.concatenate([jnp.sin(ang)] * 2, axis=-1))

    c, s = cs(MLA_ROPE)
    rest = LANES - MLA_NOPE - MLA_ROPE
    cm = jnp.concatenate([jnp.ones((S, MLA_NOPE), F32), c, jnp.zeros((S, rest), F32)], axis=-1)
    sm = jnp.concatenate([jnp.zeros((S, MLA_NOPE), F32), s, jnp.zeros((S, rest), F32)], axis=-1)
    c, s = cs(GQA_HEAD_DIM)
    zero = jnp.zeros((S, LANES - GQA_HEAD_DIM), F32)
    return {"cm": cm, "sm": sm, "cg": jnp.concatenate([c, zero], axis=-1),
            "sg": jnp.concatenate([s, zero], axis=-1)}


def _fourier_tables(S):
    n2 = 128
    n1 = S // n2

    def cos_sin(num, den):
        ang = (2.0 * math.pi / den) * (num % den).astype(F32)
        return jnp.cos(ang), jnp.sin(ang)

    i1 = jnp.arange(n1, dtype=jnp.int32)
    c1, s1 = cos_sin(i1[:, None] * i1[None, :], n1)
    w1 = jnp.concatenate([c1, -s1], axis=0)
    i2 = jnp.arange(n2, dtype=jnp.int32)
    kk = i1[:, None, None] + n1 * i2[None, :, None]
    mr, ms = cos_sin(kk * i2[None, None, :], S)
    m2 = jnp.concatenate([jnp.concatenate([mr, ms], axis=-1),
                          jnp.concatenate([-ms, mr], axis=-1)], axis=1)
    ic = jnp.arange(F_GROUP_DIM, dtype=jnp.int32)
    cc, cs = cos_sin(ic[:, None] * ic[None, :], F_GROUP_DIM)
    eye = jnp.eye(F_GROUPS, dtype=F32)
    return {"n1": n1, "n2": n2, "w1": w1, "m2": m2, "cc": jnp.kron(eye, cc),
            "cs": jnp.kron(eye, cs)}


def _pick(S, pref):
    t = pref
    while S % t:
        t //= 2
    return t


def _trunk(x, layers, final_g, rope_t, four_t):
    B, S, _ = x.shape
    T = B * S
    x2 = x.reshape(T, D_MODEL)
    tm = _pick(S, 256)
    n_layers = len(layers)
    for l, lw in enumerate(layers):
        q, k, v, xf, qg, kg, vg, gate = _in_proj(x2, lw, rope_t, S, tm)
        attn = _mla_attn(q, k, v, B, S, _pick(S, 256), _pick(S, 512))
        four = _fourier(xf, four_t, B, S)
        og = _gqa(qg, kg, vg, lw["gqa_sink"], B, S)
        x2 = _merge(x2, attn, four, og, gate, lw, tm)
        hn, st, tau, mp = _peer_prep(x2, lw, _pick(T, 256))
        x2 = _peer_main(x2, hn, st, tau, mp, lw, final_g, _pick(T, 512), 4,
                        final_norm=(l == n_layers - 1))
    return x2.reshape(B, S, D_MODEL)


def kernel(x_prompt, x_sample, norm_mix_g, w_in, mla_q_norm_g, mla_w_uq, mla_kv_norm_g, mla_w_ukv, mla_w_o, fnet_w, fnet_b, gqa_sink, gqa_w_o, w_out, norm_ffn_g, peer_w_q, peer_keys, peer_u, peer_v, final_norm_g):
    p = dict(norm_mix_g=norm_mix_g, w_in=w_in, mla_q_norm_g=mla_q_norm_g, mla_w_uq=mla_w_uq,
             mla_kv_norm_g=mla_kv_norm_g, mla_w_ukv=mla_w_ukv, mla_w_o=mla_w_o, fnet_w=fnet_w,
             fnet_b=fnet_b, gqa_sink=gqa_sink, gqa_w_o=gqa_w_o, w_out=w_out,
             norm_ffn_g=norm_ffn_g, peer_w_q=peer_w_q, peer_keys=peer_keys, peer_u=peer_u,
             peer_v=peer_v)
    layers = [_layer_weights(p, l) for l in range(w_in.shape[0])]
    final_g = final_norm_g[None, :]
    outs = []
    for x in (x_prompt, x_sample):
        S = x.shape[1]
        outs.append(_trunk(x, layers, final_g, _rope_tables(S), _fourier_tables(S)))
    return tuple(outs)
```

```python
import functools
import math

import jax
import jax.numpy as jnp
import numpy as np
from jax import lax
from jax.experimental import pallas as pl
from jax.experimental.pallas import tpu as pltpu

D_MODEL = 1024
EPS = 1e-6
ROPE_THETA = 10000.0
NEG = -1e30
LANES = 128

MLA_HEADS = 8
MLA_NOPE = 64
MLA_ROPE = 32
MLA_V = 64
MLA_Q_LORA = 384
MLA_KV_LORA = 256
F_GROUPS = 4
F_GROUP_DIM = 64
F_WIDTH = F_GROUPS * F_GROUP_DIM
GQA_HEADS = 4
GQA_KV_HEADS = 2
GQA_GROUP = GQA_HEADS // GQA_KV_HEADS
GQA_HEAD_DIM = 64
WINDOW = 128
PEER_HEADS = 8
PEER_N_KEYS = 128
PEER_N_EXPERTS = PEER_N_KEYS * PEER_N_KEYS
PEER_HALF = 128
PEER_TOPK = 16

VMEM_LIMIT = 56 * 1024 * 1024

BF16 = jnp.bfloat16
F32 = jnp.float32
HIGHEST = lax.Precision.HIGHEST


def _cparams(sem):
    return pltpu.CompilerParams(dimension_semantics=sem, vmem_limit_bytes=VMEM_LIMIT)


def _rms(x, g):
    return x * lax.rsqrt(jnp.mean(x * x, axis=-1, keepdims=True) + EPS) * g


def _dot(a, b):
    return jnp.dot(a, b, preferred_element_type=F32)


def _dot_nt(a, b):
    return lax.dot_general(a, b, (((1,), (1,)), ((), ())), preferred_element_type=F32)


_C_CQ = (0, 384)
_C_CKV = (384, 640)
_C_KR = (640, 768)
_C_KRR = (768, 896)
_C_XF = (896, 1152)
_C_GQ = (1152, 1664)
_C_GQR = (1664, 2176)
_C_GK = (2176, 2432)
_C_GKR = (2432, 2688)
_C_GV = (2688, 2944)
_C_GATE = (2944, 6016)
_W1_COLS = 6016


def _in_proj_kernel(x_ref, g_ref, w1_ref, qng_ref, wuq_ref, wuqr_ref, kvng_ref, wukv_ref,
                    cm_ref, sm_ref, cg_ref, sg_ref,
                    q_ref, k_ref, v_ref, xf_ref, qg_ref, kg_ref, vg_ref, gate_ref):
    h = _rms(x_ref[...], g_ref[...]).astype(BF16)

    def proj(c):
        return _dot(h, w1_ref[:, c[0]:c[1]])

    cm, sm = cm_ref[...], sm_ref[...]
    cg, sg = cg_ref[...], sg_ref[...]

    cqn = _rms(proj(_C_CQ), qng_ref[...]).astype(BF16)
    q = _dot(cqn, wuq_ref[...])
    qr = _dot(cqn, wuqr_ref[...])
    mla_scale = (MLA_NOPE + MLA_ROPE) ** -0.5 * math.log2(math.e)
    for hd in range(MLA_HEADS):
        sl = slice(hd * LANES, (hd + 1) * LANES)
        q_ref[:, sl] = ((q[:, sl] * cm + qr[:, sl] * sm) * mla_scale).astype(BF16)

    ckvn = _rms(proj(_C_CKV), kvng_ref[...]).astype(BF16)
    kv = _dot(ckvn, wukv_ref[...])
    k_rope = proj(_C_KR) * cm + proj(_C_KRR) * sm
    ones_col = (lax.broadcasted_iota(jnp.int32, (1, LANES), 1) == MLA_V).astype(F32)
    for hd in range(MLA_HEADS):
        sl = slice(hd * LANES, (hd + 1) * LANES)
        k_ref[:, sl] = (kv[:, sl] + k_rope).astype(BF16)
        vs = slice(MLA_HEADS * LANES + hd * LANES, MLA_HEADS * LANES + (hd + 1) * LANES)
        v_ref[:, sl] = (kv[:, vs] + ones_col).astype(BF16)

    xf_ref[...] = proj(_C_XF)

    gq, gqr = proj(_C_GQ), proj(_C_GQR)
    gqa_scale = GQA_HEAD_DIM ** -0.5
    for hd in range(GQA_HEADS):
        sl = slice(hd * LANES, (hd + 1) * LANES)
        qg_ref[:, sl] = ((gq[:, sl] * cg + gqr[:, sl] * sg) * gqa_scale).astype(BF16)
    gk, gkr = proj(_C_GK), proj(_C_GKR)
    for hd in range(GQA_KV_HEADS):
        sl = slice(hd * LANES, (hd + 1) * LANES)
        kg_ref[:, sl] = (gk[:, sl] * cg + gkr[:, sl] * sg).astype(BF16)
    vg_ref[...] = proj(_C_GV).astype(BF16)

    gate = proj(_C_GATE)
    gate_ref[...] = 1.0 / (1.0 + jnp.exp(-gate))


def _in_proj(x2, lw, tabs, S, tm):
    T = x2.shape[0]
    nt = T // tm
    ns = S // tm
    full = lambda shape: pl.BlockSpec(shape, lambda i: (0,) * len(shape))
    row = lambda w: pl.BlockSpec((tm, w), lambda i: (i, 0))
    tab = pl.BlockSpec((tm, LANES), lambda i: (i % ns, 0))
    outs = [(1024, BF16), (1024, BF16), (1024, BF16), (F_WIDTH, F32), (512, BF16), (256, BF16),
            (256, BF16), (3 * D_MODEL, F32)]
    return pl.pallas_call(
        _in_proj_kernel,
        grid=(nt,),
        in_specs=[row(D_MODEL), full((1, D_MODEL)), full((D_MODEL, _W1_COLS)),
                  full((1, MLA_Q_LORA)), full((MLA_Q_LORA, 1024)), full((MLA_Q_LORA, 1024)),
                  full((1, MLA_KV_LORA)), full((MLA_KV_LORA, 2048)),
                  tab, tab, tab, tab],
        out_specs=[row(w) for w, _ in outs],
        out_shape=[jax.ShapeDtypeStruct((T, w), dt) for w, dt in outs],
        compiler_params=_cparams(("parallel",)),
        name="in_proj",
    )(x2, lw["norm_mix_g"], lw["w1"], lw["q_norm_g"], lw["w_uq"], lw["w_uq_rot"],
      lw["kv_norm_g"], lw["w_ukv"], tabs["cm"], tabs["sm"], tabs["cg"], tabs["sg"])


def _mla_attn_kernel(q_ref, k_ref, v_ref, o_ref, s_ref, p_ref, *, tk):
    tq = q_ref.shape[0]
    nk = k_ref.shape[0] // tk
    q = q_ref[...]

    def chunk(ref, c):
        return ref[pl.ds(pl.multiple_of(c * tk, tk), tk), :]

    def step(c, slot, carry):
        m, alpha, acc = carry
        pv = _dot(p_ref[1 - slot], chunk(v_ref, jnp.maximum(c - 1, 0)))
        s_ref[1 - slot] = _dot_nt(q, chunk(k_ref, jnp.minimum(c + 1, nk - 1)))
        s = s_ref[slot]
        m_new = jnp.maximum(m, jnp.max(s, axis=-1, keepdims=True))
        p_ref[slot] = jnp.exp2(s - m_new).astype(BF16)
        return m_new, jnp.exp2(m - m_new), acc * alpha + pv

    def body(c2, carry):
        carry = step(2 * c2, 0, carry)
        return step(2 * c2 + 1, 1, carry)

    s_ref[0] = _dot_nt(q, chunk(k_ref, 0))
    p_ref[1] = jnp.zeros(p_ref.shape[1:], BF16)
    init = (jnp.full((tq, 1), NEG, F32), jnp.ones((tq, 1), F32), jnp.zeros((tq, LANES), F32))
    _, alpha, acc = lax.fori_loop(0, nk // 2, body, init)
    acc = acc * alpha + _dot(p_ref[1], chunk(v_ref, nk - 1))
    o_ref[...] = (acc / acc[:, MLA_V:MLA_V + 1]).astype(o_ref.dtype)


def _mla_attn(q, k, v, B, S, tq, tk):
    assert (S // tk) % 2 == 0
    q3, k3, v3 = (a.reshape(B, S, MLA_HEADS * LANES) for a in (q, k, v))
    out = pl.pallas_call(
        functools.partial(_mla_attn_kernel, tk=tk),
        scratch_shapes=[pltpu.VMEM((2, tq, tk), F32), pltpu.VMEM((2, tq, tk), BF16)],
        grid=(B, MLA_HEADS, S // tq),
        in_specs=[pl.BlockSpec((None, tq, LANES), lambda b, h, i: (b, i, h)),
                  pl.BlockSpec((None, S, LANES), lambda b, h, i: (b, 0, h)),
                  pl.BlockSpec((None, S, LANES), lambda b, h, i: (b, 0, h))],
        out_specs=pl.BlockSpec((None, tq, LANES), lambda b, h, i: (b, i, h)),
        out_shape=jax.ShapeDtypeStruct((B, S, MLA_HEADS * LANES), BF16),
        compiler_params=_cparams(("parallel", "parallel", "parallel")),
        name="mla_attn",
    )(q3, k3, v3)
    return out.reshape(B * S, MLA_HEADS * LANES)


def _fft1_kernel(w_ref, x_ref, o_ref):
    o_ref[...] = jnp.dot(w_ref[...], x_ref[...], preferred_element_type=F32, precision=HIGHEST)


def _fft2_kernel(m_ref, a_ref, cc_ref, cs_ref, o_ref, *, scale):
    aa = jnp.concatenate([a_ref[0], a_ref[1]], axis=0)
    y = jnp.dot(m_ref[...], aa, preferred_element_type=F32, precision=HIGHEST)
    n2 = y.shape[0] // 2
    out = (jnp.dot(y[:n2], cc_ref[...], preferred_element_type=F32, precision=HIGHEST)
           + jnp.dot(y[n2:], cs_ref[...], preferred_element_type=F32, precision=HIGHEST))
    o_ref[...] = out * scale


def _fourier(xf, ft, B, S):
    n1, n2 = ft["n1"], ft["n2"]
    cols = n2 * F_WIDTH
    tc = min(cols, 4096)
    x3 = xf.reshape(B, n1, cols)
    a = pl.pallas_call(
        _fft1_kernel,
        grid=(B, cols // tc),
        in_specs=[pl.BlockSpec((2 * n1, n1), lambda b, j: (0, 0)),
                  pl.BlockSpec((None, n1, tc), lambda b, j: (b, 0, j))],
        out_specs=pl.BlockSpec((None, 2 * n1, tc), lambda b, j: (b, 0, j)),
        out_shape=jax.ShapeDtypeStruct((B, 2 * n1, cols), F32),
        compiler_params=_cparams(("parallel", "parallel")),
        name="fourier_stage1",
    )(ft["w1"], x3)
    a5 = a.reshape(B, 2, n1, n2, F_WIDTH)
    y = pl.pallas_call(
        functools.partial(_fft2_kernel, scale=1.0 / math.sqrt(S * F_GROUP_DIM)),
        grid=(B, n1),
        in_specs=[pl.BlockSpec((None, 2 * n2, 2 * n2), lambda b, k: (k, 0, 0)),
                  pl.BlockSpec((None, 2, None, n2, F_WIDTH), lambda b, k: (b, 0, k, 0, 0)),
                  pl.BlockSpec((F_WIDTH, F_WIDTH), lambda b, k: (0, 0)),
                  pl.BlockSpec((F_WIDTH, F_WIDTH), lambda b, k: (0, 0))],
        out_specs=pl.BlockSpec((None, None, n2, F_WIDTH), lambda b, k: (b, k, 0, 0)),
        out_shape=jax.ShapeDtypeStruct((B, n1, n2, F_WIDTH), F32),
        compiler_params=_cparams(("parallel", "parallel")),
        name="fourier_stage2",
    )(ft["m2"], a5, ft["cc"], ft["cs"])
    return jnp.transpose(y, (0, 2, 1, 3)).reshape(B * S, F_WIDTH)


def _gqa_kernel(sink_ref, q_ref, kp_ref, kc_ref, kn_ref, vp_ref, vc_ref, vn_ref, o_ref, *, S):
    n = pl.program_id(1)
    blk = q_ref.shape[0]
    qpos = n * blk + lax.broadcasted_iota(jnp.int32, (blk, 3 * blk), 0)
    kpos = (n - 1) * blk + lax.broadcasted_iota(jnp.int32, (blk, 3 * blk), 1)
    mask = (jnp.abs(qpos - kpos) <= WINDOW) & (kpos >= 0) & (kpos < S)
    for hd in range(GQA_HEADS):
        kh = hd // GQA_GROUP
        qs = slice(hd * LANES, (hd + 1) * LANES)
        ks = slice(kh * LANES, (kh + 1) * LANES)
        q = q_ref[:, qs]
        s = jnp.concatenate([_dot_nt(q, kp_ref[:, ks]), _dot_nt(q, kc_ref[:, ks]),
                             _dot_nt(q, kn_ref[:, ks])], axis=-1)
        s = jnp.where(mask, s, NEG)
        sk = sink_ref[hd]
        m = jnp.maximum(jnp.max(s, axis=-1, keepdims=True), sk)
        p = jnp.exp(s - m)
        den = jnp.sum(p, axis=-1, keepdims=True) + jnp.exp(sk - m)
        pb = (p / den).astype(BF16)
        o = (_dot(pb[:, :blk], vp_ref[:, ks]) + _dot(pb[:, blk:2 * blk], vc_ref[:, ks])
             + _dot(pb[:, 2 * blk:], vn_ref[:, ks]))
        o_ref[:, qs] = o.astype(o_ref.dtype)


def _gqa(qg, kg, vg, sink, B, S):
    blk = WINDOW
    nb = S // blk
    q3 = qg.reshape(B, S, GQA_HEADS * LANES)
    k3 = kg.reshape(B, S, GQA_KV_HEADS * LANES)
    v3 = vg.reshape(B, S, GQA_KV_HEADS * LANES)
    kvw = GQA_KV_HEADS * LANES
    prev = pl.BlockSpec((None, blk, kvw), lambda b, n: (b, jnp.maximum(n - 1, 0), 0))
    cur = pl.BlockSpec((None, blk, kvw), lambda b, n: (b, n, 0))
    nxt = pl.BlockSpec((None, blk, kvw), lambda b, n: (b, jnp.minimum(n + 1, nb - 1), 0))
    out = pl.pallas_call(
        functools.partial(_gqa_kernel, S=S),
        grid=(B, nb),
        in_specs=[pl.BlockSpec(memory_space=pltpu.SMEM),
                  pl.BlockSpec((None, blk, GQA_HEADS * LANES), lambda b, n: (b, n, 0)),
                  prev, cur, nxt, prev, cur, nxt],
        out_specs=pl.BlockSpec((None, blk, GQA_HEADS * LANES), lambda b, n: (b, n, 0)),
        out_shape=jax.ShapeDtypeStruct((B, S, GQA_HEADS * LANES), BF16),
        compiler_params=_cparams(("parallel", "parallel")),
        name="gqa_window",
    )(sink, q3, k3, k3, k3, v3, v3, v3)
    return out.reshape(B * S, GQA_HEADS * LANES)


def _merge_kernel(x_ref, attn_ref, four_ref, og_ref, gate_ref, wo_ref, fw_ref, fb_ref, gwo_ref,
                  wout_ref, o_ref):
    mla_o = _dot(attn_ref[...], wo_ref[...])
    fnet_o = _dot(four_ref[...].astype(BF16), fw_ref[...]) + fb_ref[...]
    gqa_o = _dot(og_ref[...], gwo_ref[...])
    d = D_MODEL
    merged = (gate_ref[:, 0:d] * mla_o + gate_ref[:, d:2 * d] * fnet_o
              + gate_ref[:, 2 * d:3 * d] * gqa_o)
    o_ref[...] = x_ref[...] + _dot(merged.astype(BF16), wout_ref[...])


def _merge(x2, attn, four, og, gate, lw, tm):
    T = x2.shape[0]
    full = lambda shape: pl.BlockSpec(shape, lambda i: (0,) * len(shape))
    row = lambda w: pl.BlockSpec((tm, w), lambda i: (i, 0))
    return pl.pallas_call(
        _merge_kernel,
        grid=(T // tm,),
        in_specs=[row(D_MODEL), row(1024), row(F_WIDTH), row(512), row(3 * D_MODEL),
                  full((1024, D_MODEL)), full((F_WIDTH, D_MODEL)), full((1, D_MODEL)),
                  full((512, D_MODEL)), full((D_MODEL, D_MODEL))],
        out_specs=row(D_MODEL),
        out_shape=jax.ShapeDtypeStruct((T, D_MODEL), F32),
        compiler_params=_cparams(("parallel",)),
        name="merge",
    )(x2, attn, four, og, gate, lw["mla_w_o"], lw["fnet_w"], lw["fnet_b"], lw["gqa_w_o"],
      lw["w_out"])


LOG2E = math.log2(math.e)
PEER_BLK = 256
PEER_RANKS = PEER_TOPK + 1


def _top_sorted(s, k):
    vals = []
    for _ in range(k):
        mx = jnp.max(s, axis=0, keepdims=True)
        vals.append(mx)
        s = jnp.where(s == mx, -jnp.inf, s)
    return vals


def _peer_prep_kernel(x_ref, g_ref, wq_ref, keys_ref, hn_ref, st1_ref, s2k_ref, kap_ref):
    hn = _rms(x_ref[...], g_ref[...]).astype(BF16)
    hn_ref[...] = hn
    q = _dot(hn, wq_ref[...]).astype(BF16)
    tm = q.shape[0]
    kaps = []
    for hd in range(PEER_HEADS):
        sts, tops = [], []
        for p in range(2):
            hp = 2 * hd + p
            st = _dot_nt(keys_ref[hp], q[:, hp * PEER_HALF:(hp + 1) * PEER_HALF])
            sts.append(st)
            tops.append(_top_sorted(st, PEER_RANKS))
        a, b = tops
        a_lo = jnp.concatenate(a[:8], axis=0)
        a_hi = jnp.concatenate(a[8:] + [jnp.full((7, tm), -jnp.inf, F32)], axis=0)
        cand = jnp.concatenate([a_lo + b[r] for r in range(PEER_RANKS)] + [a_hi + b[0]], axis=0)
        top = _top_sorted(cand, PEER_RANKS)
        m = top[0]
        z = sum(jnp.exp(t - m) for t in top[:PEER_TOPK])
        mp = m + jnp.log(z)
        thr = 0.5 * (top[PEER_TOPK - 1] + top[PEER_TOPK])
        st1_ref[hd] = sts[0] * LOG2E
        s2k_ref[hd] = (sts[1] - mp) * LOG2E
        kaps.append((thr - mp) * LOG2E)
    kap_ref[...] = jnp.concatenate(kaps, axis=0)


def _peer_prep(x2, lw, tm):
    T = x2.shape[0]
    sc = pl.BlockSpec((PEER_HEADS, PEER_N_KEYS, tm), lambda i: (0, 0, i))
    return pl.pallas_call(
        _peer_prep_kernel,
        grid=(T // tm,),
        in_specs=[pl.BlockSpec((tm, D_MODEL), lambda i: (i, 0)),
                  pl.BlockSpec((1, D_MODEL), lambda i: (0, 0)),
                  pl.BlockSpec((D_MODEL, 2 * PEER_HEADS * PEER_HALF), lambda i: (0, 0)),
                  pl.BlockSpec((2 * PEER_HEADS, PEER_N_KEYS, PEER_HALF), lambda i: (0, 0, 0))],
        out_specs=[pl.BlockSpec((tm, D_MODEL), lambda i: (i, 0)), sc, sc,
                   pl.BlockSpec((PEER_HEADS, tm), lambda i: (0, i))],
        out_shape=[jax.ShapeDtypeStruct((T, D_MODEL), BF16),
                   jax.ShapeDtypeStruct((PEER_HEADS, PEER_N_KEYS, T), F32),
                   jax.ShapeDtypeStruct((PEER_HEADS, PEER_N_KEYS, T), F32),
                   jax.ShapeDtypeStruct((PEER_HEADS, T), F32)],
        compiler_params=_cparams(("parallel",)),
        name="peer_prep",
    )(x2, lw["norm_ffn_g"], lw["peer_w_q"], lw["peer_keys"])


def _peer_main_kernel(x_ref, hn_ref, st1_ref, s2k_ref, kap_ref, u_ref, vt_ref, fg_ref, o_ref,
                      acc_ref, act_ref, w_ref, *, n_i, final_norm):
    e = pl.program_id(1)
    tm = hn_ref.shape[0]

    @pl.when(e == 0)
    def _():
        acc_ref[...] = jnp.zeros_like(acc_ref)

    n_blk = n_i * PEER_N_KEYS // PEER_BLK
    hn = hn_ref[...]

    def activations(p):
        rows = pl.ds(pl.multiple_of(p * PEER_BLK, PEER_BLK), PEER_BLK)
        a = _dot_nt(u_ref[rows, :], hn)
        return 0.5 * a * (1.0 + lax.erf(a * (1.0 / math.sqrt(2.0))))

    def step(p, slot):
        act_ref[1 - slot] = activations(jnp.minimum(p + 1, n_blk - 1))
        acc_ref[...] += _dot(vt_ref[jnp.maximum(p - 1, 0)], w_ref[1 - slot])
        for ii in range(PEER_BLK // PEER_N_KEYS):
            i = e * n_i + p * (PEER_BLK // PEER_N_KEYS) + ii
            rows = slice(ii * PEER_N_KEYS, (ii + 1) * PEER_N_KEYS)
            s1 = [st1_ref[hd, pl.ds(i, 1), :] for hd in range(PEER_HEADS)]
            for c in range(tm // LANES):
                ls = slice(c * LANES, (c + 1) * LANES)
                g = None
                for hd in range(PEER_HEADS):
                    z = s2k_ref[hd, :, ls] + s1[hd][:, ls]
                    t = jnp.where(z > kap_ref[hd:hd + 1, ls], jnp.exp2(z), 0.0)
                    g = t if g is None else g + t
                w_ref[slot, rows, ls] = (g * act_ref[slot, rows, ls]).astype(BF16)

    def body(p2, carry):
        step(2 * p2, 0)
        step(2 * p2 + 1, 1)
        return carry

    act_ref[0] = activations(0)
    w_ref[1] = jnp.zeros(w_ref.shape[1:], BF16)
    lax.fori_loop(0, n_blk // 2, body, 0)
    acc_ref[...] += _dot(vt_ref[n_blk - 1], w_ref[1])

    @pl.when(e == pl.num_programs(1) - 1)
    def _():
        y = x_ref[...] + acc_ref[...].T
        if final_norm:
            y = _rms(y, fg_ref[...])
        o_ref[...] = y


def _peer_main(x2, hn, st1, s2k, kap, lw, final_g, tm, n_i, final_norm):
    T = x2.shape[0]
    et = n_i * PEER_N_KEYS
    sc = pl.BlockSpec((PEER_HEADS, PEER_N_KEYS, tm), lambda t, e: (0, 0, t))
    return pl.pallas_call(
        functools.partial(_peer_main_kernel, n_i=n_i, final_norm=final_norm),
        grid=(T // tm, PEER_N_EXPERTS // et),
        in_specs=[pl.BlockSpec((tm, D_MODEL), lambda t, e: (t, 0)),
                  pl.BlockSpec((tm, D_MODEL), lambda t, e: (t, 0)),
                  sc, sc,
                  pl.BlockSpec((PEER_HEADS, tm), lambda t, e: (0, t)),
                  pl.BlockSpec((et, D_MODEL), lambda t, e: (e, 0)),
                  pl.BlockSpec((et // PEER_BLK, D_MODEL, PEER_BLK), lambda t, e: (e, 0, 0)),
                  pl.BlockSpec((1, D_MODEL), lambda t, e: (0, 0))],
        out_specs=pl.BlockSpec((tm, D_MODEL), lambda t, e: (t, 0)),
        out_shape=jax.ShapeDtypeStruct((T, D_MODEL), F32),
        scratch_shapes=[pltpu.VMEM((D_MODEL, tm), F32), pltpu.VMEM((2, PEER_BLK, tm), F32),
                        pltpu.VMEM((2, PEER_BLK, tm), BF16)],
        compiler_params=_cparams(("parallel", "arbitrary")),
        name="peer_main",
    )(x2, hn, st1, s2k, kap, lw["peer_u"], lw["peer_vt"], final_g)


def _rot_half_cols(w):
    d = w.shape[-1]
    return jnp.concatenate([-w[..., d // 2:], w[..., :d // 2]], axis=-1)


def _pad_cols(w, left, total):
    return jnp.pad(w, ((0, 0), (left, total - left - w.shape[-1])))


def _head_groups(w, n_heads, width, left=0):
    return jnp.concatenate(
        [_pad_cols(w[:, h * width:(h + 1) * width], left, LANES) for h in range(n_heads)], axis=-1)


def _layer_weights(p, l):
    w_in = p["w_in"][l]
    widths = (MLA_Q_LORA, MLA_KV_LORA, MLA_ROPE, F_WIDTH, GQA_HEADS * GQA_HEAD_DIM,
              GQA_KV_HEADS * GQA_HEAD_DIM, GQA_KV_HEADS * GQA_HEAD_DIM, 3 * D_MODEL)
    offs = np.cumsum((0,) + widths)
    wcq, wckv, wkr, wxf, wgq, wgk, wgv, wgate = (w_in[:, offs[i]:offs[i + 1]] for i in range(8))

    def rot_heads(w, n_heads, width):
        return jnp.concatenate(
            [_rot_half_cols(w[:, h * width:(h + 1) * width]) for h in range(n_heads)], axis=-1)

    w1 = jnp.concatenate([
        wcq, wckv,
        _pad_cols(wkr, MLA_NOPE, LANES), _pad_cols(_rot_half_cols(wkr), MLA_NOPE, LANES),
        wxf,
        _head_groups(wgq, GQA_HEADS, GQA_HEAD_DIM),
        _head_groups(rot_heads(wgq, GQA_HEADS, GQA_HEAD_DIM), GQA_HEADS, GQA_HEAD_DIM),
        _head_groups(wgk, GQA_KV_HEADS, GQA_HEAD_DIM),
        _head_groups(rot_heads(wgk, GQA_KV_HEADS, GQA_HEAD_DIM), GQA_KV_HEADS, GQA_HEAD_DIM),
        _head_groups(wgv, GQA_KV_HEADS, GQA_HEAD_DIM),
        wgate], axis=-1).astype(BF16)
    assert w1.shape[-1] == _W1_COLS

    w_uq = p["mla_w_uq"][l]
    qd = MLA_NOPE + MLA_ROPE
    uq, uqr = [], []
    for h in range(MLA_HEADS):
        blk = w_uq[:, h * qd:(h + 1) * qd]
        uq.append(_pad_cols(blk, 0, LANES))
        uqr.append(_pad_cols(_rot_half_cols(blk[:, MLA_NOPE:]), MLA_NOPE, LANES))
    w_ukv = p["mla_w_ukv"][l]
    kd = MLA_NOPE + MLA_V
    uk = [_pad_cols(w_ukv[:, h * kd:h * kd + MLA_NOPE], 0, LANES) for h in range(MLA_HEADS)]
    uv = [_pad_cols(w_ukv[:, h * kd + MLA_NOPE:(h + 1) * kd], 0, LANES) for h in range(MLA_HEADS)]

    def pad_rows(w, n_heads, width):
        return jnp.concatenate(
            [jnp.pad(w[h * width:(h + 1) * width], ((0, LANES - width), (0, 0)))
             for h in range(n_heads)], axis=0)

    return {
        "norm_mix_g": p["norm_mix_g"][l][None, :],
        "w1": w1,
        "q_norm_g": p["mla_q_norm_g"][l][None, :],
        "w_uq": jnp.concatenate(uq, axis=-1).astype(BF16),
        "w_uq_rot": jnp.concatenate(uqr, axis=-1).astype(BF16),
        "kv_norm_g": p["mla_kv_norm_g"][l][None, :],
        "w_ukv": jnp.concatenate(uk + uv, axis=-1).astype(BF16),
        "mla_w_o": pad_rows(p["mla_w_o"][l], MLA_HEADS, MLA_V).astype(BF16),
        "fnet_w": p["fnet_w"][l].astype(BF16),
        "fnet_b": p["fnet_b"][l][None, :],
        "gqa_sink": p["gqa_sink"][l],
        "gqa_w_o": pad_rows(p["gqa_w_o"][l], GQA_HEADS, GQA_HEAD_DIM).astype(BF16),
        "w_out": p["w_out"][l].astype(BF16),
        "norm_ffn_g": p["norm_ffn_g"][l][None, :],
        "peer_w_q": p["peer_w_q"][l].astype(BF16),
        "peer_keys": p["peer_keys"][l].reshape(2 * PEER_HEADS, PEER_N_KEYS, PEER_HALF).astype(BF16),
        "peer_u": p["peer_u"][l].astype(BF16),
        "peer_vt": p["peer_v"][l].astype(BF16).reshape(
            PEER_N_EXPERTS // PEER_BLK, PEER_BLK, D_MODEL).transpose(0, 2, 1),
    }


def _rope_tables(S):
    pos = jnp.arange(S, dtype=F32)[:, None]

    def cs(d):
        inv = 1.0 / (ROPE_THETA ** (jnp.arange(0, d, 2, dtype=F32) / d))
        ang = pos * inv[None, :]
        return (jnp.concatenate([jnp.cos(ang)] * 2, axis=-1),
                jnp.concatenate([jnp.sin(ang)] * 2, axis=-1))

    c, s = cs(MLA_ROPE)
    rest = LANES - MLA_NOPE - MLA_ROPE
    cm = jnp.concatenate([jnp.ones((S, MLA_NOPE), F32), c, jnp.zeros((S, rest), F32)], axis=-1)
    sm = jnp.concatenate([jnp.zeros((S, MLA_NOPE), F32), s, jnp.zeros((S, rest), F32)], axis=-1)
    c, s = cs(GQA_HEAD_DIM)
    zero = jnp.zeros((S, LANES - GQA_HEAD_DIM), F32)
    return {"cm": cm, "sm": sm, "cg": jnp.concatenate([c, zero], axis=-1),
            "sg": jnp.concatenate([s, zero], axis=-1)}


def _fourier_tables(S):
    n2 = 128
    n1 = S // n2

    def cos_sin(num, den):
        ang = (2.0 * math.pi / den) * (num % den).astype(F32)
        return jnp.cos(ang), jnp.sin(ang)

    i1 = jnp.arange(n1, dtype=jnp.int32)
    c1, s1 = cos_sin(i1[:, None] * i1[None, :], n1)
    w1 = jnp.concatenate([c1, -s1], axis=0)
    i2 = jnp.arange(n2, dtype=jnp.int32)
    kk = i1[:, None, None] + n1 * i2[None, :, None]
    mr, ms = cos_sin(kk * i2[None, None, :], S)
    m2 = jnp.concatenate([jnp.concatenate([mr, ms], axis=-1),
                          jnp.concatenate([-ms, mr], axis=-1)], axis=1)
    ic = jnp.arange(F_GROUP_DIM, dtype=jnp.int32)
    cc, cs = cos_sin(ic[:, None] * ic[None, :], F_GROUP_DIM)
    eye = jnp.eye(F_GROUPS, dtype=F32)
    return {"n1": n1, "n2": n2, "w1": w1, "m2": m2, "cc": jnp.kron(eye, cc),
            "cs": jnp.kron(eye, cs)}


def _pick(S, pref):
    t = pref
    while S % t:
        t //= 2
    return t


def _trunk(x, layers, final_g, rope_t, four_t):
    B, S, _ = x.shape
    T = B * S
    x2 = x.reshape(T, D_MODEL)
    tm = _pick(S, 256)
    n_layers = len(layers)
    for l, lw in enumerate(layers):
        q, k, v, xf, qg, kg, vg, gate = _in_proj(x2, lw, rope_t, S, tm)
        attn = _mla_attn(q, k, v, B, S, _pick(S, 1024), _pick(S // 2, 512))
        four = _fourier(xf, four_t, B, S)
        og = _gqa(qg, kg, vg, lw["gqa_sink"], B, S)
        x2 = _merge(x2, attn, four, og, gate, lw, tm)
        hn, st1, s2k, kap = _peer_prep(x2, lw, _pick(T, 256))
        x2 = _peer_main(x2, hn, st1, s2k, kap, lw, final_g, _pick(T, 512), 16,
                        final_norm=(l == n_layers - 1))
    return x2.reshape(B, S, D_MODEL)


def kernel(x_prompt, x_sample, norm_mix_g, w_in, mla_q_norm_g, mla_w_uq, mla_kv_norm_g, mla_w_ukv, mla_w_o, fnet_w, fnet_b, gqa_sink, gqa_w_o, w_out, norm_ffn_g, peer_w_q, peer_keys, peer_u, peer_v, final_norm_g):
    p = dict(norm_mix_g=norm_mix_g, w_in=w_in, mla_q_norm_g=mla_q_norm_g, mla_w_uq=mla_w_uq,
             mla_kv_norm_g=mla_kv_norm_g, mla_w_ukv=mla_w_ukv, mla_w_o=mla_w_o, fnet_w=fnet_w,
             fnet_b=fnet_b, gqa_sink=gqa_sink, gqa_w_o=gqa_w_o, w_out=w_out,
             norm_ffn_g=norm_ffn_g, peer_w_q=peer_w_q, peer_keys=peer_keys, peer_u=peer_u,
             peer_v=peer_v)
    layers = [_layer_weights(p, l) for l in range(w_in.shape[0])]
    final_g = final_norm_g[None, :]
    outs = []
    for x in (x_prompt, x_sample):
        S = x.shape[1]
        outs.append(_trunk(x, layers, final_g, _rope_tables(S), _fourier_tables(S)))
    return tuple(outs)
```

```python
import functools
import math

import jax
import jax.numpy as jnp
import numpy as np
from jax import lax
from jax.experimental import pallas as pl
from jax.experimental.pallas import tpu as pltpu

D_MODEL = 1024
EPS = 1e-6
ROPE_THETA = 10000.0
NEG = -1e30
LANES = 128

MLA_HEADS = 8
MLA_NOPE = 64
MLA_ROPE = 32
MLA_V = 64
MLA_Q_LORA = 384
MLA_KV_LORA = 256
F_GROUPS = 4
F_GROUP_DIM = 64
F_WIDTH = F_GROUPS * F_GROUP_DIM
GQA_HEADS = 4
GQA_KV_HEADS = 2
GQA_GROUP = GQA_HEADS // GQA_KV_HEADS
GQA_HEAD_DIM = 64
WINDOW = 128
PEER_HEADS = 8
PEER_N_KEYS = 128
PEER_N_EXPERTS = PEER_N_KEYS * PEER_N_KEYS
PEER_HALF = 128
PEER_TOPK = 16

VMEM_LIMIT = 56 * 1024 * 1024

BF16 = jnp.bfloat16
F32 = jnp.float32
HIGHEST = lax.Precision.HIGHEST


def _cparams(sem):
    return pltpu.CompilerParams(dimension_semantics=sem, vmem_limit_bytes=VMEM_LIMIT)


def _rms(x, g):
    return x * lax.rsqrt(jnp.mean(x * x, axis=-1, keepdims=True) + EPS) * g


def _dot(a, b):
    return jnp.dot(a, b, preferred_element_type=F32)


def _dot_nt(a, b):
    return lax.dot_general(a, b, (((1,), (1,)), ((), ())), preferred_element_type=F32)


_C_CQ = (0, 384)
_C_CKV = (384, 640)
_C_KR = (640, 768)
_C_KRR = (768, 896)
_C_XF = (896, 1152)
_C_GQ = (1152, 1664)
_C_GQR = (1664, 2176)
_C_GK = (2176, 2432)
_C_GKR = (2432, 2688)
_C_GV = (2688, 2944)
_C_GATE = (2944, 6016)
_W1_COLS = 6016


def _in_proj_kernel(x_ref, g_ref, w1_ref, qng_ref, wuq_ref, wuqr_ref, kvng_ref, wukv_ref,
                    cm_ref, sm_ref, cg_ref, sg_ref,
                    q_ref, k_ref, v_ref, xf_ref, qg_ref, kg_ref, vg_ref, gate_ref):
    h = _rms(x_ref[...], g_ref[...]).astype(BF16)

    def proj(c):
        return _dot(h, w1_ref[:, c[0]:c[1]])

    cm, sm = cm_ref[...], sm_ref[...]
    cg, sg = cg_ref[...], sg_ref[...]

    cqn = _rms(proj(_C_CQ), qng_ref[...]).astype(BF16)
    q = _dot(cqn, wuq_ref[...])
    qr = _dot(cqn, wuqr_ref[...])
    mla_scale = (MLA_NOPE + MLA_ROPE) ** -0.5 * math.log2(math.e)
    for hd in range(MLA_HEADS):
        sl = slice(hd * LANES, (hd + 1) * LANES)
        q_ref[:, sl] = ((q[:, sl] * cm + qr[:, sl] * sm) * mla_scale).astype(BF16)

    ckvn = _rms(proj(_C_CKV), kvng_ref[...]).astype(BF16)
    kv = _dot(ckvn, wukv_ref[...])
    k_rope = proj(_C_KR) * cm + proj(_C_KRR) * sm
    ones_col = (lax.broadcasted_iota(jnp.int32, (1, LANES), 1) == MLA_V).astype(F32)
    for hd in range(MLA_HEADS):
        sl = slice(hd * LANES, (hd + 1) * LANES)
        k_ref[:, sl] = (kv[:, sl] + k_rope).astype(BF16)
        vs = slice(MLA_HEADS * LANES + hd * LANES, MLA_HEADS * LANES + (hd + 1) * LANES)
        v_ref[:, sl] = (kv[:, vs] + ones_col).astype(BF16)

    xf_ref[...] = proj(_C_XF)

    gq, gqr = proj(_C_GQ), proj(_C_GQR)
    gqa_scale = GQA_HEAD_DIM ** -0.5
    for hd in range(GQA_HEADS):
        sl = slice(hd * LANES, (hd + 1) * LANES)
        qg_ref[:, sl] = ((gq[:, sl] * cg + gqr[:, sl] * sg) * gqa_scale).astype(BF16)
    gk, gkr = proj(_C_GK), proj(_C_GKR)
    for hd in range(GQA_KV_HEADS):
        sl = slice(hd * LANES, (hd + 1) * LANES)
        kg_ref[:, sl] = (gk[:, sl] * cg + gkr[:, sl] * sg).astype(BF16)
    vg_ref[...] = proj(_C_GV).astype(BF16)

    gate = proj(_C_GATE)
    gate_ref[...] = 1.0 / (1.0 + jnp.exp(-gate))


def _in_proj(x2, lw, tabs, S, tm):
    T = x2.shape[0]
    nt = T // tm
    ns = S // tm
    full = lambda shape: pl.BlockSpec(shape, lambda i: (0,) * len(shape))
    row = lambda w: pl.BlockSpec((tm, w), lambda i: (i, 0))
    tab = pl.BlockSpec((tm, LANES), lambda i: (i % ns, 0))
    outs = [(1024, BF16), (1024, BF16), (1024, BF16), (F_WIDTH, F32), (512, BF16), (256, BF16),
            (256, BF16), (3 * D_MODEL, F32)]
    return pl.pallas_call(
        _in_proj_kernel,
        grid=(nt,),
        in_specs=[row(D_MODEL), full((1, D_MODEL)), full((D_MODEL, _W1_COLS)),
                  full((1, MLA_Q_LORA)), full((MLA_Q_LORA, 1024)), full((MLA_Q_LORA, 1024)),
                  full((1, MLA_KV_LORA)), full((MLA_KV_LORA, 2048)),
                  tab, tab, tab, tab],
        out_specs=[row(w) for w, _ in outs],
        out_shape=[jax.ShapeDtypeStruct((T, w), dt) for w, dt in outs],
        compiler_params=_cparams(("parallel",)),
        name="in_proj",
    )(x2, lw["norm_mix_g"], lw["w1"], lw["q_norm_g"], lw["w_uq"], lw["w_uq_rot"],
      lw["kv_norm_g"], lw["w_ukv"], tabs["cm"], tabs["sm"], tabs["cg"], tabs["sg"])


def _mla_attn_kernel(q_ref, k_ref, v_ref, o_ref, s_ref, p_ref, *, tk):
    tq = q_ref.shape[0]
    nk = k_ref.shape[0] // tk
    q = q_ref[...]

    def chunk(ref, c):
        return ref[pl.ds(pl.multiple_of(c * tk, tk), tk), :]

    def step(c, slot, carry):
        m, alpha, acc = carry
        pv = _dot(p_ref[1 - slot], chunk(v_ref, jnp.maximum(c - 1, 0)))
        s_ref[1 - slot] = _dot_nt(q, chunk(k_ref, jnp.minimum(c + 1, nk - 1)))
        s = s_ref[slot]
        m_new = jnp.maximum(m, jnp.max(s, axis=-1, keepdims=True))
        p_ref[slot] = jnp.exp2(s - m_new).astype(BF16)
        return m_new, jnp.exp2(m - m_new), acc * alpha + pv

    def body(c2, carry):
        carry = step(2 * c2, 0, carry)
        return step(2 * c2 + 1, 1, carry)

    s_ref[0] = _dot_nt(q, chunk(k_ref, 0))
    p_ref[1] = jnp.zeros(p_ref.shape[1:], BF16)
    init = (jnp.full((tq, 1), NEG, F32), jnp.ones((tq, 1), F32), jnp.zeros((tq, LANES), F32))
    _, alpha, acc = lax.fori_loop(0, nk // 2, body, init)
    acc = acc * alpha + _dot(p_ref[1], chunk(v_ref, nk - 1))
    o_ref[...] = (acc / acc[:, MLA_V:MLA_V + 1]).astype(o_ref.dtype)


def _mla_attn(q, k, v, B, S, tq, tk):
    assert (S // tk) % 2 == 0
    q3, k3, v3 = (a.reshape(B, S, MLA_HEADS * LANES) for a in (q, k, v))
    out = pl.pallas_call(
        functools.partial(_mla_attn_kernel, tk=tk),
        scratch_shapes=[pltpu.VMEM((2, tq, tk), F32), pltpu.VMEM((2, tq, tk), BF16)],
        grid=(B, MLA_HEADS, S // tq),
        in_specs=[pl.BlockSpec((None, tq, LANES), lambda b, h, i: (b, i, h)),
                  pl.BlockSpec((None, S, LANES), lambda b, h, i: (b, 0, h)),
                  pl.BlockSpec((None, S, LANES), lambda b, h, i: (b, 0, h))],
        out_specs=pl.BlockSpec((None, tq, LANES), lambda b, h, i: (b, i, h)),
        out_shape=jax.ShapeDtypeStruct((B, S, MLA_HEADS * LANES), BF16),
        compiler_params=_cparams(("parallel", "parallel", "parallel")),
        name="mla_attn",
    )(q3, k3, v3)
    return out.reshape(B * S, MLA_HEADS * LANES)


def _fft1_kernel(w_ref, x_ref, o_ref):
    o_ref[...] = jnp.dot(w_ref[...], x_ref[...], preferred_element_type=F32, precision=HIGHEST)


def _fft2_kernel(m_ref, a_ref, cc_ref, cs_ref, o_ref, *, scale):
    aa = jnp.concatenate([a_ref[0], a_ref[1]], axis=0)
    y = jnp.dot(m_ref[...], aa, preferred_element_type=F32, precision=HIGHEST)
    n2 = y.shape[0] // 2
    out = (jnp.dot(y[:n2], cc_ref[...], preferred_element_type=F32, precision=HIGHEST)
           + jnp.dot(y[n2:], cs_ref[...], preferred_element_type=F32, precision=HIGHEST))
    o_ref[...] = out * scale


def _fourier(xf, ft, B, S):
    n1, n2 = ft["n1"], ft["n2"]
    cols = n2 * F_WIDTH
    tc = min(cols, 4096)
    x3 = xf.reshape(B, n1, cols)
    a = pl.pallas_call(
        _fft1_kernel,
        grid=(B, cols // tc),
        in_specs=[pl.BlockSpec((2 * n1, n1), lambda b, j: (0, 0)),
                  pl.BlockSpec((None, n1, tc), lambda b, j: (b, 0, j))],
        out_specs=pl.BlockSpec((None, 2 * n1, tc), lambda b, j: (b, 0, j)),
        out_shape=jax.ShapeDtypeStruct((B, 2 * n1, cols), F32),
        compiler_params=_cparams(("parallel", "parallel")),
        name="fourier_stage1",
    )(ft["w1"], x3)
    a5 = a.reshape(B, 2, n1, n2, F_WIDTH)
    y = pl.pallas_call(
        functools.partial(_fft2_kernel, scale=1.0 / math.sqrt(S * F_GROUP_DIM)),
        grid=(B, n1),
        in_specs=[pl.BlockSpec((None, 2 * n2, 2 * n2), lambda b, k: (k, 0, 0)),
                  pl.BlockSpec((None, 2, None, n2, F_WIDTH), lambda b, k: (b, 0, k, 0, 0)),
                  pl.BlockSpec((F_WIDTH, F_WIDTH), lambda b, k: (0, 0)),
                  pl.BlockSpec((F_WIDTH, F_WIDTH), lambda b, k: (0, 0))],
        out_specs=pl.BlockSpec((None, None, n2, F_WIDTH), lambda b, k: (b, k, 0, 0)),
        out_shape=jax.ShapeDtypeStruct((B, n1, n2, F_WIDTH), F32),
        compiler_params=_cparams(("parallel", "parallel")),
        name="fourier_stage2",
    )(ft["m2"], a5, ft["cc"], ft["cs"])
    return jnp.transpose(y, (0, 2, 1, 3)).reshape(B * S, F_WIDTH)


def _gqa_kernel(sink_ref, q_ref, kp_ref, kc_ref, kn_ref, vp_ref, vc_ref, vn_ref, o_ref, *, S):
    n = pl.program_id(1)
    tq, halo = q_ref.shape[0], kp_ref.shape[0]
    width = tq + 2 * halo
    qpos = n * tq + lax.broadcasted_iota(jnp.int32, (tq, width), 0)
    kpos = n * tq - halo + lax.broadcasted_iota(jnp.int32, (tq, width), 1)
    mask = (jnp.abs(qpos - kpos) <= WINDOW) & (kpos >= 0) & (kpos < S)
    for hd in range(GQA_HEADS):
        kh = hd // GQA_GROUP
        qs = slice(hd * LANES, (hd + 1) * LANES)
        ks = slice(kh * LANES, (kh + 1) * LANES)
        q = q_ref[:, qs]
        s = jnp.concatenate([_dot_nt(q, kp_ref[:, ks]), _dot_nt(q, kc_ref[:, ks]),
                             _dot_nt(q, kn_ref[:, ks])], axis=-1)
        s = jnp.where(mask, s, NEG)
        sk = sink_ref[hd]
        m = jnp.maximum(jnp.max(s, axis=-1, keepdims=True), sk)
        p = jnp.exp(s - m)
        den = jnp.sum(p, axis=-1, keepdims=True) + jnp.exp(sk - m)
        pb = (p / den).astype(BF16)
        o = (_dot(pb[:, :halo], vp_ref[:, ks]) + _dot(pb[:, halo:halo + tq], vc_ref[:, ks])
             + _dot(pb[:, halo + tq:], vn_ref[:, ks]))
        o_ref[:, qs] = o.astype(o_ref.dtype)


def _gqa(qg, kg, vg, sink, B, S, tq):
    halo = WINDOW
    per = tq // halo
    nb = S // halo
    q3 = qg.reshape(B, S, GQA_HEADS * LANES)
    k3 = kg.reshape(B, S, GQA_KV_HEADS * LANES)
    v3 = vg.reshape(B, S, GQA_KV_HEADS * LANES)
    kvw = GQA_KV_HEADS * LANES
    prev = pl.BlockSpec((None, halo, kvw), lambda b, n: (b, jnp.maximum(n * per - 1, 0), 0))
    cur = pl.BlockSpec((None, tq, kvw), lambda b, n: (b, n, 0))
    nxt = pl.BlockSpec((None, halo, kvw), lambda b, n: (b, jnp.minimum((n + 1) * per, nb - 1), 0))
    out = pl.pallas_call(
        functools.partial(_gqa_kernel, S=S),
        grid=(B, S // tq),
        in_specs=[pl.BlockSpec(memory_space=pltpu.SMEM),
                  pl.BlockSpec((None, tq, GQA_HEADS * LANES), lambda b, n: (b, n, 0)),
                  prev, cur, nxt, prev, cur, nxt],
        out_specs=pl.BlockSpec((None, tq, GQA_HEADS * LANES), lambda b, n: (b, n, 0)),
        out_shape=jax.ShapeDtypeStruct((B, S, GQA_HEADS * LANES), BF16),
        compiler_params=_cparams(("parallel", "parallel")),
        name="gqa_window",
    )(sink, q3, k3, k3, k3, v3, v3, v3)
    return out.reshape(B * S, GQA_HEADS * LANES)


def _merge_kernel(x_ref, attn_ref, four_ref, og_ref, gate_ref, wo_ref, fw_ref, fb_ref, gwo_ref,
                  wout_ref, o_ref):
    mla_o = _dot(attn_ref[...], wo_ref[...])
    fnet_o = _dot(four_ref[...].astype(BF16), fw_ref[...]) + fb_ref[...]
    gqa_o = _dot(og_ref[...], gwo_ref[...])
    d = D_MODEL
    merged = (gate_ref[:, 0:d] * mla_o + gate_ref[:, d:2 * d] * fnet_o
              + gate_ref[:, 2 * d:3 * d] * gqa_o)
    o_ref[...] = x_ref[...] + _dot(merged.astype(BF16), wout_ref[...])


def _merge(x2, attn, four, og, gate, lw, tm):
    T = x2.shape[0]
    full = lambda shape: pl.BlockSpec(shape, lambda i: (0,) * len(shape))
    row = lambda w: pl.BlockSpec((tm, w), lambda i: (i, 0))
    return pl.pallas_call(
        _merge_kernel,
        grid=(T // tm,),
        in_specs=[row(D_MODEL), row(1024), row(F_WIDTH), row(512), row(3 * D_MODEL),
                  full((1024, D_MODEL)), full((F_WIDTH, D_MODEL)), full((1, D_MODEL)),
                  full((512, D_MODEL)), full((D_MODEL, D_MODEL))],
        out_specs=row(D_MODEL),
        out_shape=jax.ShapeDtypeStruct((T, D_MODEL), F32),
        compiler_params=_cparams(("parallel",)),
        name="merge",
    )(x2, attn, four, og, gate, lw["mla_w_o"], lw["fnet_w"], lw["fnet_b"], lw["gqa_w_o"],
      lw["w_out"])


LOG2E = math.log2(math.e)
PEER_BLK = 256
PEER_RANKS = PEER_TOPK + 1


def _sort_network(n):
    pairs = []

    def merge(lo, m, r):
        step = 2 * r
        if step < m:
            merge(lo, m, step)
            merge(lo + r, m, step)
            pairs.extend((i, i + r) for i in range(lo + r, lo + m - r, step))
        else:
            pairs.append((lo, lo + r))

    def sort(lo, m):
        if m > 1:
            sort(lo, m // 2)
            sort(lo + m // 2, m // 2)
            merge(lo, m, 1)

    sort(0, n)
    return pairs


def _top_of_sorted_lists(lists, k, extra=None):
    lists = list(lists)
    neg = jnp.full(lists[0].shape, -jnp.inf, F32)
    vals = []
    for r in range(k):
        mx = jnp.max(lists[0], axis=0, keepdims=True)
        if extra is not None:
            mx = jnp.maximum(mx, jnp.max(extra, axis=0, keepdims=True))
        vals.append(mx)
        if r == k - 1:
            break
        win = lists[0] == mx
        for v in range(min(len(lists), k - 1 - r)):
            lists[v] = jnp.where(win, lists[v + 1] if v + 1 < len(lists) else neg, lists[v])
        if extra is not None:
            extra = jnp.where(extra == mx, -jnp.inf, extra)
    return vals


def _top_sorted(s, k):
    lists = [s[8 * v:8 * v + 8] for v in range(s.shape[0] // 8)]
    for i, j in _sort_network(len(lists)):
        lists[i], lists[j] = jnp.maximum(lists[i], lists[j]), jnp.minimum(lists[i], lists[j])
    return _top_of_sorted_lists(lists, k)


def _peer_prep_kernel(x_ref, g_ref, wq_ref, keys_ref, hn_ref, st1_ref, s2k_ref, kap_ref):
    hn = _rms(x_ref[...], g_ref[...]).astype(BF16)
    hn_ref[...] = hn
    q = _dot(hn, wq_ref[...]).astype(BF16)
    tm = q.shape[0]
    kaps = []
    for hd in range(PEER_HEADS):
        sts, tops = [], []
        for p in range(2):
            hp = 2 * hd + p
            st = _dot_nt(keys_ref[hp], q[:, hp * PEER_HALF:(hp + 1) * PEER_HALF])
            sts.append(st)
            tops.append(_top_sorted(st, PEER_RANKS))
        a, b = tops
        a_lo = jnp.concatenate(a[:8], axis=0)
        a_hi = jnp.concatenate(a[8:] + [jnp.full((7, tm), -jnp.inf, F32)], axis=0)
        top = _top_of_sorted_lists([a_lo + b[r] for r in range(PEER_RANKS)], PEER_RANKS,
                                   extra=a_hi + b[0])
        m = top[0]
        z = sum(jnp.exp(t - m) for t in top[:PEER_TOPK])
        mp = m + jnp.log(z)
        thr = 0.5 * (top[PEER_TOPK - 1] + top[PEER_TOPK])
        st1_ref[hd] = sts[0] * LOG2E
        s2k_ref[hd] = (sts[1] - mp) * LOG2E
        kaps.append((thr - mp) * LOG2E)
    kap_ref[...] = jnp.concatenate(kaps, axis=0)


def _peer_prep(x2, lw, tm):
    T = x2.shape[0]
    sc = pl.BlockSpec((PEER_HEADS, PEER_N_KEYS, tm), lambda i: (0, 0, i))
    return pl.pallas_call(
        _peer_prep_kernel,
        grid=(T // tm,),
        in_specs=[pl.BlockSpec((tm, D_MODEL), lambda i: (i, 0)),
                  pl.BlockSpec((1, D_MODEL), lambda i: (0, 0)),
                  pl.BlockSpec((D_MODEL, 2 * PEER_HEADS * PEER_HALF), lambda i: (0, 0)),
                  pl.BlockSpec((2 * PEER_HEADS, PEER_N_KEYS, PEER_HALF), lambda i: (0, 0, 0))],
        out_specs=[pl.BlockSpec((tm, D_MODEL), lambda i: (i, 0)), sc, sc,
                   pl.BlockSpec((PEER_HEADS, tm), lambda i: (0, i))],
        out_shape=[jax.ShapeDtypeStruct((T, D_MODEL), BF16),
                   jax.ShapeDtypeStruct((PEER_HEADS, PEER_N_KEYS, T), F32),
                   jax.ShapeDtypeStruct((PEER_HEADS, PEER_N_KEYS, T), F32),
                   jax.ShapeDtypeStruct((PEER_HEADS, T), F32)],
        compiler_params=_cparams(("parallel",)),
        name="peer_prep",
    )(x2, lw["norm_ffn_g"], lw["peer_w_q"], lw["peer_keys"])


def _peer_main_kernel(x_ref, hn_ref, st1_ref, s2k_ref, kap_ref, u_ref, vt_ref, fg_ref, o_ref,
                      acc_ref, act_ref, w_ref, *, n_i, final_norm):
    e = pl.program_id(1)
    tm = hn_ref.shape[0]

    @pl.when(e == 0)
    def _():
        acc_ref[...] = jnp.zeros_like(acc_ref)

    n_blk = n_i * PEER_N_KEYS // PEER_BLK
    def activations(p):
        rows = pl.ds(pl.multiple_of(p * PEER_BLK, PEER_BLK), PEER_BLK)
        a = _dot_nt(u_ref[rows, :], hn_ref[...])
        return 0.5 * a * (1.0 + lax.erf(a * (1.0 / math.sqrt(2.0))))

    def step(p, slot):
        act_ref[1 - slot] = activations(jnp.minimum(p + 1, n_blk - 1))
        acc_ref[...] += _dot(vt_ref[jnp.maximum(p - 1, 0)], w_ref[1 - slot])
        for ii in range(PEER_BLK // PEER_N_KEYS):
            i = e * n_i + p * (PEER_BLK // PEER_N_KEYS) + ii
            rows = slice(ii * PEER_N_KEYS, (ii + 1) * PEER_N_KEYS)
            s1 = [st1_ref[hd, pl.ds(i, 1), :] for hd in range(PEER_HEADS)]
            for c in range(tm // LANES):
                ls = slice(c * LANES, (c + 1) * LANES)
                g = None
                for hd in range(PEER_HEADS):
                    z = s2k_ref[hd, :, ls] + s1[hd][:, ls]
                    t = jnp.where(z > kap_ref[hd:hd + 1, ls], jnp.exp2(z), 0.0)
                    g = t if g is None else g + t
                w_ref[slot, rows, ls] = (g * act_ref[slot, rows, ls]).astype(BF16)

    def body(p2, carry):
        step(2 * p2, 0)
        step(2 * p2 + 1, 1)
        return carry

    act_ref[0] = activations(0)
    w_ref[1] = jnp.zeros(w_ref.shape[1:], BF16)
    lax.fori_loop(0, n_blk // 2, body, 0)
    acc_ref[...] += _dot(vt_ref[n_blk - 1], w_ref[1])

    @pl.when(e == pl.num_programs(1) - 1)
    def _():
        y = x_ref[...] + acc_ref[...].T
        if final_norm:
            y = _rms(y, fg_ref[...])
        o_ref[...] = y


def _peer_main(x2, hn, st1, s2k, kap, lw, final_g, tm, n_i, final_norm):
    T = x2.shape[0]
    et = n_i * PEER_N_KEYS
    sc = pl.BlockSpec((PEER_HEADS, PEER_N_KEYS, tm), lambda t, e: (0, 0, t))
    return pl.pallas_call(
        functools.partial(_peer_main_kernel, n_i=n_i, final_norm=final_norm),
        grid=(T // tm, PEER_N_EXPERTS // et),
        in_specs=[pl.BlockSpec((tm, D_MODEL), lambda t, e: (t, 0)),
                  pl.BlockSpec((tm, D_MODEL), lambda t, e: (t, 0)),
                  sc, sc,
                  pl.BlockSpec((PEER_HEADS, tm), lambda t, e: (0, t)),
                  pl.BlockSpec((et, D_MODEL), lambda t, e: (e, 0)),
                  pl.BlockSpec((et // PEER_BLK, D_MODEL, PEER_BLK), lambda t, e: (e, 0, 0)),
                  pl.BlockSpec((1, D_MODEL), lambda t, e: (0, 0))],
        out_specs=pl.BlockSpec((tm, D_MODEL), lambda t, e: (t, 0)),
        out_shape=jax.ShapeDtypeStruct((T, D_MODEL), F32),
        scratch_shapes=[pltpu.VMEM((D_MODEL, tm), F32), pltpu.VMEM((2, PEER_BLK, tm), F32),
                        pltpu.VMEM((2, PEER_BLK, tm), BF16)],
        compiler_params=_cparams(("parallel", "arbitrary")),
        name="peer_main",
    )(x2, hn, st1, s2k, kap, lw["peer_u"], lw["peer_vt"], final_g)


def _rot_half_cols(w):
    d = w.shape[-1]
    return jnp.concatenate([-w[..., d // 2:], w[..., :d // 2]], axis=-1)


def _pad_cols(w, left, total):
    return jnp.pad(w, ((0, 0), (left, total - left - w.shape[-1])))


def _head_groups(w, n_heads, width, left=0):
    return jnp.concatenate(
        [_pad_cols(w[:, h * width:(h + 1) * width], left, LANES) for h in range(n_heads)], axis=-1)


def _layer_weights(p, l):
    w_in = p["w_in"][l]
    widths = (MLA_Q_LORA, MLA_KV_LORA, MLA_ROPE, F_WIDTH, GQA_HEADS * GQA_HEAD_DIM,
              GQA_KV_HEADS * GQA_HEAD_DIM, GQA_KV_HEADS * GQA_HEAD_DIM, 3 * D_MODEL)
    offs = np.cumsum((0,) + widths)
    wcq, wckv, wkr, wxf, wgq, wgk, wgv, wgate = (w_in[:, offs[i]:offs[i + 1]] for i in range(8))

    def rot_heads(w, n_heads, width):
        return jnp.concatenate(
            [_rot_half_cols(w[:, h * width:(h + 1) * width]) for h in range(n_heads)], axis=-1)

    w1 = jnp.concatenate([
        wcq, wckv,
        _pad_cols(wkr, MLA_NOPE, LANES), _pad_cols(_rot_half_cols(wkr), MLA_NOPE, LANES),
        wxf,
        _head_groups(wgq, GQA_HEADS, GQA_HEAD_DIM),
        _head_groups(rot_heads(wgq, GQA_HEADS, GQA_HEAD_DIM), GQA_HEADS, GQA_HEAD_DIM),
        _head_groups(wgk, GQA_KV_HEADS, GQA_HEAD_DIM),
        _head_groups(rot_heads(wgk, GQA_KV_HEADS, GQA_HEAD_DIM), GQA_KV_HEADS, GQA_HEAD_DIM),
        _head_groups(wgv, GQA_KV_HEADS, GQA_HEAD_DIM),
        wgate], axis=-1).astype(BF16)
    assert w1.shape[-1] == _W1_COLS

    w_uq = p["mla_w_uq"][l]
    qd = MLA_NOPE + MLA_ROPE
    uq, uqr = [], []
    for h in range(MLA_HEADS):
        blk = w_uq[:, h * qd:(h + 1) * qd]
        uq.append(_pad_cols(blk, 0, LANES))
        uqr.append(_pad_cols(_rot_half_cols(blk[:, MLA_NOPE:]), MLA_NOPE, LANES))
    w_ukv = p["mla_w_ukv"][l]
    kd = MLA_NOPE + MLA_V
    uk = [_pad_cols(w_ukv[:, h * kd:h * kd + MLA_NOPE], 0, LANES) for h in range(MLA_HEADS)]
    uv = [_pad_cols(w_ukv[:, h * kd + MLA_NOPE:(h + 1) * kd], 0, LANES) for h in range(MLA_HEADS)]

    def pad_rows(w, n_heads, width):
        return jnp.concatenate(
            [jnp.pad(w[h * width:(h + 1) * width], ((0, LANES - width), (0, 0)))
             for h in range(n_heads)], axis=0)

    return {
        "norm_mix_g": p["norm_mix_g"][l][None, :],
        "w1": w1,
        "q_norm_g": p["mla_q_norm_g"][l][None, :],
        "w_uq": jnp.concatenate(uq, axis=-1).astype(BF16),
        "w_uq_rot": jnp.concatenate(uqr, axis=-1).astype(BF16),
        "kv_norm_g": p["mla_kv_norm_g"][l][None, :],
        "w_ukv": jnp.concatenate(uk + uv, axis=-1).astype(BF16),
        "mla_w_o": pad_rows(p["mla_w_o"][l], MLA_HEADS, MLA_V).astype(BF16),
        "fnet_w": p["fnet_w"][l].astype(BF16),
        "fnet_b": p["fnet_b"][l][None, :],
        "gqa_sink": p["gqa_sink"][l],
        "gqa_w_o": pad_rows(p["gqa_w_o"][l], GQA_HEADS, GQA_HEAD_DIM).astype(BF16),
        "w_out": p["w_out"][l].astype(BF16),
        "norm_ffn_g": p["norm_ffn_g"][l][None, :],
        "peer_w_q": p["peer_w_q"][l].astype(BF16),
        "peer_keys": p["peer_keys"][l].reshape(2 * PEER_HEADS, PEER_N_KEYS, PEER_HALF).astype(BF16),
        "peer_u": p["peer_u"][l].astype(BF16),
        "peer_vt": p["peer_v"][l].astype(BF16).reshape(
            PEER_N_EXPERTS // PEER_BLK, PEER_BLK, D_MODEL).transpose(0, 2, 1),
    }


def _rope_tables(S):
    pos = jnp.arange(S, dtype=F32)[:, None]

    def cs(d):
        inv = 1.0 / (ROPE_THETA ** (jnp.arange(0, d, 2, dtype=F32) / d))
        ang = pos * inv[None, :]
        return (jnp.concatenate([jnp.cos(ang)] * 2, axis=-1),
                jnp.concatenate([jnp.sin(ang)] * 2, axis=-1))

    c, s = cs(MLA_ROPE)
    rest = LANES - MLA_NOPE - MLA_ROPE
    cm = jnp.concatenate([jnp.ones((S, MLA_NOPE), F32), c, jnp.zeros((S, rest), F32)], axis=-1)
    sm = jnp.concatenate([jnp.zeros((S, MLA_NOPE), F32), s, jnp.zeros((S, rest), F32)], axis=-1)
    c, s = cs(GQA_HEAD_DIM)
    zero = jnp.zeros((S, LANES - GQA_HEAD_DIM), F32)
    return {"cm": cm, "sm": sm, "cg": jnp.concatenate([c, zero], axis=-1),
            "sg": jnp.concatenate([s, zero], axis=-1)}


def _fourier_tables(S):
    n2 = 128
    n1 = S // n2

    def cos_sin(num, den):
        ang = (2.0 * math.pi / den) * (num % den).astype(F32)
        return jnp.cos(ang), jnp.sin(ang)

    i1 = jnp.arange(n1, dtype=jnp.int32)
    c1, s1 = cos_sin(i1[:, None] * i1[None, :], n1)
    w1 = jnp.concatenate([c1, -s1], axis=0)
    i2 = jnp.arange(n2, dtype=jnp.int32)
    kk = i1[:, None, None] + n1 * i2[None, :, None]
    mr, ms = cos_sin(kk * i2[None, None, :], S)
    m2 = jnp.concatenate([jnp.concatenate([mr, ms], axis=-1),
                          jnp.concatenate([-ms, mr], axis=-1)], axis=1)
    ic = jnp.arange(F_GROUP_DIM, dtype=jnp.int32)
    cc, cs = cos_sin(ic[:, None] * ic[None, :], F_GROUP_DIM)
    eye = jnp.eye(F_GROUPS, dtype=F32)
    return {"n1": n1, "n2": n2, "w1": w1, "m2": m2, "cc": jnp.kron(eye, cc),
            "cs": jnp.kron(eye, cs)}


def _pick(S, pref):
    t = pref
    while S % t:
        t //= 2
    return t


def _trunk(x, layers, final_g, rope_t, four_t):
    B, S, _ = x.shape
    T = B * S
    x2 = x.reshape(T, D_MODEL)
    tm = _pick(S, 256)
    n_layers = len(layers)
    for l, lw in enumerate(layers):
        q, k, v, xf, qg, kg, vg, gate = _in_proj(x2, lw, rope_t, S, tm)
        attn = _mla_attn(q, k, v, B, S, _pick(S, 1024), _pick(S // 2, 512))
        four = _fourier(xf, four_t, B, S)
        og = _gqa(qg, kg, vg, lw["gqa_sink"], B, S, _pick(S, 512))
        x2 = _merge(x2, attn, four, og, gate, lw, tm)
        hn, st1, s2k, kap = _peer_prep(x2, lw, _pick(T, 256))
        x2 = _peer_main(x2, hn, st1, s2k, kap, lw, final_g, _pick(T, 512), 16,
                        final_norm=(l == n_layers - 1))
    return x2.reshape(B, S, D_MODEL)


def kernel(x_prompt, x_sample, norm_mix_g, w_in, mla_q_norm_g, mla_w_uq, mla_kv_norm_g, mla_w_ukv, mla_w_o, fnet_w, fnet_b, gqa_sink, gqa_w_o, w_out, norm_ffn_g, peer_w_q, peer_keys, peer_u, peer_v, final_norm_g):
    p = dict(norm_mix_g=norm_mix_g, w_in=w_in, mla_q_norm_g=mla_q_norm_g, mla_w_uq=mla_w_uq,
             mla_kv_norm_g=mla_kv_norm_g, mla_w_ukv=mla_w_ukv, mla_w_o=mla_w_o, fnet_w=fnet_w,
             fnet_b=fnet_b, gqa_sink=gqa_sink, gqa_w_o=gqa_w_o, w_out=w_out,
             norm_ffn_g=norm_ffn_g, peer_w_q=peer_w_q, peer_keys=peer_keys, peer_u=peer_u,
             peer_v=peer_v)
    layers = [_layer_weights(p, l) for l in range(w_in.shape[0])]
    final_g = final_norm_g[None, :]
    outs = []
    for x in (x_prompt, x_sample):
        S = x.shape[1]
        outs.append(_trunk(x, layers, final_g, _rope_tables(S), _fourier_tables(S)))
    return tuple(outs)
```

```python
import functools
import math

import jax
import jax.numpy as jnp
import numpy as np
from jax import lax
from jax.experimental import pallas as pl
from jax.experimental.pallas import tpu as pltpu

D_MODEL = 1024
EPS = 1e-6
ROPE_THETA = 10000.0
NEG = -1e30
LANES = 128

MLA_HEADS = 8
MLA_NOPE = 64
MLA_ROPE = 32
MLA_V = 64
MLA_Q_LORA = 384
MLA_KV_LORA = 256
F_GROUPS = 4
F_GROUP_DIM = 64
F_WIDTH = F_GROUPS * F_GROUP_DIM
GQA_HEADS = 4
GQA_KV_HEADS = 2
GQA_GROUP = GQA_HEADS // GQA_KV_HEADS
GQA_HEAD_DIM = 64
WINDOW = 128
PEER_HEADS = 8
PEER_N_KEYS = 128
PEER_N_EXPERTS = PEER_N_KEYS * PEER_N_KEYS
PEER_HALF = 128
PEER_TOPK = 16

VMEM_LIMIT = 56 * 1024 * 1024

BF16 = jnp.bfloat16
F32 = jnp.float32
HIGHEST = lax.Precision.HIGHEST


def _cparams(sem):
    return pltpu.CompilerParams(dimension_semantics=sem, vmem_limit_bytes=VMEM_LIMIT)


def _rms(x, g):
    return x * lax.rsqrt(jnp.mean(x * x, axis=-1, keepdims=True) + EPS) * g


def _dot(a, b):
    return jnp.dot(a, b, preferred_element_type=F32)


def _dot_nt(a, b):
    return lax.dot_general(a, b, (((1,), (1,)), ((), ())), preferred_element_type=F32)


_C_CQ = (0, 384)
_C_CKV = (384, 640)
_C_KR = (640, 768)
_C_KRR = (768, 896)
_C_XF = (896, 1152)
_C_GQ = (1152, 1664)
_C_GQR = (1664, 2176)
_C_GK = (2176, 2432)
_C_GKR = (2432, 2688)
_C_GV = (2688, 2944)
_C_GATE = (2944, 6016)
_W1_COLS = 6016


def _in_proj_kernel(x_ref, g_ref, w1_ref, qng_ref, wuq_ref, wuqr_ref, kvng_ref, wukv_ref,
                    cm_ref, sm_ref, cg_ref, sg_ref,
                    q_ref, k_ref, v_ref, xf_ref, qg_ref, kg_ref, vg_ref, gate_ref):
    h = _rms(x_ref[...], g_ref[...]).astype(BF16)

    def proj(c):
        return _dot(h, w1_ref[:, c[0]:c[1]])

    cm, sm = cm_ref[...], sm_ref[...]
    cg, sg = cg_ref[...], sg_ref[...]

    cqn = _rms(proj(_C_CQ), qng_ref[...]).astype(BF16)
    q = _dot(cqn, wuq_ref[...])
    qr = _dot(cqn, wuqr_ref[...])
    mla_scale = (MLA_NOPE + MLA_ROPE) ** -0.5 * math.log2(math.e)
    for hd in range(MLA_HEADS):
        sl = slice(hd * LANES, (hd + 1) * LANES)
        q_ref[:, sl] = ((q[:, sl] * cm + qr[:, sl] * sm) * mla_scale).astype(BF16)

    ckvn = _rms(proj(_C_CKV), kvng_ref[...]).astype(BF16)
    kv = _dot(ckvn, wukv_ref[...])
    k_rope = proj(_C_KR) * cm + proj(_C_KRR) * sm
    ones_col = (lax.broadcasted_iota(jnp.int32, (1, LANES), 1) == MLA_V).astype(F32)
    for hd in range(MLA_HEADS):
        sl = slice(hd * LANES, (hd + 1) * LANES)
        k_ref[:, sl] = (kv[:, sl] + k_rope).astype(BF16)
        vs = slice(MLA_HEADS * LANES + hd * LANES, MLA_HEADS * LANES + (hd + 1) * LANES)
        v_ref[:, sl] = (kv[:, vs] + ones_col).astype(BF16)

    xf_ref[...] = proj(_C_XF)

    gq, gqr = proj(_C_GQ), proj(_C_GQR)
    gqa_scale = GQA_HEAD_DIM ** -0.5
    for hd in range(GQA_HEADS):
        sl = slice(hd * LANES, (hd + 1) * LANES)
        qg_ref[:, sl] = ((gq[:, sl] * cg + gqr[:, sl] * sg) * gqa_scale).astype(BF16)
    gk, gkr = proj(_C_GK), proj(_C_GKR)
    for hd in range(GQA_KV_HEADS):
        sl = slice(hd * LANES, (hd + 1) * LANES)
        kg_ref[:, sl] = (gk[:, sl] * cg + gkr[:, sl] * sg).astype(BF16)
    vg_ref[...] = proj(_C_GV).astype(BF16)

    gate = proj(_C_GATE)
    gate_ref[...] = 1.0 / (1.0 + jnp.exp(-gate))


def _in_proj(x2, lw, tabs, S, tm):
    T = x2.shape[0]
    nt = T // tm
    ns = S // tm
    full = lambda shape: pl.BlockSpec(shape, lambda i: (0,) * len(shape))
    row = lambda w: pl.BlockSpec((tm, w), lambda i: (i, 0))
    tab = pl.BlockSpec((tm, LANES), lambda i: (i % ns, 0))
    outs = [(1024, BF16), (1024, BF16), (1024, BF16), (F_WIDTH, F32), (512, BF16), (256, BF16),
            (256, BF16), (3 * D_MODEL, F32)]
    return pl.pallas_call(
        _in_proj_kernel,
        grid=(nt,),
        in_specs=[row(D_MODEL), full((1, D_MODEL)), full((D_MODEL, _W1_COLS)),
                  full((1, MLA_Q_LORA)), full((MLA_Q_LORA, 1024)), full((MLA_Q_LORA, 1024)),
                  full((1, MLA_KV_LORA)), full((MLA_KV_LORA, 2048)),
                  tab, tab, tab, tab],
        out_specs=[row(w) for w, _ in outs],
        out_shape=[jax.ShapeDtypeStruct((T, w), dt) for w, dt in outs],
        compiler_params=_cparams(("parallel",)),
        name="in_proj",
    )(x2, lw["norm_mix_g"], lw["w1"], lw["q_norm_g"], lw["w_uq"], lw["w_uq_rot"],
      lw["kv_norm_g"], lw["w_ukv"], tabs["cm"], tabs["sm"], tabs["cg"], tabs["sg"])


def _mla_attn_kernel(q_ref, k_ref, v_ref, o_ref, s_ref, p_ref, *, tk):
    tq = q_ref.shape[0]
    nk = k_ref.shape[0] // tk
    q = q_ref[...]

    def chunk(ref, c):
        return ref[pl.ds(pl.multiple_of(c * tk, tk), tk), :]

    def step(slot, v_prev, k_next, carry):
        m, alpha, acc = carry
        pv = _dot(p_ref[1 - slot], v_prev)
        s_ref[1 - slot] = _dot_nt(q, k_next)
        s = s_ref[slot]
        m_new = jnp.maximum(m, jnp.max(s, axis=-1, keepdims=True))
        p_ref[slot] = jnp.exp2(s - m_new).astype(BF16)
        return m_new, jnp.exp2(m - m_new), acc * alpha + pv

    def static_chunk(ref, c):
        return ref[c * tk:(c + 1) * tk, :]

    s_ref[0] = _dot_nt(q, static_chunk(k_ref, 0))
    p_ref[1] = jnp.zeros(p_ref.shape[1:], BF16)
    carry = (jnp.full((tq, 1), NEG, F32), jnp.ones((tq, 1), F32), jnp.zeros((tq, LANES), F32))
    for c in range(nk):
        carry = step(c % 2, static_chunk(v_ref, max(c - 1, 0)),
                     static_chunk(k_ref, min(c + 1, nk - 1)), carry)
    _, alpha, acc = carry
    acc = acc * alpha + _dot(p_ref[1], static_chunk(v_ref, nk - 1))
    o_ref[...] = (acc / acc[:, MLA_V:MLA_V + 1]).astype(o_ref.dtype)


def _mla_attn(q, k, v, B, S, tq, tk):
    assert (S // tk) % 2 == 0
    q3, k3, v3 = (a.reshape(B, S, MLA_HEADS * LANES) for a in (q, k, v))
    out = pl.pallas_call(
        functools.partial(_mla_attn_kernel, tk=tk),
        scratch_shapes=[pltpu.VMEM((2, tq, tk), F32), pltpu.VMEM((2, tq, tk), BF16)],
        grid=(B, MLA_HEADS, S // tq),
        in_specs=[pl.BlockSpec((None, tq, LANES), lambda b, h, i: (b, i, h)),
                  pl.BlockSpec((None, S, LANES), lambda b, h, i: (b, 0, h)),
                  pl.BlockSpec((None, S, LANES), lambda b, h, i: (b, 0, h))],
        out_specs=pl.BlockSpec((None, tq, LANES), lambda b, h, i: (b, i, h)),
        out_shape=jax.ShapeDtypeStruct((B, S, MLA_HEADS * LANES), BF16),
        compiler_params=_cparams(("parallel", "parallel", "parallel")),
        name="mla_attn",
    )(q3, k3, v3)
    return out.reshape(B * S, MLA_HEADS * LANES)


def _fft1_kernel(w_ref, x_ref, o_ref):
    o_ref[...] = jnp.dot(w_ref[...], x_ref[...], preferred_element_type=F32, precision=HIGHEST)


def _fft2_kernel(m_ref, a_ref, cc_ref, cs_ref, o_ref, *, scale):
    aa = jnp.concatenate([a_ref[0], a_ref[1]], axis=0)
    y = jnp.dot(m_ref[...], aa, preferred_element_type=F32, precision=HIGHEST)
    n2 = y.shape[0] // 2
    out = (jnp.dot(y[:n2], cc_ref[...], preferred_element_type=F32, precision=HIGHEST)
           + jnp.dot(y[n2:], cs_ref[...], preferred_element_type=F32, precision=HIGHEST))
    o_ref[...] = out * scale


def _fourier(xf, ft, B, S):
    n1, n2 = ft["n1"], ft["n2"]
    cols = n2 * F_WIDTH
    tc = min(cols, 4096)
    x3 = xf.reshape(B, n1, cols)
    a = pl.pallas_call(
        _fft1_kernel,
        grid=(B, cols // tc),
        in_specs=[pl.BlockSpec((2 * n1, n1), lambda b, j: (0, 0)),
                  pl.BlockSpec((None, n1, tc), lambda b, j: (b, 0, j))],
        out_specs=pl.BlockSpec((None, 2 * n1, tc), lambda b, j: (b, 0, j)),
        out_shape=jax.ShapeDtypeStruct((B, 2 * n1, cols), F32),
        compiler_params=_cparams(("parallel", "parallel")),
        name="fourier_stage1",
    )(ft["w1"], x3)
    a5 = a.reshape(B, 2, n1, n2, F_WIDTH)
    y = pl.pallas_call(
        functools.partial(_fft2_kernel, scale=1.0 / math.sqrt(S * F_GROUP_DIM)),
        grid=(B, n1),
        in_specs=[pl.BlockSpec((None, 2 * n2, 2 * n2), lambda b, k: (k, 0, 0)),
                  pl.BlockSpec((None, 2, None, n2, F_WIDTH), lambda b, k: (b, 0, k, 0, 0)),
                  pl.BlockSpec((F_WIDTH, F_WIDTH), lambda b, k: (0, 0)),
                  pl.BlockSpec((F_WIDTH, F_WIDTH), lambda b, k: (0, 0))],
        out_specs=pl.BlockSpec((None, None, n2, F_WIDTH), lambda b, k: (b, k, 0, 0)),
        out_shape=jax.ShapeDtypeStruct((B, n1, n2, F_WIDTH), F32),
        compiler_params=_cparams(("parallel", "parallel")),
        name="fourier_stage2",
    )(ft["m2"], a5, ft["cc"], ft["cs"])
    return jnp.transpose(y, (0, 2, 1, 3)).reshape(B * S, F_WIDTH)


def _gqa_kernel(sink_ref, q_ref, kp_ref, kc_ref, kn_ref, vp_ref, vc_ref, vn_ref, o_ref, *, S):
    n = pl.program_id(1)
    tq, halo = q_ref.shape[0], kp_ref.shape[0]
    width = tq + 2 * halo
    qpos = n * tq + lax.broadcasted_iota(jnp.int32, (tq, width), 0)
    kpos = n * tq - halo + lax.broadcasted_iota(jnp.int32, (tq, width), 1)
    mask = (jnp.abs(qpos - kpos) <= WINDOW) & (kpos >= 0) & (kpos < S)
    for hd in range(GQA_HEADS):
        kh = hd // GQA_GROUP
        qs = slice(hd * LANES, (hd + 1) * LANES)
        ks = slice(kh * LANES, (kh + 1) * LANES)
        q = q_ref[:, qs]
        s = jnp.concatenate([_dot_nt(q, kp_ref[:, ks]), _dot_nt(q, kc_ref[:, ks]),
                             _dot_nt(q, kn_ref[:, ks])], axis=-1)
        s = jnp.where(mask, s, NEG)
        sk = sink_ref[hd]
        m = jnp.maximum(jnp.max(s, axis=-1, keepdims=True), sk)
        p = jnp.exp(s - m)
        den = jnp.sum(p, axis=-1, keepdims=True) + jnp.exp(sk - m)
        pb = (p / den).astype(BF16)
        o = (_dot(pb[:, :halo], vp_ref[:, ks]) + _dot(pb[:, halo:halo + tq], vc_ref[:, ks])
             + _dot(pb[:, halo + tq:], vn_ref[:, ks]))
        o_ref[:, qs] = o.astype(o_ref.dtype)


def _gqa(qg, kg, vg, sink, B, S, tq):
    halo = WINDOW
    per = tq // halo
    nb = S // halo
    q3 = qg.reshape(B, S, GQA_HEADS * LANES)
    k3 = kg.reshape(B, S, GQA_KV_HEADS * LANES)
    v3 = vg.reshape(B, S, GQA_KV_HEADS * LANES)
    kvw = GQA_KV_HEADS * LANES
    prev = pl.BlockSpec((None, halo, kvw), lambda b, n: (b, jnp.maximum(n * per - 1, 0), 0))
    cur = pl.BlockSpec((None, tq, kvw), lambda b, n: (b, n, 0))
    nxt = pl.BlockSpec((None, halo, kvw), lambda b, n: (b, jnp.minimum((n + 1) * per, nb - 1), 0))
    out = pl.pallas_call(
        functools.partial(_gqa_kernel, S=S),
        grid=(B, S // tq),
        in_specs=[pl.BlockSpec(memory_space=pltpu.SMEM),
                  pl.BlockSpec((None, tq, GQA_HEADS * LANES), lambda b, n: (b, n, 0)),
                  prev, cur, nxt, prev, cur, nxt],
        out_specs=pl.BlockSpec((None, tq, GQA_HEADS * LANES), lambda b, n: (b, n, 0)),
        out_shape=jax.ShapeDtypeStruct((B, S, GQA_HEADS * LANES), BF16),
        compiler_params=_cparams(("parallel", "parallel")),
        name="gqa_window",
    )(sink, q3, k3, k3, k3, v3, v3, v3)
    return out.reshape(B * S, GQA_HEADS * LANES)


def _merge_kernel(x_ref, attn_ref, four_ref, og_ref, gate_ref, wo_ref, fw_ref, fb_ref, gwo_ref,
                  wout_ref, o_ref):
    mla_o = _dot(attn_ref[...], wo_ref[...])
    fnet_o = _dot(four_ref[...].astype(BF16), fw_ref[...]) + fb_ref[...]
    gqa_o = _dot(og_ref[...], gwo_ref[...])
    d = D_MODEL
    merged = (gate_ref[:, 0:d] * mla_o + gate_ref[:, d:2 * d] * fnet_o
              + gate_ref[:, 2 * d:3 * d] * gqa_o)
    o_ref[...] = x_ref[...] + _dot(merged.astype(BF16), wout_ref[...])


def _merge(x2, attn, four, og, gate, lw, tm):
    T = x2.shape[0]
    full = lambda shape: pl.BlockSpec(shape, lambda i: (0,) * len(shape))
    row = lambda w: pl.BlockSpec((tm, w), lambda i: (i, 0))
    return pl.pallas_call(
        _merge_kernel,
        grid=(T // tm,),
        in_specs=[row(D_MODEL), row(1024), row(F_WIDTH), row(512), row(3 * D_MODEL),
                  full((1024, D_MODEL)), full((F_WIDTH, D_MODEL)), full((1, D_MODEL)),
                  full((512, D_MODEL)), full((D_MODEL, D_MODEL))],
        out_specs=row(D_MODEL),
        out_shape=jax.ShapeDtypeStruct((T, D_MODEL), F32),
        compiler_params=_cparams(("parallel",)),
        name="merge",
    )(x2, attn, four, og, gate, lw["mla_w_o"], lw["fnet_w"], lw["fnet_b"], lw["gqa_w_o"],
      lw["w_out"])


LOG2E = math.log2(math.e)
PEER_BLK = 256
PEER_RANKS = PEER_TOPK + 1


def _sort_network(n):
    pairs = []

    def merge(lo, m, r):
        step = 2 * r
        if step < m:
            merge(lo, m, step)
            merge(lo + r, m, step)
            pairs.extend((i, i + r) for i in range(lo + r, lo + m - r, step))
        else:
            pairs.append((lo, lo + r))

    def sort(lo, m):
        if m > 1:
            sort(lo, m // 2)
            sort(lo + m // 2, m // 2)
            merge(lo, m, 1)

    sort(0, n)
    return pairs


def _top_of_sorted_lists(lists, k, extra=None):
    lists = list(lists)
    neg = jnp.full(lists[0].shape, -jnp.inf, F32)
    vals = []
    for r in range(k):
        mx = jnp.max(lists[0], axis=0, keepdims=True)
        if extra is not None:
            mx = jnp.maximum(mx, jnp.max(extra, axis=0, keepdims=True))
        vals.append(mx)
        if r == k - 1:
            break
        win = lists[0] == mx
        for v in range(min(len(lists), k - 1 - r)):
            lists[v] = jnp.where(win, lists[v + 1] if v + 1 < len(lists) else neg, lists[v])
        if extra is not None:
            extra = jnp.where(extra == mx, -jnp.inf, extra)
    return vals


def _top_sorted(s, k):
    lists = [s[8 * v:8 * v + 8] for v in range(s.shape[0] // 8)]
    for i, j in _sort_network(len(lists)):
        lists[i], lists[j] = jnp.maximum(lists[i], lists[j]), jnp.minimum(lists[i], lists[j])
    return _top_of_sorted_lists(lists, k)


def _peer_prep_kernel(x_ref, g_ref, wq_ref, keys_ref, hn_ref, st1_ref, s2k_ref, kap_ref):
    hn = _rms(x_ref[...], g_ref[...]).astype(BF16)
    hn_ref[...] = hn
    q = _dot(hn, wq_ref[...]).astype(BF16)
    tm = q.shape[0]
    kaps = []
    for hd in range(PEER_HEADS):
        sts, tops = [], []
        for p in range(2):
            hp = 2 * hd + p
            st = _dot_nt(keys_ref[hp], q[:, hp * PEER_HALF:(hp + 1) * PEER_HALF])
            sts.append(st)
            tops.append(_top_sorted(st, PEER_RANKS))
        a, b = tops
        a_lo = jnp.concatenate(a[:8], axis=0)
        a_hi = jnp.concatenate(a[8:] + [jnp.full((7, tm), -jnp.inf, F32)], axis=0)
        top = _top_of_sorted_lists([a_lo + b[r] for r in range(PEER_RANKS)], PEER_RANKS,
                                   extra=a_hi + b[0])
        m = top[0]
        z = sum(jnp.exp(t - m) for t in top[:PEER_TOPK])
        mp = m + jnp.log(z)
        thr = 0.5 * (top[PEER_TOPK - 1] + top[PEER_TOPK])
        st1_ref[hd] = sts[0] * LOG2E
        s2k_ref[hd] = (sts[1] - mp) * LOG2E
        kaps.append((thr - mp) * LOG2E)
    kap_ref[...] = jnp.concatenate(kaps, axis=0)


def _peer_prep(x2, lw, tm):
    T = x2.shape[0]
    sc = pl.BlockSpec((PEER_HEADS, PEER_N_KEYS, tm), lambda i: (0, 0, i))
    return pl.pallas_call(
        _peer_prep_kernel,
        grid=(T // tm,),
        in_specs=[pl.BlockSpec((tm, D_MODEL), lambda i: (i, 0)),
                  pl.BlockSpec((1, D_MODEL), lambda i: (0, 0)),
                  pl.BlockSpec((D_MODEL, 2 * PEER_HEADS * PEER_HALF), lambda i: (0, 0)),
                  pl.BlockSpec((2 * PEER_HEADS, PEER_N_KEYS, PEER_HALF), lambda i: (0, 0, 0))],
        out_specs=[pl.BlockSpec((tm, D_MODEL), lambda i: (i, 0)), sc, sc,
                   pl.BlockSpec((PEER_HEADS, tm), lambda i: (0, i))],
        out_shape=[jax.ShapeDtypeStruct((T, D_MODEL), BF16),
                   jax.ShapeDtypeStruct((PEER_HEADS, PEER_N_KEYS, T), F32),
                   jax.ShapeDtypeStruct((PEER_HEADS, PEER_N_KEYS, T), F32),
                   jax.ShapeDtypeStruct((PEER_HEADS, T), F32)],
        compiler_params=_cparams(("parallel",)),
        name="peer_prep",
    )(x2, lw["norm_ffn_g"], lw["peer_w_q"], lw["peer_keys"])


def _peer_main_kernel(x_ref, hn_ref, st1_ref, s2k_ref, kap_ref, u_ref, vt_ref, fg_ref, o_ref,
                      acc_ref, act_ref, w_ref, *, n_i, final_norm):
    e = pl.program_id(1)
    tm = hn_ref.shape[0]

    @pl.when(e == 0)
    def _():
        acc_ref[...] = jnp.zeros_like(acc_ref)

    n_blk = n_i * PEER_N_KEYS // PEER_BLK
    per_blk = PEER_BLK // PEER_N_KEYS

    def activations(p):
        a = _dot_nt(u_ref[p * PEER_BLK:(p + 1) * PEER_BLK, :], hn_ref[...])
        return 0.5 * a * (1.0 + lax.erf(a * (1.0 / math.sqrt(2.0))))

    act_ref[0] = activations(0)
    for p in range(n_blk):
        slot = p % 2
        if p + 1 < n_blk:
            act_ref[1 - slot] = activations(p + 1)
        if p > 0:
            acc_ref[...] += _dot(vt_ref[p - 1], w_ref[1 - slot])
        for ii in range(per_blk):
            i = p * per_blk + ii
            rows = slice(ii * PEER_N_KEYS, (ii + 1) * PEER_N_KEYS)
            for c in range(tm // LANES):
                ls = slice(c * LANES, (c + 1) * LANES)
                g = None
                for hd in range(PEER_HEADS):
                    z = s2k_ref[hd, :, ls] + st1_ref[hd, i:i + 1, ls]
                    t = jnp.where(z > kap_ref[hd:hd + 1, ls], jnp.exp2(z), 0.0)
                    g = t if g is None else g + t
                w_ref[slot, rows, ls] = (g * act_ref[slot, rows, ls]).astype(BF16)
    acc_ref[...] += _dot(vt_ref[n_blk - 1], w_ref[(n_blk - 1) % 2])

    @pl.when(e == pl.num_programs(1) - 1)
    def _():
        y = x_ref[...] + acc_ref[...].T
        if final_norm:
            y = _rms(y, fg_ref[...])
        o_ref[...] = y


def _peer_main(x2, hn, st1, s2k, kap, lw, final_g, tm, n_i, final_norm):
    T = x2.shape[0]
    et = n_i * PEER_N_KEYS
    sc = pl.BlockSpec((PEER_HEADS, PEER_N_KEYS, tm), lambda t, e: (0, 0, t))
    return pl.pallas_call(
        functools.partial(_peer_main_kernel, n_i=n_i, final_norm=final_norm),
        grid=(T // tm, PEER_N_EXPERTS // et),
        in_specs=[pl.BlockSpec((tm, D_MODEL), lambda t, e: (t, 0)),
                  pl.BlockSpec((tm, D_MODEL), lambda t, e: (t, 0)),
                  pl.BlockSpec((PEER_HEADS, n_i, tm), lambda t, e: (0, e, t)),
                  sc,
                  pl.BlockSpec((PEER_HEADS, tm), lambda t, e: (0, t)),
                  pl.BlockSpec((et, D_MODEL), lambda t, e: (e, 0)),
                  pl.BlockSpec((et // PEER_BLK, D_MODEL, PEER_BLK), lambda t, e: (e, 0, 0)),
                  pl.BlockSpec((1, D_MODEL), lambda t, e: (0, 0))],
        out_specs=pl.BlockSpec((tm, D_MODEL), lambda t, e: (t, 0)),
        out_shape=jax.ShapeDtypeStruct((T, D_MODEL), F32),
        scratch_shapes=[pltpu.VMEM((D_MODEL, tm), F32), pltpu.VMEM((2, PEER_BLK, tm), F32),
                        pltpu.VMEM((2, PEER_BLK, tm), BF16)],
        compiler_params=_cparams(("parallel", "arbitrary")),
        name="peer_main",
    )(x2, hn, st1, s2k, kap, lw["peer_u"], lw["peer_vt"], final_g)


def _rot_half_cols(w):
    d = w.shape[-1]
    return jnp.concatenate([-w[..., d // 2:], w[..., :d // 2]], axis=-1)


def _pad_cols(w, left, total):
    return jnp.pad(w, ((0, 0), (left, total - left - w.shape[-1])))


def _head_groups(w, n_heads, width, left=0):
    return jnp.concatenate(
        [_pad_cols(w[:, h * width:(h + 1) * width], left, LANES) for h in range(n_heads)], axis=-1)


def _layer_weights(p, l):
    w_in = p["w_in"][l]
    widths = (MLA_Q_LORA, MLA_KV_LORA, MLA_ROPE, F_WIDTH, GQA_HEADS * GQA_HEAD_DIM,
              GQA_KV_HEADS * GQA_HEAD_DIM, GQA_KV_HEADS * GQA_HEAD_DIM, 3 * D_MODEL)
    offs = np.cumsum((0,) + widths)
    wcq, wckv, wkr, wxf, wgq, wgk, wgv, wgate = (w_in[:, offs[i]:offs[i + 1]] for i in range(8))

    def rot_heads(w, n_heads, width):
        return jnp.concatenate(
            [_rot_half_cols(w[:, h * width:(h + 1) * width]) for h in range(n_heads)], axis=-1)

    w1 = jnp.concatenate([
        wcq, wckv,
        _pad_cols(wkr, MLA_NOPE, LANES), _pad_cols(_rot_half_cols(wkr), MLA_NOPE, LANES),
        wxf,
        _head_groups(wgq, GQA_HEADS, GQA_HEAD_DIM),
        _head_groups(rot_heads(wgq, GQA_HEADS, GQA_HEAD_DIM), GQA_HEADS, GQA_HEAD_DIM),
        _head_groups(wgk, GQA_KV_HEADS, GQA_HEAD_DIM),
        _head_groups(rot_heads(wgk, GQA_KV_HEADS, GQA_HEAD_DIM), GQA_KV_HEADS, GQA_HEAD_DIM),
        _head_groups(wgv, GQA_KV_HEADS, GQA_HEAD_DIM),
        wgate], axis=-1).astype(BF16)
    assert w1.shape[-1] == _W1_COLS

    w_uq = p["mla_w_uq"][l]
    qd = MLA_NOPE + MLA_ROPE
    uq, uqr = [], []
    for h in range(MLA_HEADS):
        blk = w_uq[:, h * qd:(h + 1) * qd]
        uq.append(_pad_cols(blk, 0, LANES))
        uqr.append(_pad_cols(_rot_half_cols(blk[:, MLA_NOPE:]), MLA_NOPE, LANES))
    w_ukv = p["mla_w_ukv"][l]
    kd = MLA_NOPE + MLA_V
    uk = [_pad_cols(w_ukv[:, h * kd:h * kd + MLA_NOPE], 0, LANES) for h in range(MLA_HEADS)]
    uv = [_pad_cols(w_ukv[:, h * kd + MLA_NOPE:(h + 1) * kd], 0, LANES) for h in range(MLA_HEADS)]

    def pad_rows(w, n_heads, width):
        return jnp.concatenate(
            [jnp.pad(w[h * width:(h + 1) * width], ((0, LANES - width), (0, 0)))
             for h in range(n_heads)], axis=0)

    return {
        "norm_mix_g": p["norm_mix_g"][l][None, :],
        "w1": w1,
        "q_norm_g": p["mla_q_norm_g"][l][None, :],
        "w_uq": jnp.concatenate(uq, axis=-1).astype(BF16),
        "w_uq_rot": jnp.concatenate(uqr, axis=-1).astype(BF16),
        "kv_norm_g": p["mla_kv_norm_g"][l][None, :],
        "w_ukv": jnp.concatenate(uk + uv, axis=-1).astype(BF16),
        "mla_w_o": pad_rows(p["mla_w_o"][l], MLA_HEADS, MLA_V).astype(BF16),
        "fnet_w": p["fnet_w"][l].astype(BF16),
        "fnet_b": p["fnet_b"][l][None, :],
        "gqa_sink": p["gqa_sink"][l],
        "gqa_w_o": pad_rows(p["gqa_w_o"][l], GQA_HEADS, GQA_HEAD_DIM).astype(BF16),
        "w_out": p["w_out"][l].astype(BF16),
        "norm_ffn_g": p["norm_ffn_g"][l][None, :],
        "peer_w_q": p["peer_w_q"][l].astype(BF16),
        "peer_keys": p["peer_keys"][l].reshape(2 * PEER_HEADS, PEER_N_KEYS, PEER_HALF).astype(BF16),
        "peer_u": p["peer_u"][l].astype(BF16),
        "peer_vt": p["peer_v"][l].astype(BF16).reshape(
            PEER_N_EXPERTS // PEER_BLK, PEER_BLK, D_MODEL).transpose(0, 2, 1),
    }


def _rope_tables(S):
    pos = jnp.arange(S, dtype=F32)[:, None]

    def cs(d):
        inv = 1.0 / (ROPE_THETA ** (jnp.arange(0, d, 2, dtype=F32) / d))
        ang = pos * inv[None, :]
        return (jnp.concatenate([jnp.cos(ang)] * 2, axis=-1),
                jnp.concatenate([jnp.sin(ang)] * 2, axis=-1))

    c, s = cs(MLA_ROPE)
    rest = LANES - MLA_NOPE - MLA_ROPE
    cm = jnp.concatenate([jnp.ones((S, MLA_NOPE), F32), c, jnp.zeros((S, rest), F32)], axis=-1)
    sm = jnp.concatenate([jnp.zeros((S, MLA_NOPE), F32), s, jnp.zeros((S, rest), F32)], axis=-1)
    c, s = cs(GQA_HEAD_DIM)
    zero = jnp.zeros((S, LANES - GQA_HEAD_DIM), F32)
    return {"cm": cm, "sm": sm, "cg": jnp.concatenate([c, zero], axis=-1),
            "sg": jnp.concatenate([s, zero], axis=-1)}


def _fourier_tables(S):
    n2 = 128
    n1 = S // n2

    def cos_sin(num, den):
        ang = (2.0 * math.pi / den) * (num % den).astype(F32)
        return jnp.cos(ang), jnp.sin(ang)

    i1 = jnp.arange(n1, dtype=jnp.int32)
    c1, s1 = cos_sin(i1[:, None] * i1[None, :], n1)
    w1 = jnp.concatenate([c1, -s1], axis=0)
    i2 = jnp.arange(n2, dtype=jnp.int32)
    kk = i1[:, None, None] + n1 * i2[None, :, None]
    mr, ms = cos_sin(kk * i2[None, None, :], S)
    m2 = jnp.concatenate([jnp.concatenate([mr, ms], axis=-1),
                          jnp.concatenate([-ms, mr], axis=-1)], axis=1)
    ic = jnp.arange(F_GROUP_DIM, dtype=jnp.int32)
    cc, cs = cos_sin(ic[:, None] * ic[None, :], F_GROUP_DIM)
    eye = jnp.eye(F_GROUPS, dtype=F32)
    return {"n1": n1, "n2": n2, "w1": w1, "m2": m2, "cc": jnp.kron(eye, cc),
            "cs": jnp.kron(eye, cs)}


def _pick(S, pref):
    t = pref
    while S % t:
        t //= 2
    return t


def _trunk(x, layers, final_g, rope_t, four_t):
    B, S, _ = x.shape
    T = B * S
    x2 = x.reshape(T, D_MODEL)
    tm = _pick(S, 256)
    n_layers = len(layers)
    for l, lw in enumerate(layers):
        q, k, v, xf, qg, kg, vg, gate = _in_proj(x2, lw, rope_t, S, tm)
        attn = _mla_attn(q, k, v, B, S, _pick(S, 1024), _pick(S // 2, 512))
        four = _fourier(xf, four_t, B, S)
        og = _gqa(qg, kg, vg, lw["gqa_sink"], B, S, _pick(S, 512))
        x2 = _merge(x2, attn, four, og, gate, lw, tm)
        hn, st1, s2k, kap = _peer_prep(x2, lw, _pick(T, 256))
        x2 = _peer_main(x2, hn, st1, s2k, kap, lw, final_g, _pick(T, 512), 16,
                        final_norm=(l == n_layers - 1))
    return x2.reshape(B, S, D_MODEL)


def kernel(x_prompt, x_sample, norm_mix_g, w_in, mla_q_norm_g, mla_w_uq, mla_kv_norm_g, mla_w_ukv, mla_w_o, fnet_w, fnet_b, gqa_sink, gqa_w_o, w_out, norm_ffn_g, peer_w_q, peer_keys, peer_u, peer_v, final_norm_g):
    p = dict(norm_mix_g=norm_mix_g, w_in=w_in, mla_q_norm_g=mla_q_norm_g, mla_w_uq=mla_w_uq,
             mla_kv_norm_g=mla_kv_norm_g, mla_w_ukv=mla_w_ukv, mla_w_o=mla_w_o, fnet_w=fnet_w,
             fnet_b=fnet_b, gqa_sink=gqa_sink, gqa_w_o=gqa_w_o, w_out=w_out,
             norm_ffn_g=norm_ffn_g, peer_w_q=peer_w_q, peer_keys=peer_keys, peer_u=peer_u,
             peer_v=peer_v)
    layers = [_layer_weights(p, l) for l in range(w_in.shape[0])]
    final_g = final_norm_g[None, :]
    outs = []
    for x in (x_prompt, x_sample):
        S = x.shape[1]
        outs.append(_trunk(x, layers, final_g, _rope_tables(S), _fourier_tables(S)))
    return tuple(outs)
```

```python
import functools
import math

import jax
import jax.numpy as jnp
import numpy as np
from jax import lax
from jax.experimental import pallas as pl
from jax.experimental.pallas import tpu as pltpu

D_MODEL = 1024
EPS = 1e-6
ROPE_THETA = 10000.0
NEG = -1e30
LANES = 128

MLA_HEADS = 8
MLA_NOPE = 64
MLA_ROPE = 32
MLA_V = 64
MLA_Q_LORA = 384
MLA_KV_LORA = 256
F_GROUPS = 4
F_GROUP_DIM = 64
F_WIDTH = F_GROUPS * F_GROUP_DIM
GQA_HEADS = 4
GQA_KV_HEADS = 2
GQA_GROUP = GQA_HEADS // GQA_KV_HEADS
GQA_HEAD_DIM = 64
WINDOW = 128
PEER_HEADS = 8
PEER_N_KEYS = 128
PEER_N_EXPERTS = PEER_N_KEYS * PEER_N_KEYS
PEER_HALF = 128
PEER_TOPK = 16

VMEM_LIMIT = 56 * 1024 * 1024

BF16 = jnp.bfloat16
F32 = jnp.float32
HIGHEST = lax.Precision.HIGHEST


def _cparams(sem):
    return pltpu.CompilerParams(dimension_semantics=sem, vmem_limit_bytes=VMEM_LIMIT)


def _rms(x, g):
    return x * lax.rsqrt(jnp.mean(x * x, axis=-1, keepdims=True) + EPS) * g


def _dot(a, b):
    return jnp.dot(a, b, preferred_element_type=F32)


def _dot_nt(a, b):
    return lax.dot_general(a, b, (((1,), (1,)), ((), ())), preferred_element_type=F32)


_C_CQ = (0, 384)
_C_CKV = (384, 640)
_C_KR = (640, 768)
_C_KRR = (768, 896)
_C_XF = (896, 1152)
_C_GQ = (1152, 1664)
_C_GQR = (1664, 2176)
_C_GK = (2176, 2432)
_C_GKR = (2432, 2688)
_C_GV = (2688, 2944)
_C_GATE = (2944, 6016)
_W1_COLS = 6016


def _in_proj_kernel(x_ref, g_ref, w1_ref, qng_ref, wuq_ref, wuqr_ref, kvng_ref, wukv_ref,
                    cm_ref, sm_ref, cg_ref, sg_ref,
                    q_ref, k_ref, v_ref, xf_ref, qg_ref, kg_ref, vg_ref, gate_ref):
    h = _rms(x_ref[...], g_ref[...]).astype(BF16)

    def proj(c):
        return _dot(h, w1_ref[:, c[0]:c[1]])

    cm, sm = cm_ref[...], sm_ref[...]
    cg, sg = cg_ref[...], sg_ref[...]

    cqn = _rms(proj(_C_CQ), qng_ref[...]).astype(BF16)
    q = _dot(cqn, wuq_ref[...])
    qr = _dot(cqn, wuqr_ref[...])
    mla_scale = (MLA_NOPE + MLA_ROPE) ** -0.5 * math.log2(math.e)
    for hd in range(MLA_HEADS):
        sl = slice(hd * LANES, (hd + 1) * LANES)
        q_ref[:, sl] = ((q[:, sl] * cm + qr[:, sl] * sm) * mla_scale).astype(BF16)

    ckvn = _rms(proj(_C_CKV), kvng_ref[...]).astype(BF16)
    kv = _dot(ckvn, wukv_ref[...])
    k_rope = proj(_C_KR) * cm + proj(_C_KRR) * sm
    ones_col = (lax.broadcasted_iota(jnp.int32, (1, LANES), 1) == MLA_V).astype(F32)
    for hd in range(MLA_HEADS):
        sl = slice(hd * LANES, (hd + 1) * LANES)
        k_ref[:, sl] = (kv[:, sl] + k_rope).astype(BF16)
        vs = slice(MLA_HEADS * LANES + hd * LANES, MLA_HEADS * LANES + (hd + 1) * LANES)
        v_ref[:, sl] = (kv[:, vs] + ones_col).astype(BF16)

    xf_ref[...] = proj(_C_XF)

    gq, gqr = proj(_C_GQ), proj(_C_GQR)
    gqa_scale = GQA_HEAD_DIM ** -0.5
    for hd in range(GQA_HEADS):
        sl = slice(hd * LANES, (hd + 1) * LANES)
        qg_ref[:, sl] = ((gq[:, sl] * cg + gqr[:, sl] * sg) * gqa_scale).astype(BF16)
    gk, gkr = proj(_C_GK), proj(_C_GKR)
    for hd in range(GQA_KV_HEADS):
        sl = slice(hd * LANES, (hd + 1) * LANES)
        kg_ref[:, sl] = (gk[:, sl] * cg + gkr[:, sl] * sg).astype(BF16)
    vg_ref[...] = proj(_C_GV).astype(BF16)

    gate = proj(_C_GATE)
    gate_ref[...] = 1.0 / (1.0 + jnp.exp(-gate))


def _in_proj(x2, lw, tabs, S, tm):
    T = x2.shape[0]
    nt = T // tm
    ns = S // tm
    full = lambda shape: pl.BlockSpec(shape, lambda i: (0,) * len(shape))
    row = lambda w: pl.BlockSpec((tm, w), lambda i: (i, 0))
    tab = pl.BlockSpec((tm, LANES), lambda i: (i % ns, 0))
    outs = [(1024, BF16), (1024, BF16), (1024, BF16), (F_WIDTH, F32), (512, BF16), (256, BF16),
            (256, BF16), (3 * D_MODEL, F32)]
    return pl.pallas_call(
        _in_proj_kernel,
        grid=(nt,),
        in_specs=[row(D_MODEL), full((1, D_MODEL)), full((D_MODEL, _W1_COLS)),
                  full((1, MLA_Q_LORA)), full((MLA_Q_LORA, 1024)), full((MLA_Q_LORA, 1024)),
                  full((1, MLA_KV_LORA)), full((MLA_KV_LORA, 2048)),
                  tab, tab, tab, tab],
        out_specs=[row(w) for w, _ in outs],
        out_shape=[jax.ShapeDtypeStruct((T, w), dt) for w, dt in outs],
        compiler_params=_cparams(("parallel",)),
        name="in_proj",
    )(x2, lw["norm_mix_g"], lw["w1"], lw["q_norm_g"], lw["w_uq"], lw["w_uq_rot"],
      lw["kv_norm_g"], lw["w_ukv"], tabs["cm"], tabs["sm"], tabs["cg"], tabs["sg"])


def _mla_attn_kernel(q_ref, k_ref, v_ref, o_ref, s_ref, p_ref, *, tk):
    tq = q_ref.shape[0]
    nk = k_ref.shape[0] // tk
    q = q_ref[...]

    def chunk(ref, c):
        return ref[pl.ds(pl.multiple_of(c * tk, tk), tk), :]

    def step(slot, v_prev, k_next, carry):
        m, alpha, acc = carry
        pv = _dot(p_ref[1 - slot], v_prev)
        s_ref[1 - slot] = _dot_nt(q, k_next)
        s = s_ref[slot]
        m_new = jnp.maximum(m, jnp.max(s, axis=-1, keepdims=True))
        p_ref[slot] = jnp.exp2(s - m_new).astype(BF16)
        return m_new, jnp.exp2(m - m_new), acc * alpha + pv

    def static_chunk(ref, c):
        return ref[c * tk:(c + 1) * tk, :]

    s_ref[0] = _dot_nt(q, static_chunk(k_ref, 0))
    p_ref[1] = jnp.zeros(p_ref.shape[1:], BF16)
    carry = (jnp.full((tq, 1), NEG, F32), jnp.ones((tq, 1), F32), jnp.zeros((tq, LANES), F32))
    for c in range(nk):
        carry = step(c % 2, static_chunk(v_ref, max(c - 1, 0)),
                     static_chunk(k_ref, min(c + 1, nk - 1)), carry)
    _, alpha, acc = carry
    acc = acc * alpha + _dot(p_ref[1], static_chunk(v_ref, nk - 1))
    o_ref[...] = (acc / acc[:, MLA_V:MLA_V + 1]).astype(o_ref.dtype)


def _mla_attn(q, k, v, B, S, tq, tk):
    assert (S // tk) % 2 == 0
    q3, k3, v3 = (a.reshape(B, S, MLA_HEADS * LANES) for a in (q, k, v))
    out = pl.pallas_call(
        functools.partial(_mla_attn_kernel, tk=tk),
        scratch_shapes=[pltpu.VMEM((2, tq, tk), F32), pltpu.VMEM((2, tq, tk), BF16)],
        grid=(B, MLA_HEADS, S // tq),
        in_specs=[pl.BlockSpec((None, tq, LANES), lambda b, h, i: (b, i, h)),
                  pl.BlockSpec((None, S, LANES), lambda b, h, i: (b, 0, h)),
                  pl.BlockSpec((None, S, LANES), lambda b, h, i: (b, 0, h))],
        out_specs=pl.BlockSpec((None, tq, LANES), lambda b, h, i: (b, i, h)),
        out_shape=jax.ShapeDtypeStruct((B, S, MLA_HEADS * LANES), BF16),
        compiler_params=_cparams(("parallel", "parallel", "parallel")),
        name="mla_attn",
    )(q3, k3, v3)
    return out.reshape(B * S, MLA_HEADS * LANES)


def _fft1_kernel(w_ref, x_ref, o_ref):
    o_ref[...] = jnp.dot(w_ref[...], x_ref[...], preferred_element_type=F32, precision=HIGHEST)


def _fft2_kernel(m_ref, a_ref, cc_ref, cs_ref, o_ref, *, scale):
    aa = jnp.concatenate([a_ref[0], a_ref[1]], axis=0)
    y = jnp.dot(m_ref[...], aa, preferred_element_type=F32, precision=HIGHEST)
    n2 = y.shape[0] // 2
    out = (jnp.dot(y[:n2], cc_ref[...], preferred_element_type=F32, precision=HIGHEST)
           + jnp.dot(y[n2:], cs_ref[...], preferred_element_type=F32, precision=HIGHEST))
    o_ref[...] = out * scale


def _fourier(xf, ft, B, S):
    n1, n2 = ft["n1"], ft["n2"]
    cols = n2 * F_WIDTH
    tc = min(cols, 4096)
    x3 = xf.reshape(B, n1, cols)
    a = pl.pallas_call(
        _fft1_kernel,
        grid=(B, cols // tc),
        in_specs=[pl.BlockSpec((2 * n1, n1), lambda b, j: (0, 0)),
                  pl.BlockSpec((None, n1, tc), lambda b, j: (b, 0, j))],
        out_specs=pl.BlockSpec((None, 2 * n1, tc), lambda b, j: (b, 0, j)),
        out_shape=jax.ShapeDtypeStruct((B, 2 * n1, cols), F32),
        compiler_params=_cparams(("parallel", "parallel")),
        name="fourier_stage1",
    )(ft["w1"], x3)
    a5 = a.reshape(B, 2, n1, n2, F_WIDTH)
    y = pl.pallas_call(
        functools.partial(_fft2_kernel, scale=1.0 / math.sqrt(S * F_GROUP_DIM)),
        grid=(B, n1),
        in_specs=[pl.BlockSpec((None, 2 * n2, 2 * n2), lambda b, k: (k, 0, 0)),
                  pl.BlockSpec((None, 2, None, n2, F_WIDTH), lambda b, k: (b, 0, k, 0, 0)),
                  pl.BlockSpec((F_WIDTH, F_WIDTH), lambda b, k: (0, 0)),
                  pl.BlockSpec((F_WIDTH, F_WIDTH), lambda b, k: (0, 0))],
        out_specs=pl.BlockSpec((None, None, n2, F_WIDTH), lambda b, k: (b, k, 0, 0)),
        out_shape=jax.ShapeDtypeStruct((B, n1, n2, F_WIDTH), F32),
        compiler_params=_cparams(("parallel", "parallel")),
        name="fourier_stage2",
    )(ft["m2"], a5, ft["cc"], ft["cs"])
    return jnp.transpose(y, (0, 2, 1, 3)).reshape(B * S, F_WIDTH)


def _gqa_kernel(sink_ref, q_ref, kp_ref, kc_ref, kn_ref, vp_ref, vc_ref, vn_ref, o_ref, *, S):
    n = pl.program_id(1)
    tq, halo = q_ref.shape[0], kp_ref.shape[0]
    width = tq + 2 * halo
    qpos = n * tq + lax.broadcasted_iota(jnp.int32, (tq, width), 0)
    kpos = n * tq - halo + lax.broadcasted_iota(jnp.int32, (tq, width), 1)
    mask = (jnp.abs(qpos - kpos) <= WINDOW) & (kpos >= 0) & (kpos < S)
    for hd in range(GQA_HEADS):
        kh = hd // GQA_GROUP
        qs = slice(hd * LANES, (hd + 1) * LANES)
        ks = slice(kh * LANES, (kh + 1) * LANES)
        q = q_ref[:, qs]
        s = jnp.concatenate([_dot_nt(q, kp_ref[:, ks]), _dot_nt(q, kc_ref[:, ks]),
                             _dot_nt(q, kn_ref[:, ks])], axis=-1)
        s = jnp.where(mask, s, NEG)
        sk = sink_ref[hd]
        m = jnp.maximum(jnp.max(s, axis=-1, keepdims=True), sk)
        p = jnp.exp(s - m)
        den = jnp.sum(p, axis=-1, keepdims=True) + jnp.exp(sk - m)
        pb = (p / den).astype(BF16)
        o = (_dot(pb[:, :halo], vp_ref[:, ks]) + _dot(pb[:, halo:halo + tq], vc_ref[:, ks])
             + _dot(pb[:, halo + tq:], vn_ref[:, ks]))
        o_ref[:, qs] = o.astype(o_ref.dtype)


def _gqa(qg, kg, vg, sink, B, S, tq):
    halo = WINDOW
    per = tq // halo
    nb = S // halo
    q3 = qg.reshape(B, S, GQA_HEADS * LANES)
    k3 = kg.reshape(B, S, GQA_KV_HEADS * LANES)
    v3 = vg.reshape(B, S, GQA_KV_HEADS * LANES)
    kvw = GQA_KV_HEADS * LANES
    prev = pl.BlockSpec((None, halo, kvw), lambda b, n: (b, jnp.maximum(n * per - 1, 0), 0))
    cur = pl.BlockSpec((None, tq, kvw), lambda b, n: (b, n, 0))
    nxt = pl.BlockSpec((None, halo, kvw), lambda b, n: (b, jnp.minimum((n + 1) * per, nb - 1), 0))
    out = pl.pallas_call(
        functools.partial(_gqa_kernel, S=S),
        grid=(B, S // tq),
        in_specs=[pl.BlockSpec(memory_space=pltpu.SMEM),
                  pl.BlockSpec((None, tq, GQA_HEADS * LANES), lambda b, n: (b, n, 0)),
                  prev, cur, nxt, prev, cur, nxt],
        out_specs=pl.BlockSpec((None, tq, GQA_HEADS * LANES), lambda b, n: (b, n, 0)),
        out_shape=jax.ShapeDtypeStruct((B, S, GQA_HEADS * LANES), BF16),
        compiler_params=_cparams(("parallel", "parallel")),
        name="gqa_window",
    )(sink, q3, k3, k3, k3, v3, v3, v3)
    return out.reshape(B * S, GQA_HEADS * LANES)


def _merge_kernel(x_ref, attn_ref, four_ref, og_ref, gate_ref, wo_ref, fw_ref, fb_ref, gwo_ref,
                  wout_ref, o_ref):
    mla_o = _dot(attn_ref[...], wo_ref[...])
    fnet_o = _dot(four_ref[...].astype(BF16), fw_ref[...]) + fb_ref[...]
    gqa_o = _dot(og_ref[...], gwo_ref[...])
    d = D_MODEL
    merged = (gate_ref[:, 0:d] * mla_o + gate_ref[:, d:2 * d] * fnet_o
              + gate_ref[:, 2 * d:3 * d] * gqa_o)
    o_ref[...] = x_ref[...] + _dot(merged.astype(BF16), wout_ref[...])


def _merge(x2, attn, four, og, gate, lw, tm):
    T = x2.shape[0]
    full = lambda shape: pl.BlockSpec(shape, lambda i: (0,) * len(shape))
    row = lambda w: pl.BlockSpec((tm, w), lambda i: (i, 0))
    return pl.pallas_call(
        _merge_kernel,
        grid=(T // tm,),
        in_specs=[row(D_MODEL), row(1024), row(F_WIDTH), row(512), row(3 * D_MODEL),
                  full((1024, D_MODEL)), full((F_WIDTH, D_MODEL)), full((1, D_MODEL)),
                  full((512, D_MODEL)), full((D_MODEL, D_MODEL))],
        out_specs=row(D_MODEL),
        out_shape=jax.ShapeDtypeStruct((T, D_MODEL), F32),
        compiler_params=_cparams(("parallel",)),
        name="merge",
    )(x2, attn, four, og, gate, lw["mla_w_o"], lw["fnet_w"], lw["fnet_b"], lw["gqa_w_o"],
      lw["w_out"])


LOG2E = math.log2(math.e)
PEER_BLK = 256
PEER_RANKS = PEER_TOPK + 1


def _sort_network(n):
    pairs = []

    def merge(lo, m, r):
        step = 2 * r
        if step < m:
            merge(lo, m, step)
            merge(lo + r, m, step)
            pairs.extend((i, i + r) for i in range(lo + r, lo + m - r, step))
        else:
            pairs.append((lo, lo + r))

    def sort(lo, m):
        if m > 1:
            sort(lo, m // 2)
            sort(lo + m // 2, m // 2)
            merge(lo, m, 1)

    sort(0, n)
    return pairs


def _top_of_sorted_lists(lists, k, extra=None):
    lists = list(lists)
    neg = jnp.full(lists[0].shape, -jnp.inf, F32)
    vals = []
    for r in range(k):
        mx = jnp.max(lists[0], axis=0, keepdims=True)
        if extra is not None:
            mx = jnp.maximum(mx, jnp.max(extra, axis=0, keepdims=True))
        vals.append(mx)
        if r == k - 1:
            break
        win = lists[0] == mx
        for v in range(min(len(lists), k - 1 - r)):
            lists[v] = jnp.where(win, lists[v + 1] if v + 1 < len(lists) else neg, lists[v])
        if extra is not None:
            extra = jnp.where(extra == mx, -jnp.inf, extra)
    return vals


def _top_sorted(s, k):
    lists = [s[8 * v:8 * v + 8] for v in range(s.shape[0] // 8)]
    for i, j in _sort_network(len(lists)):
        lists[i], lists[j] = jnp.maximum(lists[i], lists[j]), jnp.minimum(lists[i], lists[j])
    return _top_of_sorted_lists(lists, k)


def _peer_prep_kernel(x_ref, g_ref, wq_ref, keys_ref, hn_ref, a1_ref, th_ref, e2_ref):
    hn = _rms(x_ref[...], g_ref[...]).astype(BF16)
    hn_ref[...] = hn
    q = _dot(hn, wq_ref[...]).astype(BF16)
    tm = q.shape[0]
    for hd in range(PEER_HEADS):
        sts, tops = [], []
        for p in range(2):
            hp = 2 * hd + p
            st = _dot_nt(keys_ref[hp], q[:, hp * PEER_HALF:(hp + 1) * PEER_HALF])
            sts.append(st)
            tops.append(_top_sorted(st, PEER_RANKS))
        a, b = tops
        a_lo = jnp.concatenate(a[:8], axis=0)
        a_hi = jnp.concatenate(a[8:] + [jnp.full((7, tm), -jnp.inf, F32)], axis=0)
        top = _top_of_sorted_lists([a_lo + b[r] for r in range(PEER_RANKS)], PEER_RANKS,
                                   extra=a_hi + b[0])
        m = top[0]
        log_z = jnp.log(sum(jnp.exp(t - m) for t in top[:PEER_TOPK]))
        thr = 0.5 * (top[PEER_TOPK - 1] + top[PEER_TOPK])
        st1 = (sts[0] - a[0]) * LOG2E
        s2k = (sts[1] - b[0] - log_z) * LOG2E
        a1_ref[hd] = jnp.exp2(st1)
        th_ref[hd] = jnp.exp2((thr - m - log_z) * LOG2E - st1 - 1.0)
        e2_ref[hd] = jnp.exp2(s2k - 1.0)


def _peer_prep(x2, lw, tm):
    T = x2.shape[0]
    sc = pl.BlockSpec((PEER_HEADS, PEER_N_KEYS, tm), lambda i: (0, 0, i))
    return pl.pallas_call(
        _peer_prep_kernel,
        grid=(T // tm,),
        in_specs=[pl.BlockSpec((tm, D_MODEL), lambda i: (i, 0)),
                  pl.BlockSpec((1, D_MODEL), lambda i: (0, 0)),
                  pl.BlockSpec((D_MODEL, 2 * PEER_HEADS * PEER_HALF), lambda i: (0, 0)),
                  pl.BlockSpec((2 * PEER_HEADS, PEER_N_KEYS, PEER_HALF), lambda i: (0, 0, 0))],
        out_specs=[pl.BlockSpec((tm, D_MODEL), lambda i: (i, 0)), sc, sc, sc],
        out_shape=[jax.ShapeDtypeStruct((T, D_MODEL), BF16)]
        + [jax.ShapeDtypeStruct((PEER_HEADS, PEER_N_KEYS, T), F32)] * 3,
        compiler_params=_cparams(("parallel",)),
        name="peer_prep",
    )(x2, lw["norm_ffn_g"], lw["peer_w_q"], lw["peer_keys"])


def _peer_main_kernel(x_ref, hn_ref, a1_ref, th_ref, e2_ref, u_ref, vt_ref, fg_ref, o_ref,
                      acc_ref, act_ref, w_ref, *, n_i, final_norm):
    e = pl.program_id(1)
    tm = hn_ref.shape[0]

    @pl.when(e == 0)
    def _():
        acc_ref[...] = jnp.zeros_like(acc_ref)

    n_blk = n_i * PEER_N_KEYS // PEER_BLK
    per_blk = PEER_BLK // PEER_N_KEYS

    def activations(p):
        a = _dot_nt(u_ref[p * PEER_BLK:(p + 1) * PEER_BLK, :], hn_ref[...])
        return a * (1.0 + lax.erf(a * (1.0 / math.sqrt(2.0))))

    act_ref[0] = activations(0)
    for p in range(n_blk):
        slot = p % 2
        if p + 1 < n_blk:
            act_ref[1 - slot] = activations(p + 1)
        if p > 0:
            acc_ref[...] += _dot(vt_ref[p - 1], w_ref[1 - slot])
        for ii in range(per_blk):
            i = p * per_blk + ii
            rows = slice(ii * PEER_N_KEYS, (ii + 1) * PEER_N_KEYS)
            for c in range(tm // LANES):
                ls = slice(c * LANES, (c + 1) * LANES)
                g = None
                for hd in range(PEER_HEADS):
                    e2 = e2_ref[hd, :, ls]
                    t = jnp.where(e2 > th_ref[hd, i:i + 1, ls], e2, 0.0) * a1_ref[hd, i:i + 1, ls]
                    g = t if g is None else g + t
                w_ref[slot, rows, ls] = (g * act_ref[slot, rows, ls]).astype(BF16)
    acc_ref[...] += _dot(vt_ref[n_blk - 1], w_ref[(n_blk - 1) % 2])

    @pl.when(e == pl.num_programs(1) - 1)
    def _():
        y = x_ref[...] + acc_ref[...].T
        if final_norm:
            y = _rms(y, fg_ref[...])
        o_ref[...] = y


def _peer_main(x2, hn, a1, th, e2, lw, final_g, tm, n_i, final_norm):
    T = x2.shape[0]
    et = n_i * PEER_N_KEYS
    sc = pl.BlockSpec((PEER_HEADS, PEER_N_KEYS, tm), lambda t, e: (0, 0, t))
    return pl.pallas_call(
        functools.partial(_peer_main_kernel, n_i=n_i, final_norm=final_norm),
        grid=(T // tm, PEER_N_EXPERTS // et),
        in_specs=[pl.BlockSpec((tm, D_MODEL), lambda t, e: (t, 0)),
                  pl.BlockSpec((tm, D_MODEL), lambda t, e: (t, 0)),
                  pl.BlockSpec((PEER_HEADS, n_i, tm), lambda t, e: (0, e, t)),
                  pl.BlockSpec((PEER_HEADS, n_i, tm), lambda t, e: (0, e, t)),
                  sc,
                  pl.BlockSpec((et, D_MODEL), lambda t, e: (e, 0)),
                  pl.BlockSpec((et // PEER_BLK, D_MODEL, PEER_BLK), lambda t, e: (e, 0, 0)),
                  pl.BlockSpec((1, D_MODEL), lambda t, e: (0, 0))],
        out_specs=pl.BlockSpec((tm, D_MODEL), lambda t, e: (t, 0)),
        out_shape=jax.ShapeDtypeStruct((T, D_MODEL), F32),
        scratch_shapes=[pltpu.VMEM((D_MODEL, tm), F32), pltpu.VMEM((2, PEER_BLK, tm), F32),
                        pltpu.VMEM((2, PEER_BLK, tm), BF16)],
        compiler_params=_cparams(("parallel", "arbitrary")),
        name="peer_main",
    )(x2, hn, a1, th, e2, lw["peer_u"], lw["peer_vt"], final_g)


def _rot_half_cols(w):
    d = w.shape[-1]
    return jnp.concatenate([-w[..., d // 2:], w[..., :d // 2]], axis=-1)


def _pad_cols(w, left, total):
    return jnp.pad(w, ((0, 0), (left, total - left - w.shape[-1])))


def _head_groups(w, n_heads, width, left=0):
    return jnp.concatenate(
        [_pad_cols(w[:, h * width:(h + 1) * width], left, LANES) for h in range(n_heads)], axis=-1)


def _layer_weights(p, l):
    w_in = p["w_in"][l]
    widths = (MLA_Q_LORA, MLA_KV_LORA, MLA_ROPE, F_WIDTH, GQA_HEADS * GQA_HEAD_DIM,
              GQA_KV_HEADS * GQA_HEAD_DIM, GQA_KV_HEADS * GQA_HEAD_DIM, 3 * D_MODEL)
    offs = np.cumsum((0,) + widths)
    wcq, wckv, wkr, wxf, wgq, wgk, wgv, wgate = (w_in[:, offs[i]:offs[i + 1]] for i in range(8))

    def rot_heads(w, n_heads, width):
        return jnp.concatenate(
            [_rot_half_cols(w[:, h * width:(h + 1) * width]) for h in range(n_heads)], axis=-1)

    w1 = jnp.concatenate([
        wcq, wckv,
        _pad_cols(wkr, MLA_NOPE, LANES), _pad_cols(_rot_half_cols(wkr), MLA_NOPE, LANES),
        wxf,
        _head_groups(wgq, GQA_HEADS, GQA_HEAD_DIM),
        _head_groups(rot_heads(wgq, GQA_HEADS, GQA_HEAD_DIM), GQA_HEADS, GQA_HEAD_DIM),
        _head_groups(wgk, GQA_KV_HEADS, GQA_HEAD_DIM),
        _head_groups(rot_heads(wgk, GQA_KV_HEADS, GQA_HEAD_DIM), GQA_KV_HEADS, GQA_HEAD_DIM),
        _head_groups(wgv, GQA_KV_HEADS, GQA_HEAD_DIM),
        wgate], axis=-1).astype(BF16)
    assert w1.shape[-1] == _W1_COLS

    w_uq = p["mla_w_uq"][l]
    qd = MLA_NOPE + MLA_ROPE
    uq, uqr = [], []
    for h in range(MLA_HEADS):
        blk = w_uq[:, h * qd:(h + 1) * qd]
        uq.append(_pad_cols(blk, 0, LANES))
        uqr.append(_pad_cols(_rot_half_cols(blk[:, MLA_NOPE:]), MLA_NOPE, LANES))
    w_ukv = p["mla_w_ukv"][l]
    kd = MLA_NOPE + MLA_V
    uk = [_pad_cols(w_ukv[:, h * kd:h * kd + MLA_NOPE], 0, LANES) for h in range(MLA_HEADS)]
    uv = [_pad_cols(w_ukv[:, h * kd + MLA_NOPE:(h + 1) * kd], 0, LANES) for h in range(MLA_HEADS)]

    def pad_rows(w, n_heads, width):
        return jnp.concatenate(
            [jnp.pad(w[h * width:(h + 1) * width], ((0, LANES - width), (0, 0)))
             for h in range(n_heads)], axis=0)

    return {
        "norm_mix_g": p["norm_mix_g"][l][None, :],
        "w1": w1,
        "q_norm_g": p["mla_q_norm_g"][l][None, :],
        "w_uq": jnp.concatenate(uq, axis=-1).astype(BF16),
        "w_uq_rot": jnp.concatenate(uqr, axis=-1).astype(BF16),
        "kv_norm_g": p["mla_kv_norm_g"][l][None, :],
        "w_ukv": jnp.concatenate(uk + uv, axis=-1).astype(BF16),
        "mla_w_o": pad_rows(p["mla_w_o"][l], MLA_HEADS, MLA_V).astype(BF16),
        "fnet_w": p["fnet_w"][l].astype(BF16),
        "fnet_b": p["fnet_b"][l][None, :],
        "gqa_sink": p["gqa_sink"][l],
        "gqa_w_o": pad_rows(p["gqa_w_o"][l], GQA_HEADS, GQA_HEAD_DIM).astype(BF16),
        "w_out": p["w_out"][l].astype(BF16),
        "norm_ffn_g": p["norm_ffn_g"][l][None, :],
        "peer_w_q": p["peer_w_q"][l].astype(BF16),
        "peer_keys": p["peer_keys"][l].reshape(2 * PEER_HEADS, PEER_N_KEYS, PEER_HALF).astype(BF16),
        "peer_u": p["peer_u"][l].astype(BF16),
        "peer_vt": p["peer_v"][l].astype(BF16).reshape(
            PEER_N_EXPERTS // PEER_BLK, PEER_BLK, D_MODEL).transpose(0, 2, 1),
    }


def _rope_tables(S):
    pos = jnp.arange(S, dtype=F32)[:, None]

    def cs(d):
        inv = 1.0 / (ROPE_THETA ** (jnp.arange(0, d, 2, dtype=F32) / d))
        ang = pos * inv[None, :]
        return (jnp.concatenate([jnp.cos(ang)] * 2, axis=-1),
                jnp.concatenate([jnp.sin(ang)] * 2, axis=-1))

    c, s = cs(MLA_ROPE)
    rest = LANES - MLA_NOPE - MLA_ROPE
    cm = jnp.concatenate([jnp.ones((S, MLA_NOPE), F32), c, jnp.zeros((S, rest), F32)], axis=-1)
    sm = jnp.concatenate([jnp.zeros((S, MLA_NOPE), F32), s, jnp.zeros((S, rest), F32)], axis=-1)
    c, s = cs(GQA_HEAD_DIM)
    zero = jnp.zeros((S, LANES - GQA_HEAD_DIM), F32)
    return {"cm": cm, "sm": sm, "cg": jnp.concatenate([c, zero], axis=-1),
            "sg": jnp.concatenate([s, zero], axis=-1)}


def _fourier_tables(S):
    n2 = 128
    n1 = S // n2

    def cos_sin(num, den):
        ang = (2.0 * math.pi / den) * (num % den).astype(F32)
        return jnp.cos(ang), jnp.sin(ang)

    i1 = jnp.arange(n1, dtype=jnp.int32)
    c1, s1 = cos_sin(i1[:, None] * i1[None, :], n1)
    w1 = jnp.concatenate([c1, -s1], axis=0)
    i2 = jnp.arange(n2, dtype=jnp.int32)
    kk = i1[:, None, None] + n1 * i2[None, :, None]
    mr, ms = cos_sin(kk * i2[None, None, :], S)
    m2 = jnp.concatenate([jnp.concatenate([mr, ms], axis=-1),
                          jnp.concatenate([-ms, mr], axis=-1)], axis=1)
    ic = jnp.arange(F_GROUP_DIM, dtype=jnp.int32)
    cc, cs = cos_sin(ic[:, None] * ic[None, :], F_GROUP_DIM)
    eye = jnp.eye(F_GROUPS, dtype=F32)
    return {"n1": n1, "n2": n2, "w1": w1, "m2": m2, "cc": jnp.kron(eye, cc),
            "cs": jnp.kron(eye, cs)}


def _pick(S, pref):
    t = pref
    while S % t:
        t //= 2
    return t


def _trunk(x, layers, final_g, rope_t, four_t):
    B, S, _ = x.shape
    T = B * S
    x2 = x.reshape(T, D_MODEL)
    tm = _pick(S, 256)
    n_layers = len(layers)
    for l, lw in enumerate(layers):
        q, k, v, xf, qg, kg, vg, gate = _in_proj(x2, lw, rope_t, S, tm)
        attn = _mla_attn(q, k, v, B, S, _pick(S, 1024), _pick(S // 2, 512))
        four = _fourier(xf, four_t, B, S)
        og = _gqa(qg, kg, vg, lw["gqa_sink"], B, S, _pick(S, 512))
        x2 = _merge(x2, attn, four, og, gate, lw, tm)
        hn, a1, th, e2 = _peer_prep(x2, lw, _pick(T, 256))
        x2 = _peer_main(x2, hn, a1, th, e2, lw, final_g, _pick(T, 512), 16,
                        final_norm=(l == n_layers - 1))
    return x2.reshape(B, S, D_MODEL)


def kernel(x_prompt, x_sample, norm_mix_g, w_in, mla_q_norm_g, mla_w_uq, mla_kv_norm_g, mla_w_ukv, mla_w_o, fnet_w, fnet_b, gqa_sink, gqa_w_o, w_out, norm_ffn_g, peer_w_q, peer_keys, peer_u, peer_v, final_norm_g):
    p = dict(norm_mix_g=norm_mix_g, w_in=w_in, mla_q_norm_g=mla_q_norm_g, mla_w_uq=mla_w_uq,
             mla_kv_norm_g=mla_kv_norm_g, mla_w_ukv=mla_w_ukv, mla_w_o=mla_w_o, fnet_w=fnet_w,
             fnet_b=fnet_b, gqa_sink=gqa_sink, gqa_w_o=gqa_w_o, w_out=w_out,
             norm_ffn_g=norm_ffn_g, peer_w_q=peer_w_q, peer_keys=peer_keys, peer_u=peer_u,
             peer_v=peer_v)
    layers = [_layer_weights(p, l) for l in range(w_in.shape[0])]
    final_g = final_norm_g[None, :]
    outs = []
    for x in (x_prompt, x_sample):
        S = x.shape[1]
        outs.append(_trunk(x, layers, final_g, _rope_tables(S), _fourier_tables(S)))
    return tuple(outs)
```

```python
import functools
import math

import jax
import jax.numpy as jnp
import numpy as np
from jax import lax
from jax.experimental import pallas as pl
from jax.experimental.pallas import tpu as pltpu

D_MODEL = 1024
EPS = 1e-6
ROPE_THETA = 10000.0
NEG = -1e30
LANES = 128

MLA_HEADS = 8
MLA_NOPE = 64
MLA_ROPE = 32
MLA_V = 64
MLA_Q_LORA = 384
MLA_KV_LORA = 256
F_GROUPS = 4
F_GROUP_DIM = 64
F_WIDTH = F_GROUPS * F_GROUP_DIM
GQA_HEADS = 4
GQA_KV_HEADS = 2
GQA_GROUP = GQA_HEADS // GQA_KV_HEADS
GQA_HEAD_DIM = 64
WINDOW = 128
PEER_HEADS = 8
PEER_N_KEYS = 128
PEER_N_EXPERTS = PEER_N_KEYS * PEER_N_KEYS
PEER_HALF = 128
PEER_TOPK = 16

VMEM_LIMIT = 56 * 1024 * 1024

BF16 = jnp.bfloat16
F32 = jnp.float32
HIGHEST = lax.Precision.HIGHEST


def _cparams(sem):
    return pltpu.CompilerParams(dimension_semantics=sem, vmem_limit_bytes=VMEM_LIMIT)


def _rms(x, g):
    return x * lax.rsqrt(jnp.mean(x * x, axis=-1, keepdims=True) + EPS) * g


def _dot(a, b):
    return jnp.dot(a, b, preferred_element_type=F32)


def _dot_nt(a, b):
    return lax.dot_general(a, b, (((1,), (1,)), ((), ())), preferred_element_type=F32)


_C_CQ = (0, 384)
_C_CKV = (384, 640)
_C_KR = (640, 768)
_C_KRR = (768, 896)
_C_XF = (896, 1152)
_C_GQ = (1152, 1664)
_C_GQR = (1664, 2176)
_C_GK = (2176, 2432)
_C_GKR = (2432, 2688)
_C_GV = (2688, 2944)
_C_GATE = (2944, 6016)
_W1_COLS = 6016


def _in_proj_kernel(x_ref, g_ref, w1_ref, qng_ref, wuq_ref, wuqr_ref, kvng_ref, wukv_ref,
                    cm_ref, sm_ref, cg_ref, sg_ref,
                    q_ref, k_ref, v_ref, xf_ref, qg_ref, kg_ref, vg_ref, gate_ref):
    h = _rms(x_ref[...], g_ref[...]).astype(BF16)

    def proj(c):
        return _dot(h, w1_ref[:, c[0]:c[1]])

    cm, sm = cm_ref[...], sm_ref[...]
    cg, sg = cg_ref[...], sg_ref[...]

    cqn = _rms(proj(_C_CQ), qng_ref[...]).astype(BF16)
    q = _dot(cqn, wuq_ref[...])
    qr = _dot(cqn, wuqr_ref[...])
    mla_scale = (MLA_NOPE + MLA_ROPE) ** -0.5 * math.log2(math.e)
    for hd in range(MLA_HEADS):
        sl = slice(hd * LANES, (hd + 1) * LANES)
        q_ref[:, sl] = ((q[:, sl] * cm + qr[:, sl] * sm) * mla_scale).astype(BF16)

    ckvn = _rms(proj(_C_CKV), kvng_ref[...]).astype(BF16)
    kv = _dot(ckvn, wukv_ref[...])
    k_rope = proj(_C_KR) * cm + proj(_C_KRR) * sm
    ones_col = (lax.broadcasted_iota(jnp.int32, (1, LANES), 1) == MLA_V).astype(F32)
    for hd in range(MLA_HEADS):
        sl = slice(hd * LANES, (hd + 1) * LANES)
        k_ref[:, sl] = (kv[:, sl] + k_rope).astype(BF16)
        vs = slice(MLA_HEADS * LANES + hd * LANES, MLA_HEADS * LANES + (hd + 1) * LANES)
        v_ref[:, sl] = (kv[:, vs] + ones_col).astype(BF16)

    xf_ref[...] = proj(_C_XF)

    gq, gqr = proj(_C_GQ), proj(_C_GQR)
    gqa_scale = GQA_HEAD_DIM ** -0.5
    for hd in range(GQA_HEADS):
        sl = slice(hd * LANES, (hd + 1) * LANES)
        qg_ref[:, sl] = ((gq[:, sl] * cg + gqr[:, sl] * sg) * gqa_scale).astype(BF16)
    gk, gkr = proj(_C_GK), proj(_C_GKR)
    for hd in range(GQA_KV_HEADS):
        sl = slice(hd * LANES, (hd + 1) * LANES)
        kg_ref[:, sl] = (gk[:, sl] * cg + gkr[:, sl] * sg).astype(BF16)
    vg_ref[...] = proj(_C_GV).astype(BF16)

    gate = proj(_C_GATE)
    gate_ref[...] = (1.0 / (1.0 + jnp.exp(-gate))).astype(gate_ref.dtype)


def _in_proj(x2, lw, tabs, S, tm):
    T = x2.shape[0]
    nt = T // tm
    ns = S // tm
    full = lambda shape: pl.BlockSpec(shape, lambda i: (0,) * len(shape))
    row = lambda w: pl.BlockSpec((tm, w), lambda i: (i, 0))
    tab = pl.BlockSpec((tm, LANES), lambda i: (i % ns, 0))
    outs = [(1024, BF16), (1024, BF16), (1024, BF16), (F_WIDTH, F32), (512, BF16), (256, BF16),
            (256, BF16), (3 * D_MODEL, BF16)]
    return pl.pallas_call(
        _in_proj_kernel,
        grid=(nt,),
        in_specs=[row(D_MODEL), full((1, D_MODEL)), full((D_MODEL, _W1_COLS)),
                  full((1, MLA_Q_LORA)), full((MLA_Q_LORA, 1024)), full((MLA_Q_LORA, 1024)),
                  full((1, MLA_KV_LORA)), full((MLA_KV_LORA, 2048)),
                  tab, tab, tab, tab],
        out_specs=[row(w) for w, _ in outs],
        out_shape=[jax.ShapeDtypeStruct((T, w), dt) for w, dt in outs],
        compiler_params=_cparams(("parallel",)),
        name="in_proj",
    )(x2, lw["norm_mix_g"], lw["w1"], lw["q_norm_g"], lw["w_uq"], lw["w_uq_rot"],
      lw["kv_norm_g"], lw["w_ukv"], tabs["cm"], tabs["sm"], tabs["cg"], tabs["sg"])


def _mla_attn_kernel(q_ref, k_ref, v_ref, o_ref, s_ref, p_ref, *, tk):
    tq = q_ref.shape[0]
    nk = k_ref.shape[0] // tk
    q = q_ref[...]

    def chunk(ref, c):
        return ref[pl.ds(pl.multiple_of(c * tk, tk), tk), :]

    def step(slot, v_prev, k_next, carry):
        m, alpha, acc = carry
        pv = _dot(p_ref[1 - slot], v_prev)
        s_ref[1 - slot] = _dot_nt(q, k_next)
        s = s_ref[slot]
        m_new = jnp.maximum(m, jnp.max(s, axis=-1, keepdims=True))
        p_ref[slot] = jnp.exp2(s - m_new).astype(BF16)
        return m_new, jnp.exp2(m - m_new), acc * alpha + pv

    def static_chunk(ref, c):
        return ref[c * tk:(c + 1) * tk, :]

    s_ref[0] = _dot_nt(q, static_chunk(k_ref, 0))
    p_ref[1] = jnp.zeros(p_ref.shape[1:], BF16)
    carry = (jnp.full((tq, 1), NEG, F32), jnp.ones((tq, 1), F32), jnp.zeros((tq, LANES), F32))
    for c in range(nk):
        carry = step(c % 2, static_chunk(v_ref, max(c - 1, 0)),
                     static_chunk(k_ref, min(c + 1, nk - 1)), carry)
    _, alpha, acc = carry
    acc = acc * alpha + _dot(p_ref[1], static_chunk(v_ref, nk - 1))
    o_ref[...] = (acc / acc[:, MLA_V:MLA_V + 1]).astype(o_ref.dtype)


def _mla_attn(q, k, v, B, S, tq, tk):
    assert (S // tk) % 2 == 0
    q3, k3, v3 = (a.reshape(B, S, MLA_HEADS * LANES) for a in (q, k, v))
    out = pl.pallas_call(
        functools.partial(_mla_attn_kernel, tk=tk),
        scratch_shapes=[pltpu.VMEM((2, tq, tk), F32), pltpu.VMEM((2, tq, tk), BF16)],
        grid=(B, MLA_HEADS, S // tq),
        in_specs=[pl.BlockSpec((None, tq, LANES), lambda b, h, i: (b, i, h)),
                  pl.BlockSpec((None, S, LANES), lambda b, h, i: (b, 0, h)),
                  pl.BlockSpec((None, S, LANES), lambda b, h, i: (b, 0, h))],
        out_specs=pl.BlockSpec((None, tq, LANES), lambda b, h, i: (b, i, h)),
        out_shape=jax.ShapeDtypeStruct((B, S, MLA_HEADS * LANES), BF16),
        compiler_params=_cparams(("parallel", "parallel", "parallel")),
        name="mla_attn",
    )(q3, k3, v3)
    return out.reshape(B * S, MLA_HEADS * LANES)


def _fft1_kernel(w_ref, x_ref, o_ref):
    o_ref[...] = jnp.dot(w_ref[...], x_ref[...], preferred_element_type=F32, precision=HIGHEST)


def _fft2_kernel(m_ref, a_ref, cc_ref, cs_ref, o_ref, *, scale):
    aa = jnp.concatenate([a_ref[0], a_ref[1]], axis=0)
    y = jnp.dot(m_ref[...], aa, preferred_element_type=F32, precision=HIGHEST)
    n2 = y.shape[0] // 2
    out = (jnp.dot(y[:n2], cc_ref[...], preferred_element_type=F32, precision=HIGHEST)
           + jnp.dot(y[n2:], cs_ref[...], preferred_element_type=F32, precision=HIGHEST))
    o_ref[...] = out * scale


def _fourier(xf, ft, B, S):
    n1, n2 = ft["n1"], ft["n2"]
    cols = n2 * F_WIDTH
    tc = min(cols, 4096)
    x3 = xf.reshape(B, n1, cols)
    a = pl.pallas_call(
        _fft1_kernel,
        grid=(B, cols // tc),
        in_specs=[pl.BlockSpec((2 * n1, n1), lambda b, j: (0, 0)),
                  pl.BlockSpec((None, n1, tc), lambda b, j: (b, 0, j))],
        out_specs=pl.BlockSpec((None, 2 * n1, tc), lambda b, j: (b, 0, j)),
        out_shape=jax.ShapeDtypeStruct((B, 2 * n1, cols), F32),
        compiler_params=_cparams(("parallel", "parallel")),
        name="fourier_stage1",
    )(ft["w1"], x3)
    a5 = a.reshape(B, 2, n1, n2, F_WIDTH)
    y = pl.pallas_call(
        functools.partial(_fft2_kernel, scale=1.0 / math.sqrt(S * F_GROUP_DIM)),
        grid=(B, n1),
        in_specs=[pl.BlockSpec((None, 2 * n2, 2 * n2), lambda b, k: (k, 0, 0)),
                  pl.BlockSpec((None, 2, None, n2, F_WIDTH), lambda b, k: (b, 0, k, 0, 0)),
                  pl.BlockSpec((F_WIDTH, F_WIDTH), lambda b, k: (0, 0)),
                  pl.BlockSpec((F_WIDTH, F_WIDTH), lambda b, k: (0, 0))],
        out_specs=pl.BlockSpec((None, None, n2, F_WIDTH), lambda b, k: (b, k, 0, 0)),
        out_shape=jax.ShapeDtypeStruct((B, n1, n2, F_WIDTH), F32),
        compiler_params=_cparams(("parallel", "parallel")),
        name="fourier_stage2",
    )(ft["m2"], a5, ft["cc"], ft["cs"])
    return jnp.transpose(y, (0, 2, 1, 3)).reshape(B * S, F_WIDTH)


def _gqa_kernel(sink_ref, q_ref, kp_ref, kc_ref, kn_ref, vp_ref, vc_ref, vn_ref, o_ref, *, S):
    n = pl.program_id(1)
    tq, halo = q_ref.shape[0], kp_ref.shape[0]
    width = tq + 2 * halo
    qpos = n * tq + lax.broadcasted_iota(jnp.int32, (tq, width), 0)
    kpos = n * tq - halo + lax.broadcasted_iota(jnp.int32, (tq, width), 1)
    mask = (jnp.abs(qpos - kpos) <= WINDOW) & (kpos >= 0) & (kpos < S)
    for hd in range(GQA_HEADS):
        kh = hd // GQA_GROUP
        qs = slice(hd * LANES, (hd + 1) * LANES)
        ks = slice(kh * LANES, (kh + 1) * LANES)
        q = q_ref[:, qs]
        s = jnp.concatenate([_dot_nt(q, kp_ref[:, ks]), _dot_nt(q, kc_ref[:, ks]),
                             _dot_nt(q, kn_ref[:, ks])], axis=-1)
        s = jnp.where(mask, s, NEG)
        sk = sink_ref[hd]
        m = jnp.maximum(jnp.max(s, axis=-1, keepdims=True), sk)
        p = jnp.exp(s - m)
        den = jnp.sum(p, axis=-1, keepdims=True) + jnp.exp(sk - m)
        pb = (p / den).astype(BF16)
        o = (_dot(pb[:, :halo], vp_ref[:, ks]) + _dot(pb[:, halo:halo + tq], vc_ref[:, ks])
             + _dot(pb[:, halo + tq:], vn_ref[:, ks]))
        o_ref[:, qs] = o.astype(o_ref.dtype)


def _gqa(qg, kg, vg, sink, B, S, tq):
    halo = WINDOW
    per = tq // halo
    nb = S // halo
    q3 = qg.reshape(B, S, GQA_HEADS * LANES)
    k3 = kg.reshape(B, S, GQA_KV_HEADS * LANES)
    v3 = vg.reshape(B, S, GQA_KV_HEADS * LANES)
    kvw = GQA_KV_HEADS * LANES
    prev = pl.BlockSpec((None, halo, kvw), lambda b, n: (b, jnp.maximum(n * per - 1, 0), 0))
    cur = pl.BlockSpec((None, tq, kvw), lambda b, n: (b, n, 0))
    nxt = pl.BlockSpec((None, halo, kvw), lambda b, n: (b, jnp.minimum((n + 1) * per, nb - 1), 0))
    out = pl.pallas_call(
        functools.partial(_gqa_kernel, S=S),
        grid=(B, S // tq),
        in_specs=[pl.BlockSpec(memory_space=pltpu.SMEM),
                  pl.BlockSpec((None, tq, GQA_HEADS * LANES), lambda b, n: (b, n, 0)),
                  prev, cur, nxt, prev, cur, nxt],
        out_specs=pl.BlockSpec((None, tq, GQA_HEADS * LANES), lambda b, n: (b, n, 0)),
        out_shape=jax.ShapeDtypeStruct((B, S, GQA_HEADS * LANES), BF16),
        compiler_params=_cparams(("parallel", "parallel")),
        name="gqa_window",
    )(sink, q3, k3, k3, k3, v3, v3, v3)
    return out.reshape(B * S, GQA_HEADS * LANES)


def _merge_kernel(x_ref, attn_ref, four_ref, og_ref, gate_ref, wo_ref, fw_ref, fb_ref, gwo_ref,
                  wout_ref, o_ref):
    mla_o = _dot(attn_ref[...], wo_ref[...])
    fnet_o = _dot(four_ref[...].astype(BF16), fw_ref[...]) + fb_ref[...]
    gqa_o = _dot(og_ref[...], gwo_ref[...])
    d = D_MODEL
    merged = (gate_ref[:, 0:d] * mla_o + gate_ref[:, d:2 * d] * fnet_o
              + gate_ref[:, 2 * d:3 * d] * gqa_o)
    o_ref[...] = x_ref[...] + _dot(merged.astype(BF16), wout_ref[...])


def _merge(x2, attn, four, og, gate, lw, tm):
    T = x2.shape[0]
    full = lambda shape: pl.BlockSpec(shape, lambda i: (0,) * len(shape))
    row = lambda w: pl.BlockSpec((tm, w), lambda i: (i, 0))
    return pl.pallas_call(
        _merge_kernel,
        grid=(T // tm,),
        in_specs=[row(D_MODEL), row(1024), row(F_WIDTH), row(512), row(3 * D_MODEL),
                  full((1024, D_MODEL)), full((F_WIDTH, D_MODEL)), full((1, D_MODEL)),
                  full((512, D_MODEL)), full((D_MODEL, D_MODEL))],
        out_specs=row(D_MODEL),
        out_shape=jax.ShapeDtypeStruct((T, D_MODEL), F32),
        compiler_params=_cparams(("parallel",)),
        name="merge",
    )(x2, attn, four, og, gate, lw["mla_w_o"], lw["fnet_w"], lw["fnet_b"], lw["gqa_w_o"],
      lw["w_out"])


LOG2E = math.log2(math.e)
PEER_BLK = 256
PEER_RANKS = PEER_TOPK + 1


def _sort_network(n):
    pairs = []

    def merge(lo, m, r):
        step = 2 * r
        if step < m:
            merge(lo, m, step)
            merge(lo + r, m, step)
            pairs.extend((i, i + r) for i in range(lo + r, lo + m - r, step))
        else:
            pairs.append((lo, lo + r))

    def sort(lo, m):
        if m > 1:
            sort(lo, m // 2)
            sort(lo + m // 2, m // 2)
            merge(lo, m, 1)

    sort(0, n)
    return pairs


def _top_of_sorted_lists(lists, k, extra=None):
    lists = list(lists)
    neg = jnp.full(lists[0].shape, -jnp.inf, F32)
    vals = []
    for r in range(k):
        mx = jnp.max(lists[0], axis=0, keepdims=True)
        if extra is not None:
            mx = jnp.maximum(mx, jnp.max(extra, axis=0, keepdims=True))
        vals.append(mx)
        if r == k - 1:
            break
        win = lists[0] == mx
        for v in range(min(len(lists), k - 1 - r)):
            lists[v] = jnp.where(win, lists[v + 1] if v + 1 < len(lists) else neg, lists[v])
        if extra is not None:
            extra = jnp.where(extra == mx, -jnp.inf, extra)
    return vals


def _top_sorted(s, k):
    lists = [s[8 * v:8 * v + 8] for v in range(s.shape[0] // 8)]
    for i, j in _sort_network(len(lists)):
        lists[i], lists[j] = jnp.maximum(lists[i], lists[j]), jnp.minimum(lists[i], lists[j])
    return _top_of_sorted_lists(lists, k)


def _peer_prep_kernel(x_ref, g_ref, wq_ref, keys_ref, hn_ref, a1_ref, th_ref, e2_ref):
    hn = _rms(x_ref[...], g_ref[...])
    hn_ref[...] = hn.T.astype(BF16)
    q = _dot(hn.astype(BF16), wq_ref[...]).astype(BF16)
    tm = q.shape[0]
    for hd in range(PEER_HEADS):
        sts, tops = [], []
        for p in range(2):
            hp = 2 * hd + p
            st = _dot_nt(keys_ref[hp], q[:, hp * PEER_HALF:(hp + 1) * PEER_HALF])
            sts.append(st)
            tops.append(_top_sorted(st, PEER_RANKS))
        a, b = tops
        a_lo = jnp.concatenate(a[:8], axis=0)
        a_hi = jnp.concatenate(a[8:] + [jnp.full((7, tm), -jnp.inf, F32)], axis=0)
        top = _top_of_sorted_lists([a_lo + b[r] for r in range(PEER_RANKS)], PEER_RANKS,
                                   extra=a_hi + b[0])
        m = top[0]
        log_z = jnp.log(sum(jnp.exp(t - m) for t in top[:PEER_TOPK]))
        thr = 0.5 * (top[PEER_TOPK - 1] + top[PEER_TOPK])
        st1 = (sts[0] - a[0]) * LOG2E
        s2k = (sts[1] - b[0] - log_z) * LOG2E
        a1_ref[hd] = jnp.exp2(st1)
        th_ref[hd] = jnp.exp2((thr - m - log_z) * LOG2E - st1 - 1.0)
        e2_ref[hd] = jnp.exp2(s2k - 1.0)


def _peer_prep(x2, lw, tm):
    T = x2.shape[0]
    sc = pl.BlockSpec((PEER_HEADS, PEER_N_KEYS, tm), lambda i: (0, 0, i))
    return pl.pallas_call(
        _peer_prep_kernel,
        grid=(T // tm,),
        in_specs=[pl.BlockSpec((tm, D_MODEL), lambda i: (i, 0)),
                  pl.BlockSpec((1, D_MODEL), lambda i: (0, 0)),
                  pl.BlockSpec((D_MODEL, 2 * PEER_HEADS * PEER_HALF), lambda i: (0, 0)),
                  pl.BlockSpec((2 * PEER_HEADS, PEER_N_KEYS, PEER_HALF), lambda i: (0, 0, 0))],
        out_specs=[pl.BlockSpec((D_MODEL, tm), lambda i: (0, i)), sc, sc, sc],
        out_shape=[jax.ShapeDtypeStruct((D_MODEL, T), BF16)]
        + [jax.ShapeDtypeStruct((PEER_HEADS, PEER_N_KEYS, T), F32)] * 3,
        compiler_params=_cparams(("parallel",)),
        name="peer_prep",
    )(x2, lw["norm_ffn_g"], lw["peer_w_q"], lw["peer_keys"])


def _peer_main_kernel(x_ref, hn_ref, a1_ref, th_ref, e2_ref, u_ref, vt_ref, fg_ref, o_ref,
                      acc_ref, act_ref, w_ref, *, n_i, final_norm):
    e = pl.program_id(1)
    tm = hn_ref.shape[1]

    @pl.when(e == 0)
    def _():
        acc_ref[...] = jnp.zeros_like(acc_ref)

    n_blk = n_i * PEER_N_KEYS // PEER_BLK
    per_blk = PEER_BLK // PEER_N_KEYS

    def activations(p):
        a = _dot(u_ref[p * PEER_BLK:(p + 1) * PEER_BLK, :], hn_ref[...])
        return a * (1.0 + lax.erf(a * (1.0 / math.sqrt(2.0))))

    act_ref[0] = activations(0)
    for p in range(n_blk):
        slot = p % 2
        if p + 1 < n_blk:
            act_ref[1 - slot] = activations(p + 1)
        if p > 0:
            acc_ref[...] += _dot(vt_ref[p - 1], w_ref[1 - slot])
        for ii in range(per_blk):
            i = p * per_blk + ii
            rows = slice(ii * PEER_N_KEYS, (ii + 1) * PEER_N_KEYS)
            for c in range(tm // LANES):
                ls = slice(c * LANES, (c + 1) * LANES)
                g = None
                for hd in range(PEER_HEADS):
                    e2 = e2_ref[hd, :, ls]
                    t = jnp.where(e2 > th_ref[hd, i:i + 1, ls], e2, 0.0) * a1_ref[hd, i:i + 1, ls]
                    g = t if g is None else g + t
                w_ref[slot, rows, ls] = (g * act_ref[slot, rows, ls]).astype(BF16)
    acc_ref[...] += _dot(vt_ref[n_blk - 1], w_ref[(n_blk - 1) % 2])

    @pl.when(e == pl.num_programs(1) - 1)
    def _():
        y = x_ref[...] + acc_ref[...].T
        if final_norm:
            y = _rms(y, fg_ref[...])
        o_ref[...] = y


def _peer_main(x2, hn, a1, th, e2, lw, final_g, tm, n_i, final_norm):
    T = x2.shape[0]
    et = n_i * PEER_N_KEYS
    sc = pl.BlockSpec((PEER_HEADS, PEER_N_KEYS, tm), lambda t, e: (0, 0, t))
    return pl.pallas_call(
        functools.partial(_peer_main_kernel, n_i=n_i, final_norm=final_norm),
        grid=(T // tm, PEER_N_EXPERTS // et),
        in_specs=[pl.BlockSpec((tm, D_MODEL), lambda t, e: (t, 0)),
                  pl.BlockSpec((D_MODEL, tm), lambda t, e: (0, t)),
                  pl.BlockSpec((PEER_HEADS, n_i, tm), lambda t, e: (0, e, t)),
                  pl.BlockSpec((PEER_HEADS, n_i, tm), lambda t, e: (0, e, t)),
                  sc,
                  pl.BlockSpec((et, D_MODEL), lambda t, e: (e, 0)),
                  pl.BlockSpec((et // PEER_BLK, D_MODEL, PEER_BLK), lambda t, e: (e, 0, 0)),
                  pl.BlockSpec((1, D_MODEL), lambda t, e: (0, 0))],
        out_specs=pl.BlockSpec((tm, D_MODEL), lambda t, e: (t, 0)),
        out_shape=jax.ShapeDtypeStruct((T, D_MODEL), F32),
        scratch_shapes=[pltpu.VMEM((D_MODEL, tm), F32), pltpu.VMEM((2, PEER_BLK, tm), F32),
                        pltpu.VMEM((2, PEER_BLK, tm), BF16)],
        compiler_params=_cparams(("parallel", "arbitrary")),
        name="peer_main",
    )(x2, hn, a1, th, e2, lw["peer_u"], lw["peer_vt"], final_g)


def _rot_half_cols(w):
    d = w.shape[-1]
    return jnp.concatenate([-w[..., d // 2:], w[..., :d // 2]], axis=-1)


def _pad_cols(w, left, total):
    return jnp.pad(w, ((0, 0), (left, total - left - w.shape[-1])))


def _head_groups(w, n_heads, width, left=0):
    return jnp.concatenate(
        [_pad_cols(w[:, h * width:(h + 1) * width], left, LANES) for h in range(n_heads)], axis=-1)


def _layer_weights(p, l):
    w_in = p["w_in"][l]
    widths = (MLA_Q_LORA, MLA_KV_LORA, MLA_ROPE, F_WIDTH, GQA_HEADS * GQA_HEAD_DIM,
              GQA_KV_HEADS * GQA_HEAD_DIM, GQA_KV_HEADS * GQA_HEAD_DIM, 3 * D_MODEL)
    offs = np.cumsum((0,) + widths)
    wcq, wckv, wkr, wxf, wgq, wgk, wgv, wgate = (w_in[:, offs[i]:offs[i + 1]] for i in range(8))

    def rot_heads(w, n_heads, width):
        return jnp.concatenate(
            [_rot_half_cols(w[:, h * width:(h + 1) * width]) for h in range(n_heads)], axis=-1)

    w1 = jnp.concatenate([
        wcq, wckv,
        _pad_cols(wkr, MLA_NOPE, LANES), _pad_cols(_rot_half_cols(wkr), MLA_NOPE, LANES),
        wxf,
        _head_groups(wgq, GQA_HEADS, GQA_HEAD_DIM),
        _head_groups(rot_heads(wgq, GQA_HEADS, GQA_HEAD_DIM), GQA_HEADS, GQA_HEAD_DIM),
        _head_groups(wgk, GQA_KV_HEADS, GQA_HEAD_DIM),
        _head_groups(rot_heads(wgk, GQA_KV_HEADS, GQA_HEAD_DIM), GQA_KV_HEADS, GQA_HEAD_DIM),
        _head_groups(wgv, GQA_KV_HEADS, GQA_HEAD_DIM),
        wgate], axis=-1).astype(BF16)
    assert w1.shape[-1] == _W1_COLS

    w_uq = p["mla_w_uq"][l]
    qd = MLA_NOPE + MLA_ROPE
    uq, uqr = [], []
    for h in range(MLA_HEADS):
        blk = w_uq[:, h * qd:(h + 1) * qd]
        uq.append(_pad_cols(blk, 0, LANES))
        uqr.append(_pad_cols(_rot_half_cols(blk[:, MLA_NOPE:]), MLA_NOPE, LANES))
    w_ukv = p["mla_w_ukv"][l]
    kd = MLA_NOPE + MLA_V
    uk = [_pad_cols(w_ukv[:, h * kd:h * kd + MLA_NOPE], 0, LANES) for h in range(MLA_HEADS)]
    uv = [_pad_cols(w_ukv[:, h * kd + MLA_NOPE:(h + 1) * kd], 0, LANES) for h in range(MLA_HEADS)]

    def pad_rows(w, n_heads, width):
        return jnp.concatenate(
            [jnp.pad(w[h * width:(h + 1) * width], ((0, LANES - width), (0, 0)))
             for h in range(n_heads)], axis=0)

    return {
        "norm_mix_g": p["norm_mix_g"][l][None, :],
        "w1": w1,
        "q_norm_g": p["mla_q_norm_g"][l][None, :],
        "w_uq": jnp.concatenate(uq, axis=-1).astype(BF16),
        "w_uq_rot": jnp.concatenate(uqr, axis=-1).astype(BF16),
        "kv_norm_g": p["mla_kv_norm_g"][l][None, :],
        "w_ukv": jnp.concatenate(uk + uv, axis=-1).astype(BF16),
        "mla_w_o": pad_rows(p["mla_w_o"][l], MLA_HEADS, MLA_V).astype(BF16),
        "fnet_w": p["fnet_w"][l].astype(BF16),
        "fnet_b": p["fnet_b"][l][None, :],
        "gqa_sink": p["gqa_sink"][l],
        "gqa_w_o": pad_rows(p["gqa_w_o"][l], GQA_HEADS, GQA_HEAD_DIM).astype(BF16),
        "w_out": p["w_out"][l].astype(BF16),
        "norm_ffn_g": p["norm_ffn_g"][l][None, :],
        "peer_w_q": p["peer_w_q"][l].astype(BF16),
        "peer_keys": p["peer_keys"][l].reshape(2 * PEER_HEADS, PEER_N_KEYS, PEER_HALF).astype(BF16),
        "peer_u": p["peer_u"][l].astype(BF16),
        "peer_vt": p["peer_v"][l].astype(BF16).reshape(
            PEER_N_EXPERTS // PEER_BLK, PEER_BLK, D_MODEL).transpose(0, 2, 1),
    }


def _rope_tables(S):
    pos = jnp.arange(S, dtype=F32)[:, None]

    def cs(d):
        inv = 1.0 / (ROPE_THETA ** (jnp.arange(0, d, 2, dtype=F32) / d))
        ang = pos * inv[None, :]
        return (jnp.concatenate([jnp.cos(ang)] * 2, axis=-1),
                jnp.concatenate([jnp.sin(ang)] * 2, axis=-1))

    c, s = cs(MLA_ROPE)
    rest = LANES - MLA_NOPE - MLA_ROPE
    cm = jnp.concatenate([jnp.ones((S, MLA_NOPE), F32), c, jnp.zeros((S, rest), F32)], axis=-1)
    sm = jnp.concatenate([jnp.zeros((S, MLA_NOPE), F32), s, jnp.zeros((S, rest), F32)], axis=-1)
    c, s = cs(GQA_HEAD_DIM)
    zero = jnp.zeros((S, LANES - GQA_HEAD_DIM), F32)
    return {"cm": cm, "sm": sm, "cg": jnp.concatenate([c, zero], axis=-1),
            "sg": jnp.concatenate([s, zero], axis=-1)}


def _fourier_tables(S):
    n2 = 128
    n1 = S // n2

    def cos_sin(num, den):
        ang = (2.0 * math.pi / den) * (num % den).astype(F32)
        return jnp.cos(ang), jnp.sin(ang)

    i1 = jnp.arange(n1, dtype=jnp.int32)
    c1, s1 = cos_sin(i1[:, None] * i1[None, :], n1)
    w1 = jnp.concatenate([c1, -s1], axis=0)
    i2 = jnp.arange(n2, dtype=jnp.int32)
    kk = i1[:, None, None] + n1 * i2[None, :, None]
    mr, ms = cos_sin(kk * i2[None, None, :], S)
    m2 = jnp.concatenate([jnp.concatenate([mr, ms], axis=-1),
                          jnp.concatenate([-ms, mr], axis=-1)], axis=1)
    ic = jnp.arange(F_GROUP_DIM, dtype=jnp.int32)
    cc, cs = cos_sin(ic[:, None] * ic[None, :], F_GROUP_DIM)
    eye = jnp.eye(F_GROUPS, dtype=F32)
    return {"n1": n1, "n2": n2, "w1": w1, "m2": m2, "cc": jnp.kron(eye, cc),
            "cs": jnp.kron(eye, cs)}


def _pick(S, pref):
    t = pref
    while S % t:
        t //= 2
    return t


def _trunk(x, layers, final_g, rope_t, four_t):
    B, S, _ = x.shape
    T = B * S
    x2 = x.reshape(T, D_MODEL)
    tm = _pick(S, 256)
    n_layers = len(layers)
    for l, lw in enumerate(layers):
        q, k, v, xf, qg, kg, vg, gate = _in_proj(x2, lw, rope_t, S, tm)
        attn = _mla_attn(q, k, v, B, S, _pick(S, 1024), _pick(S // 2, 512))
        four = _fourier(xf, four_t, B, S)
        og = _gqa(qg, kg, vg, lw["gqa_sink"], B, S, _pick(S, 512))
        x2 = _merge(x2, attn, four, og, gate, lw, tm)
        hn, a1, th, e2 = _peer_prep(x2, lw, _pick(T, 256))
        x2 = _peer_main(x2, hn, a1, th, e2, lw, final_g, _pick(T, 512), 16,
                        final_norm=(l == n_layers - 1))
    return x2.reshape(B, S, D_MODEL)


def kernel(x_prompt, x_sample, norm_mix_g, w_in, mla_q_norm_g, mla_w_uq, mla_kv_norm_g, mla_w_ukv, mla_w_o, fnet_w, fnet_b, gqa_sink, gqa_w_o, w_out, norm_ffn_g, peer_w_q, peer_keys, peer_u, peer_v, final_norm_g):
    p = dict(norm_mix_g=norm_mix_g, w_in=w_in, mla_q_norm_g=mla_q_norm_g, mla_w_uq=mla_w_uq,
             mla_kv_norm_g=mla_kv_norm_g, mla_w_ukv=mla_w_ukv, mla_w_o=mla_w_o, fnet_w=fnet_w,
             fnet_b=fnet_b, gqa_sink=gqa_sink, gqa_w_o=gqa_w_o, w_out=w_out,
             norm_ffn_g=norm_ffn_g, peer_w_q=peer_w_q, peer_keys=peer_keys, peer_u=peer_u,
             peer_v=peer_v)
    layers = [_layer_weights(p, l) for l in range(w_in.shape[0])]
    final_g = final_norm_g[None, :]
    outs = []
    for x in (x_prompt, x_sample):
        S = x.shape[1]
        outs.append(_trunk(x, layers, final_g, _rope_tables(S), _fourier_tables(S)))
    return tuple(outs)
```

```python
import functools
import math

import jax
import jax.numpy as jnp
import numpy as np
from jax import lax
from jax.experimental import pallas as pl
from jax.experimental.pallas import tpu as pltpu

D_MODEL = 1024
EPS = 1e-6
ROPE_THETA = 10000.0
NEG = -1e30
LANES = 128

MLA_HEADS = 8
MLA_NOPE = 64
MLA_ROPE = 32
MLA_V = 64
MLA_Q_LORA = 384
MLA_KV_LORA = 256
F_GROUPS = 4
F_GROUP_DIM = 64
F_WIDTH = F_GROUPS * F_GROUP_DIM
GQA_HEADS = 4
GQA_KV_HEADS = 2
GQA_GROUP = GQA_HEADS // GQA_KV_HEADS
GQA_HEAD_DIM = 64
WINDOW = 128
PEER_HEADS = 8
PEER_N_KEYS = 128
PEER_N_EXPERTS = PEER_N_KEYS * PEER_N_KEYS
PEER_HALF = 128
PEER_TOPK = 16

VMEM_LIMIT = 56 * 1024 * 1024

BF16 = jnp.bfloat16
F32 = jnp.float32
HIGHEST = lax.Precision.HIGHEST


def _cparams(sem):
    return pltpu.CompilerParams(dimension_semantics=sem, vmem_limit_bytes=VMEM_LIMIT)


def _rms(x, g):
    return x * lax.rsqrt(jnp.mean(x * x, axis=-1, keepdims=True) + EPS) * g


def _dot(a, b):
    return jnp.dot(a, b, preferred_element_type=F32)


def _dot_nt(a, b):
    return lax.dot_general(a, b, (((1,), (1,)), ((), ())), preferred_element_type=F32)


_C_CQ = (0, 384)
_C_CKV = (384, 640)
_C_KR = (640, 768)
_C_KRR = (768, 896)
_C_XF = (896, 1152)
_C_GQ = (1152, 1664)
_C_GQR = (1664, 2176)
_C_GK = (2176, 2432)
_C_GKR = (2432, 2688)
_C_GV = (2688, 2944)
_C_GATE = (2944, 6016)
_W1_COLS = 6016


def _in_proj_kernel(x_ref, g_ref, w1_ref, qng_ref, wuq_ref, wuqr_ref, kvng_ref, wukv_ref,
                    cm_ref, sm_ref, cg_ref, sg_ref,
                    q_ref, k_ref, v_ref, xf_ref, qg_ref, kg_ref, vg_ref, gate_ref):
    h = _rms(x_ref[...], g_ref[...]).astype(BF16)

    def proj(c):
        return _dot(h, w1_ref[:, c[0]:c[1]])

    cm, sm = cm_ref[...], sm_ref[...]
    cg, sg = cg_ref[...], sg_ref[...]

    cqn = _rms(proj(_C_CQ), qng_ref[...]).astype(BF16)
    q = _dot(cqn, wuq_ref[...])
    qr = _dot(cqn, wuqr_ref[...])
    mla_scale = (MLA_NOPE + MLA_ROPE) ** -0.5 * math.log2(math.e)
    for hd in range(MLA_HEADS):
        sl = slice(hd * LANES, (hd + 1) * LANES)
        q_ref[:, sl] = ((q[:, sl] * cm + qr[:, sl] * sm) * mla_scale).astype(BF16)

    ckvn = _rms(proj(_C_CKV), kvng_ref[...]).astype(BF16)
    kv = _dot(ckvn, wukv_ref[...])
    k_rope = proj(_C_KR) * cm + proj(_C_KRR) * sm
    ones_col = (lax.broadcasted_iota(jnp.int32, (1, LANES), 1) == MLA_V).astype(F32)
    for hd in range(MLA_HEADS):
        sl = slice(hd * LANES, (hd + 1) * LANES)
        k_ref[:, sl] = (kv[:, sl] + k_rope).astype(BF16)
        vs = slice(MLA_HEADS * LANES + hd * LANES, MLA_HEADS * LANES + (hd + 1) * LANES)
        v_ref[:, sl] = (kv[:, vs] + ones_col).astype(BF16)

    xf_ref[...] = proj(_C_XF)

    gq, gqr = proj(_C_GQ), proj(_C_GQR)
    gqa_scale = GQA_HEAD_DIM ** -0.5
    for hd in range(GQA_HEADS):
        sl = slice(hd * LANES, (hd + 1) * LANES)
        qg_ref[:, sl] = ((gq[:, sl] * cg + gqr[:, sl] * sg) * gqa_scale).astype(BF16)
    gk, gkr = proj(_C_GK), proj(_C_GKR)
    for hd in range(GQA_KV_HEADS):
        sl = slice(hd * LANES, (hd + 1) * LANES)
        kg_ref[:, sl] = (gk[:, sl] * cg + gkr[:, sl] * sg).astype(BF16)
    vg_ref[...] = proj(_C_GV).astype(BF16)

    gate = proj(_C_GATE)
    gate_ref[...] = (1.0 / (1.0 + jnp.exp(-gate))).astype(gate_ref.dtype)


def _in_proj(x2, lw, tabs, S, tm):
    T = x2.shape[0]
    nt = T // tm
    ns = S // tm
    full = lambda shape: pl.BlockSpec(shape, lambda i: (0,) * len(shape))
    row = lambda w: pl.BlockSpec((tm, w), lambda i: (i, 0))
    tab = pl.BlockSpec((tm, LANES), lambda i: (i % ns, 0))
    outs = [(1024, BF16), (1024, BF16), (1024, BF16), (F_WIDTH, F32), (512, BF16), (256, BF16),
            (256, BF16), (3 * D_MODEL, BF16)]
    return pl.pallas_call(
        _in_proj_kernel,
        grid=(nt,),
        in_specs=[row(D_MODEL), full((1, D_MODEL)), full((D_MODEL, _W1_COLS)),
                  full((1, MLA_Q_LORA)), full((MLA_Q_LORA, 1024)), full((MLA_Q_LORA, 1024)),
                  full((1, MLA_KV_LORA)), full((MLA_KV_LORA, 2048)),
                  tab, tab, tab, tab],
        out_specs=[row(w) for w, _ in outs],
        out_shape=[jax.ShapeDtypeStruct((T, w), dt) for w, dt in outs],
        compiler_params=_cparams(("parallel",)),
        name="in_proj",
    )(x2, lw["norm_mix_g"], lw["w1"], lw["q_norm_g"], lw["w_uq"], lw["w_uq_rot"],
      lw["kv_norm_g"], lw["w_ukv"], tabs["cm"], tabs["sm"], tabs["cg"], tabs["sg"])


def _mla_attn_kernel(q_ref, k_ref, v_ref, o_ref, s_ref, p_ref, *, tk):
    tq = q_ref.shape[0]
    nk = k_ref.shape[0] // tk
    q = q_ref[...]

    def chunk(ref, c):
        return ref[pl.ds(pl.multiple_of(c * tk, tk), tk), :]

    def step(slot, v_prev, k_next, carry):
        m, alpha, acc = carry
        pv = _dot(p_ref[1 - slot], v_prev)
        s_ref[1 - slot] = _dot_nt(q, k_next)
        s = s_ref[slot]
        m_new = jnp.maximum(m, jnp.max(s, axis=-1, keepdims=True))
        p_ref[slot] = jnp.exp2(s - m_new).astype(BF16)
        return m_new, jnp.exp2(m - m_new), acc * alpha + pv

    def static_chunk(ref, c):
        return ref[c * tk:(c + 1) * tk, :]

    s_ref[0] = _dot_nt(q, static_chunk(k_ref, 0))
    p_ref[1] = jnp.zeros(p_ref.shape[1:], BF16)
    carry = (jnp.full((tq, 1), NEG, F32), jnp.ones((tq, 1), F32), jnp.zeros((tq, LANES), F32))
    for c in range(nk):
        carry = step(c % 2, static_chunk(v_ref, max(c - 1, 0)),
                     static_chunk(k_ref, min(c + 1, nk - 1)), carry)
    _, alpha, acc = carry
    acc = acc * alpha + _dot(p_ref[1], static_chunk(v_ref, nk - 1))
    o_ref[...] = (acc / acc[:, MLA_V:MLA_V + 1]).astype(o_ref.dtype)


def _mla_attn(q, k, v, B, S, tq, tk):
    assert (S // tk) % 2 == 0
    q3, k3, v3 = (a.reshape(B, S, MLA_HEADS * LANES) for a in (q, k, v))
    out = pl.pallas_call(
        functools.partial(_mla_attn_kernel, tk=tk),
        scratch_shapes=[pltpu.VMEM((2, tq, tk), F32), pltpu.VMEM((2, tq, tk), BF16)],
        grid=(B, MLA_HEADS, S // tq),
        in_specs=[pl.BlockSpec((None, tq, LANES), lambda b, h, i: (b, i, h)),
                  pl.BlockSpec((None, S, LANES), lambda b, h, i: (b, 0, h)),
                  pl.BlockSpec((None, S, LANES), lambda b, h, i: (b, 0, h))],
        out_specs=pl.BlockSpec((None, tq, LANES), lambda b, h, i: (b, i, h)),
        out_shape=jax.ShapeDtypeStruct((B, S, MLA_HEADS * LANES), BF16),
        compiler_params=_cparams(("parallel", "parallel", "parallel")),
        name="mla_attn",
    )(q3, k3, v3)
    return out.reshape(B * S, MLA_HEADS * LANES)


def _fft1_kernel(w_ref, x_ref, o_ref):
    o_ref[...] = jnp.dot(w_ref[...], x_ref[...], preferred_element_type=F32, precision=HIGHEST)


def _fft2_kernel(m_ref, a_ref, cc_ref, cs_ref, o_ref, *, scale):
    aa = jnp.concatenate([a_ref[0], a_ref[1]], axis=0)
    y = jnp.dot(m_ref[...], aa, preferred_element_type=F32, precision=HIGHEST)
    n2 = y.shape[0] // 2
    out = (jnp.dot(y[:n2], cc_ref[...], preferred_element_type=F32, precision=HIGHEST)
           + jnp.dot(y[n2:], cs_ref[...], preferred_element_type=F32, precision=HIGHEST))
    o_ref[...] = out * scale


def _fourier(xf, ft, B, S):
    n1, n2 = ft["n1"], ft["n2"]
    cols = n2 * F_WIDTH
    tc = min(cols, 4096)
    x3 = xf.reshape(B, n1, cols)
    a = pl.pallas_call(
        _fft1_kernel,
        grid=(B, cols // tc),
        in_specs=[pl.BlockSpec((2 * n1, n1), lambda b, j: (0, 0)),
                  pl.BlockSpec((None, n1, tc), lambda b, j: (b, 0, j))],
        out_specs=pl.BlockSpec((None, 2 * n1, tc), lambda b, j: (b, 0, j)),
        out_shape=jax.ShapeDtypeStruct((B, 2 * n1, cols), F32),
        compiler_params=_cparams(("parallel", "parallel")),
        name="fourier_stage1",
    )(ft["w1"], x3)
    a5 = a.reshape(B, 2, n1, n2, F_WIDTH)
    y = pl.pallas_call(
        functools.partial(_fft2_kernel, scale=1.0 / math.sqrt(S * F_GROUP_DIM)),
        grid=(B, n1),
        in_specs=[pl.BlockSpec((None, 2 * n2, 2 * n2), lambda b, k: (k, 0, 0)),
                  pl.BlockSpec((None, 2, None, n2, F_WIDTH), lambda b, k: (b, 0, k, 0, 0)),
                  pl.BlockSpec((F_WIDTH, F_WIDTH), lambda b, k: (0, 0)),
                  pl.BlockSpec((F_WIDTH, F_WIDTH), lambda b, k: (0, 0))],
        out_specs=pl.BlockSpec((None, None, n2, F_WIDTH), lambda b, k: (b, k, 0, 0)),
        out_shape=jax.ShapeDtypeStruct((B, n1, n2, F_WIDTH), F32),
        compiler_params=_cparams(("parallel", "parallel")),
        name="fourier_stage2",
    )(ft["m2"], a5, ft["cc"], ft["cs"])
    return jnp.transpose(y, (0, 2, 1, 3)).reshape(B * S, F_WIDTH)


def _gqa_kernel(sink_ref, q_ref, kp_ref, kc_ref, kn_ref, vp_ref, vc_ref, vn_ref, o_ref, *, S):
    n = pl.program_id(1)
    tq, halo = q_ref.shape[0], kp_ref.shape[0]
    width = tq + 2 * halo
    qpos = n * tq + lax.broadcasted_iota(jnp.int32, (tq, width), 0)
    kpos = n * tq - halo + lax.broadcasted_iota(jnp.int32, (tq, width), 1)
    mask = (jnp.abs(qpos - kpos) <= WINDOW) & (kpos >= 0) & (kpos < S)
    for hd in range(GQA_HEADS):
        kh = hd // GQA_GROUP
        qs = slice(hd * LANES, (hd + 1) * LANES)
        ks = slice(kh * LANES, (kh + 1) * LANES)
        q = q_ref[:, qs]
        s = jnp.concatenate([_dot_nt(q, kp_ref[:, ks]), _dot_nt(q, kc_ref[:, ks]),
                             _dot_nt(q, kn_ref[:, ks])], axis=-1)
        s = jnp.where(mask, s, NEG)
        sk = sink_ref[hd]
        m = jnp.maximum(jnp.max(s, axis=-1, keepdims=True), sk)
        p = jnp.exp(s - m)
        den = jnp.sum(p, axis=-1, keepdims=True) + jnp.exp(sk - m)
        pb = (p / den).astype(BF16)
        o = (_dot(pb[:, :halo], vp_ref[:, ks]) + _dot(pb[:, halo:halo + tq], vc_ref[:, ks])
             + _dot(pb[:, halo + tq:], vn_ref[:, ks]))
        o_ref[:, qs] = o.astype(o_ref.dtype)


def _gqa(qg, kg, vg, sink, B, S, tq):
    halo = WINDOW
    per = tq // halo
    nb = S // halo
    q3 = qg.reshape(B, S, GQA_HEADS * LANES)
    k3 = kg.reshape(B, S, GQA_KV_HEADS * LANES)
    v3 = vg.reshape(B, S, GQA_KV_HEADS * LANES)
    kvw = GQA_KV_HEADS * LANES
    prev = pl.BlockSpec((None, halo, kvw), lambda b, n: (b, jnp.maximum(n * per - 1, 0), 0))
    cur = pl.BlockSpec((None, tq, kvw), lambda b, n: (b, n, 0))
    nxt = pl.BlockSpec((None, halo, kvw), lambda b, n: (b, jnp.minimum((n + 1) * per, nb - 1), 0))
    out = pl.pallas_call(
        functools.partial(_gqa_kernel, S=S),
        grid=(B, S // tq),
        in_specs=[pl.BlockSpec(memory_space=pltpu.SMEM),
                  pl.BlockSpec((None, tq, GQA_HEADS * LANES), lambda b, n: (b, n, 0)),
                  prev, cur, nxt, prev, cur, nxt],
        out_specs=pl.BlockSpec((None, tq, GQA_HEADS * LANES), lambda b, n: (b, n, 0)),
        out_shape=jax.ShapeDtypeStruct((B, S, GQA_HEADS * LANES), BF16),
        compiler_params=_cparams(("parallel", "parallel")),
        name="gqa_window",
    )(sink, q3, k3, k3, k3, v3, v3, v3)
    return out.reshape(B * S, GQA_HEADS * LANES)


def _merge_kernel(x_ref, attn_ref, four_ref, og_ref, gate_ref, wo_ref, fw_ref, fb_ref, gwo_ref,
                  wout_ref, o_ref):
    mla_o = _dot(attn_ref[...], wo_ref[...])
    fnet_o = _dot(four_ref[...].astype(BF16), fw_ref[...]) + fb_ref[...]
    gqa_o = _dot(og_ref[...], gwo_ref[...])
    d = D_MODEL
    merged = (gate_ref[:, 0:d] * mla_o + gate_ref[:, d:2 * d] * fnet_o
              + gate_ref[:, 2 * d:3 * d] * gqa_o)
    o_ref[...] = x_ref[...] + _dot(merged.astype(BF16), wout_ref[...])


def _merge(x2, attn, four, og, gate, lw, tm):
    T = x2.shape[0]
    full = lambda shape: pl.BlockSpec(shape, lambda i: (0,) * len(shape))
    row = lambda w: pl.BlockSpec((tm, w), lambda i: (i, 0))
    return pl.pallas_call(
        _merge_kernel,
        grid=(T // tm,),
        in_specs=[row(D_MODEL), row(1024), row(F_WIDTH), row(512), row(3 * D_MODEL),
                  full((1024, D_MODEL)), full((F_WIDTH, D_MODEL)), full((1, D_MODEL)),
                  full((512, D_MODEL)), full((D_MODEL, D_MODEL))],
        out_specs=row(D_MODEL),
        out_shape=jax.ShapeDtypeStruct((T, D_MODEL), F32),
        compiler_params=_cparams(("parallel",)),
        name="merge",
    )(x2, attn, four, og, gate, lw["mla_w_o"], lw["fnet_w"], lw["fnet_b"], lw["gqa_w_o"],
      lw["w_out"])


LOG2E = math.log2(math.e)
PEER_BLK = 256
PEER_RANKS = PEER_TOPK + 1


def _sort_network(n):
    pairs = []

    def merge(lo, m, r):
        step = 2 * r
        if step < m:
            merge(lo, m, step)
            merge(lo + r, m, step)
            pairs.extend((i, i + r) for i in range(lo + r, lo + m - r, step))
        else:
            pairs.append((lo, lo + r))

    def sort(lo, m):
        if m > 1:
            sort(lo, m // 2)
            sort(lo + m // 2, m // 2)
            merge(lo, m, 1)

    sort(0, n)
    return pairs


def _top_of_sorted_lists(lists, k, extra=None):
    lists = list(lists)
    neg = jnp.full(lists[0].shape, -jnp.inf, F32)
    vals = []
    for r in range(k):
        mx = jnp.max(lists[0], axis=0, keepdims=True)
        if extra is not None:
            mx = jnp.maximum(mx, jnp.max(extra, axis=0, keepdims=True))
        vals.append(mx)
        if r == k - 1:
            break
        win = lists[0] == mx
        for v in range(min(len(lists), k - 1 - r)):
            lists[v] = jnp.where(win, lists[v + 1] if v + 1 < len(lists) else neg, lists[v])
        if extra is not None:
            extra = jnp.where(extra == mx, -jnp.inf, extra)
    return vals


def _top_sorted(s, k):
    lists = [s[8 * v:8 * v + 8] for v in range(s.shape[0] // 8)]
    for i, j in _sort_network(len(lists)):
        lists[i], lists[j] = jnp.maximum(lists[i], lists[j]), jnp.minimum(lists[i], lists[j])
    return _top_of_sorted_lists(lists, k)


def _peer_prep_kernel(x_ref, g_ref, wq_ref, keys_ref, hn_ref, a1_ref, th_ref, e2_ref):
    hn = _rms(x_ref[...], g_ref[...])
    hn_ref[...] = hn.T.astype(BF16)
    q = _dot(hn.astype(BF16), wq_ref[...]).astype(BF16)
    tm = q.shape[0]
    for hd in range(PEER_HEADS):
        sts, tops = [], []
        for p in range(2):
            hp = 2 * hd + p
            st = _dot_nt(keys_ref[hp], q[:, hp * PEER_HALF:(hp + 1) * PEER_HALF])
            sts.append(st)
            tops.append(_top_sorted(st, PEER_RANKS))
        a, b = tops
        a_lo = jnp.concatenate(a[:8], axis=0)
        a_hi = jnp.concatenate(a[8:] + [jnp.full((7, tm), -jnp.inf, F32)], axis=0)
        top = _top_of_sorted_lists([a_lo + b[r] for r in range(PEER_RANKS)], PEER_RANKS,
                                   extra=a_hi + b[0])
        m = top[0]
        log_z = jnp.log(sum(jnp.exp(t - m) for t in top[:PEER_TOPK]))
        thr = 0.5 * (top[PEER_TOPK - 1] + top[PEER_TOPK])
        st1 = (sts[0] - a[0]) * LOG2E
        s2k = (sts[1] - b[0] - log_z) * LOG2E
        a1_ref[hd] = jnp.exp2(st1)
        th_ref[hd] = jnp.exp2((thr - m - log_z) * LOG2E - st1 - 1.0)
        e2 = jnp.exp2(s2k - 1.0)
        for c in range(tm // LANES):
            e2_ref[hd, c] = e2[:, c * LANES:(c + 1) * LANES]


def _peer_prep(x2, lw, tm):
    T = x2.shape[0]
    sc = pl.BlockSpec((PEER_HEADS, PEER_N_KEYS, tm), lambda i: (0, 0, i))
    return pl.pallas_call(
        _peer_prep_kernel,
        grid=(T // tm,),
        in_specs=[pl.BlockSpec((tm, D_MODEL), lambda i: (i, 0)),
                  pl.BlockSpec((1, D_MODEL), lambda i: (0, 0)),
                  pl.BlockSpec((D_MODEL, 2 * PEER_HEADS * PEER_HALF), lambda i: (0, 0)),
                  pl.BlockSpec((2 * PEER_HEADS, PEER_N_KEYS, PEER_HALF), lambda i: (0, 0, 0))],
        out_specs=[pl.BlockSpec((D_MODEL, tm), lambda i: (0, i)), sc, sc,
                   pl.BlockSpec((PEER_HEADS, tm // LANES, PEER_N_KEYS, LANES),
                                lambda i: (0, i, 0, 0))],
        out_shape=[jax.ShapeDtypeStruct((D_MODEL, T), BF16)]
        + [jax.ShapeDtypeStruct((PEER_HEADS, PEER_N_KEYS, T), F32)] * 2
        + [jax.ShapeDtypeStruct((PEER_HEADS, T // LANES, PEER_N_KEYS, LANES), F32)],
        compiler_params=_cparams(("parallel",)),
        name="peer_prep",
    )(x2, lw["norm_ffn_g"], lw["peer_w_q"], lw["peer_keys"])


def _peer_main_kernel(x_ref, hn_ref, a1_ref, th_ref, e2_ref, u_ref, vt_ref, fg_ref, o_ref,
                      acc_ref, act_ref, w_ref, *, n_i, final_norm):
    e = pl.program_id(1)
    tm = hn_ref.shape[1]

    @pl.when(e == 0)
    def _():
        acc_ref[...] = jnp.zeros_like(acc_ref)

    n_blk = n_i * PEER_N_KEYS // PEER_BLK
    per_blk = PEER_BLK // PEER_N_KEYS

    n_strip = tm // LANES

    def store_activations(p, slot):
        a = _dot(u_ref[p * PEER_BLK:(p + 1) * PEER_BLK, :], hn_ref[...])
        act = a * (1.0 + lax.erf(a * (1.0 / math.sqrt(2.0))))
        for c in range(n_strip):
            act_ref[slot, c] = act[:, c * LANES:(c + 1) * LANES]

    def fold(p, slot):
        w = jnp.concatenate([w_ref[slot, c] for c in range(n_strip)], axis=1)
        acc_ref[...] += _dot(vt_ref[p], w)

    store_activations(0, 0)
    for p in range(n_blk):
        slot = p % 2
        if p + 1 < n_blk:
            store_activations(p + 1, 1 - slot)
        if p > 0:
            fold(p - 1, 1 - slot)
        for ii in range(per_blk):
            i = p * per_blk + ii
            rows = slice(ii * PEER_N_KEYS, (ii + 1) * PEER_N_KEYS)
            for c in range(n_strip):
                ls = slice(c * LANES, (c + 1) * LANES)
                g = None
                for hd in range(PEER_HEADS):
                    e2 = e2_ref[hd, c]
                    t = jnp.where(e2 > th_ref[hd, i:i + 1, ls], e2, 0.0) * a1_ref[hd, i:i + 1, ls]
                    g = t if g is None else g + t
                w_ref[slot, c, rows, :] = (g * act_ref[slot, c, rows, :]).astype(BF16)
    fold(n_blk - 1, (n_blk - 1) % 2)

    @pl.when(e == pl.num_programs(1) - 1)
    def _():
        y = x_ref[...] + acc_ref[...].T
        if final_norm:
            y = _rms(y, fg_ref[...])
        o_ref[...] = y


def _peer_main(x2, hn, a1, th, e2, lw, final_g, tm, n_i, final_norm):
    T = x2.shape[0]
    et = n_i * PEER_N_KEYS
    sc = pl.BlockSpec((PEER_HEADS, PEER_N_KEYS, tm), lambda t, e: (0, 0, t))
    return pl.pallas_call(
        functools.partial(_peer_main_kernel, n_i=n_i, final_norm=final_norm),
        grid=(T // tm, PEER_N_EXPERTS // et),
        in_specs=[pl.BlockSpec((tm, D_MODEL), lambda t, e: (t, 0)),
                  pl.BlockSpec((D_MODEL, tm), lambda t, e: (0, t)),
                  pl.BlockSpec((PEER_HEADS, n_i, tm), lambda t, e: (0, e, t)),
                  pl.BlockSpec((PEER_HEADS, n_i, tm), lambda t, e: (0, e, t)),
                  pl.BlockSpec((PEER_HEADS, tm // LANES, PEER_N_KEYS, LANES),
                               lambda t, e: (0, t, 0, 0)),
                  pl.BlockSpec((et, D_MODEL), lambda t, e: (e, 0)),
                  pl.BlockSpec((et // PEER_BLK, D_MODEL, PEER_BLK), lambda t, e: (e, 0, 0)),
                  pl.BlockSpec((1, D_MODEL), lambda t, e: (0, 0))],
        out_specs=pl.BlockSpec((tm, D_MODEL), lambda t, e: (t, 0)),
        out_shape=jax.ShapeDtypeStruct((T, D_MODEL), F32),
        scratch_shapes=[pltpu.VMEM((D_MODEL, tm), F32),
                        pltpu.VMEM((2, tm // LANES, PEER_BLK, LANES), F32),
                        pltpu.VMEM((2, tm // LANES, PEER_BLK, LANES), BF16)],
        compiler_params=_cparams(("parallel", "arbitrary")),
        name="peer_main",
    )(x2, hn, a1, th, e2, lw["peer_u"], lw["peer_vt"], final_g)


def _rot_half_cols(w):
    d = w.shape[-1]
    return jnp.concatenate([-w[..., d // 2:], w[..., :d // 2]], axis=-1)


def _pad_cols(w, left, total):
    return jnp.pad(w, ((0, 0), (left, total - left - w.shape[-1])))


def _head_groups(w, n_heads, width, left=0):
    return jnp.concatenate(
        [_pad_cols(w[:, h * width:(h + 1) * width], left, LANES) for h in range(n_heads)], axis=-1)


def _layer_weights(p, l):
    w_in = p["w_in"][l]
    widths = (MLA_Q_LORA, MLA_KV_LORA, MLA_ROPE, F_WIDTH, GQA_HEADS * GQA_HEAD_DIM,
              GQA_KV_HEADS * GQA_HEAD_DIM, GQA_KV_HEADS * GQA_HEAD_DIM, 3 * D_MODEL)
    offs = np.cumsum((0,) + widths)
    wcq, wckv, wkr, wxf, wgq, wgk, wgv, wgate = (w_in[:, offs[i]:offs[i + 1]] for i in range(8))

    def rot_heads(w, n_heads, width):
        return jnp.concatenate(
            [_rot_half_cols(w[:, h * width:(h + 1) * width]) for h in range(n_heads)], axis=-1)

    w1 = jnp.concatenate([
        wcq, wckv,
        _pad_cols(wkr, MLA_NOPE, LANES), _pad_cols(_rot_half_cols(wkr), MLA_NOPE, LANES),
        wxf,
        _head_groups(wgq, GQA_HEADS, GQA_HEAD_DIM),
        _head_groups(rot_heads(wgq, GQA_HEADS, GQA_HEAD_DIM), GQA_HEADS, GQA_HEAD_DIM),
        _head_groups(wgk, GQA_KV_HEADS, GQA_HEAD_DIM),
        _head_groups(rot_heads(wgk, GQA_KV_HEADS, GQA_HEAD_DIM), GQA_KV_HEADS, GQA_HEAD_DIM),
        _head_groups(wgv, GQA_KV_HEADS, GQA_HEAD_DIM),
        wgate], axis=-1).astype(BF16)
    assert w1.shape[-1] == _W1_COLS

    w_uq = p["mla_w_uq"][l]
    qd = MLA_NOPE + MLA_ROPE
    uq, uqr = [], []
    for h in range(MLA_HEADS):
        blk = w_uq[:, h * qd:(h + 1) * qd]
        uq.append(_pad_cols(blk, 0, LANES))
        uqr.append(_pad_cols(_rot_half_cols(blk[:, MLA_NOPE:]), MLA_NOPE, LANES))
    w_ukv = p["mla_w_ukv"][l]
    kd = MLA_NOPE + MLA_V
    uk = [_pad_cols(w_ukv[:, h * kd:h * kd + MLA_NOPE], 0, LANES) for h in range(MLA_HEADS)]
    uv = [_pad_cols(w_ukv[:, h * kd + MLA_NOPE:(h + 1) * kd], 0, LANES) for h in range(MLA_HEADS)]

    def pad_rows(w, n_heads, width):
        return jnp.concatenate(
            [jnp.pad(w[h * width:(h + 1) * width], ((0, LANES - width), (0, 0)))
             for h in range(n_heads)], axis=0)

    return {
        "norm_mix_g": p["norm_mix_g"][l][None, :],
        "w1": w1,
        "q_norm_g": p["mla_q_norm_g"][l][None, :],
        "w_uq": jnp.concatenate(uq, axis=-1).astype(BF16),
        "w_uq_rot": jnp.concatenate(uqr, axis=-1).astype(BF16),
        "kv_norm_g": p["mla_kv_norm_g"][l][None, :],
        "w_ukv": jnp.concatenate(uk + uv, axis=-1).astype(BF16),
        "mla_w_o": pad_rows(p["mla_w_o"][l], MLA_HEADS, MLA_V).astype(BF16),
        "fnet_w": p["fnet_w"][l].astype(BF16),
        "fnet_b": p["fnet_b"][l][None, :],
        "gqa_sink": p["gqa_sink"][l],
        "gqa_w_o": pad_rows(p["gqa_w_o"][l], GQA_HEADS, GQA_HEAD_DIM).astype(BF16),
        "w_out": p["w_out"][l].astype(BF16),
        "norm_ffn_g": p["norm_ffn_g"][l][None, :],
        "peer_w_q": p["peer_w_q"][l].astype(BF16),
        "peer_keys": p["peer_keys"][l].reshape(2 * PEER_HEADS, PEER_N_KEYS, PEER_HALF).astype(BF16),
        "peer_u": p["peer_u"][l].astype(BF16),
        "peer_vt": p["peer_v"][l].astype(BF16).reshape(
            PEER_N_EXPERTS // PEER_BLK, PEER_BLK, D_MODEL).transpose(0, 2, 1),
    }


def _rope_tables(S):
    pos = jnp.arange(S, dtype=F32)[:, None]

    def cs(d):
        inv = 1.0 / (ROPE_THETA ** (jnp.arange(0, d, 2, dtype=F32) / d))
        ang = pos * inv[None, :]
        return (jnp.concatenate([jnp.cos(ang)] * 2, axis=-1),
                jnp.concatenate([jnp.sin(ang)] * 2, axis=-1))

    c, s = cs(MLA_ROPE)
    rest = LANES - MLA_NOPE - MLA_ROPE
    cm = jnp.concatenate([jnp.ones((S, MLA_NOPE), F32), c, jnp.zeros((S, rest), F32)], axis=-1)
    sm = jnp.concatenate([jnp.zeros((S, MLA_NOPE), F32), s, jnp.zeros((S, rest), F32)], axis=-1)
    c, s = cs(GQA_HEAD_DIM)
    zero = jnp.zeros((S, LANES - GQA_HEAD_DIM), F32)
    return {"cm": cm, "sm": sm, "cg": jnp.concatenate([c, zero], axis=-1),
            "sg": jnp.concatenate([s, zero], axis=-1)}


def _fourier_tables(S):
    n2 = 128
    n1 = S // n2

    def cos_sin(num, den):
        ang = (2.0 * math.pi / den) * (num % den).astype(F32)
        return jnp.cos(ang), jnp.sin(ang)

    i1 = jnp.arange(n1, dtype=jnp.int32)
    c1, s1 = cos_sin(i1[:, None] * i1[None, :], n1)
    w1 = jnp.concatenate([c1, -s1], axis=0)
    i2 = jnp.arange(n2, dtype=jnp.int32)
    kk = i1[:, None, None] + n1 * i2[None, :, None]
    mr, ms = cos_sin(kk * i2[None, None, :], S)
    m2 = jnp.concatenate([jnp.concatenate([mr, ms], axis=-1),
                          jnp.concatenate([-ms, mr], axis=-1)], axis=1)
    ic = jnp.arange(F_GROUP_DIM, dtype=jnp.int32)
    cc, cs = cos_sin(ic[:, None] * ic[None, :], F_GROUP_DIM)
    eye = jnp.eye(F_GROUPS, dtype=F32)
    return {"n1": n1, "n2": n2, "w1": w1, "m2": m2, "cc": jnp.kron(eye, cc),
            "cs": jnp.kron(eye, cs)}


def _pick(S, pref):
    t = pref
    while S % t:
        t //= 2
    return t


def _trunk(x, layers, final_g, rope_t, four_t):
    B, S, _ = x.shape
    T = B * S
    x2 = x.reshape(T, D_MODEL)
    tm = _pick(S, 256)
    n_layers = len(layers)
    for l, lw in enumerate(layers):
        q, k, v, xf, qg, kg, vg, gate = _in_proj(x2, lw, rope_t, S, tm)
        attn = _mla_attn(q, k, v, B, S, _pick(S, 1024), _pick(S // 2, 512))
        four = _fourier(xf, four_t, B, S)
        og = _gqa(qg, kg, vg, lw["gqa_sink"], B, S, _pick(S, 512))
        x2 = _merge(x2, attn, four, og, gate, lw, tm)
        hn, a1, th, e2 = _peer_prep(x2, lw, _pick(T, 256))
        x2 = _peer_main(x2, hn, a1, th, e2, lw, final_g, _pick(T, 512), 16,
                        final_norm=(l == n_layers - 1))
    return x2.reshape(B, S, D_MODEL)


def kernel(x_prompt, x_sample, norm_mix_g, w_in, mla_q_norm_g, mla_w_uq, mla_kv_norm_g, mla_w_ukv, mla_w_o, fnet_w, fnet_b, gqa_sink, gqa_w_o, w_out, norm_ffn_g, peer_w_q, peer_keys, peer_u, peer_v, final_norm_g):
    p = dict(norm_mix_g=norm_mix_g, w_in=w_in, mla_q_norm_g=mla_q_norm_g, mla_w_uq=mla_w_uq,
             mla_kv_norm_g=mla_kv_norm_g, mla_w_ukv=mla_w_ukv, mla_w_o=mla_w_o, fnet_w=fnet_w,
             fnet_b=fnet_b, gqa_sink=gqa_sink, gqa_w_o=gqa_w_o, w_out=w_out,
             norm_ffn_g=norm_ffn_g, peer_w_q=peer_w_q, peer_keys=peer_keys, peer_u=peer_u,
             peer_v=peer_v)
    layers = [_layer_weights(p, l) for l in range(w_in.shape[0])]
    final_g = final_norm_g[None, :]
    outs = []
    for x in (x_prompt, x_sample):
        S = x.shape[1]
        outs.append(_trunk(x, layers, final_g, _rope_tables(S), _fourier_tables(S)))
    return tuple(outs)
```

```python
import functools
import math

import jax
import jax.numpy as jnp
import numpy as np
from jax import lax
from jax.experimental import pallas as pl
from jax.experimental.pallas import tpu as pltpu

D_MODEL = 1024
EPS = 1e-6
ROPE_THETA = 10000.0
NEG = -1e30
LANES = 128

MLA_HEADS = 8
MLA_NOPE = 64
MLA_ROPE = 32
MLA_V = 64
MLA_Q_LORA = 384
MLA_KV_LORA = 256
F_GROUPS = 4
F_GROUP_DIM = 64
F_WIDTH = F_GROUPS * F_GROUP_DIM
GQA_HEADS = 4
GQA_KV_HEADS = 2
GQA_GROUP = GQA_HEADS // GQA_KV_HEADS
GQA_HEAD_DIM = 64
WINDOW = 128
PEER_HEADS = 8
PEER_N_KEYS = 128
PEER_N_EXPERTS = PEER_N_KEYS * PEER_N_KEYS
PEER_HALF = 128
PEER_TOPK = 16

VMEM_LIMIT = 60 * 1024 * 1024

BF16 = jnp.bfloat16
F32 = jnp.float32
HIGHEST = lax.Precision.HIGHEST


def _cparams(sem):
    return pltpu.CompilerParams(dimension_semantics=sem, vmem_limit_bytes=VMEM_LIMIT)


def _rms(x, g):
    return x * lax.rsqrt(jnp.mean(x * x, axis=-1, keepdims=True) + EPS) * g


def _dot(a, b):
    return jnp.dot(a, b, preferred_element_type=F32)


def _dot_nt(a, b):
    return lax.dot_general(a, b, (((1,), (1,)), ((), ())), preferred_element_type=F32)


_C_CQ = (0, 384)
_C_CKV = (384, 640)
_C_KR = (640, 768)
_C_KRR = (768, 896)
_C_XF = (896, 1152)
_C_GQ = (1152, 1664)
_C_GQR = (1664, 2176)
_C_GK = (2176, 2432)
_C_GKR = (2432, 2688)
_C_GV = (2688, 2944)
_C_GATE = (2944, 6016)
_W1_COLS = 6016


def _in_proj_kernel(x_ref, g_ref, w1_ref, qng_ref, wuq_ref, wuqr_ref, kvng_ref, wukv_ref,
                    cm_ref, sm_ref, cg_ref, sg_ref,
                    q_ref, k_ref, v_ref, xf_ref, qg_ref, kg_ref, vg_ref, gate_ref):
    h = _rms(x_ref[...], g_ref[...]).astype(BF16)

    def proj(c):
        return _dot(h, w1_ref[:, c[0]:c[1]])

    cm, sm = cm_ref[...], sm_ref[...]
    cg, sg = cg_ref[...], sg_ref[...]

    cqn = _rms(proj(_C_CQ), qng_ref[...]).astype(BF16)
    q = _dot(cqn, wuq_ref[...])
    qr = _dot(cqn, wuqr_ref[...])
    mla_scale = (MLA_NOPE + MLA_ROPE) ** -0.5 * math.log2(math.e)
    for hd in range(MLA_HEADS):
        sl = slice(hd * LANES, (hd + 1) * LANES)
        q_ref[:, sl] = ((q[:, sl] * cm + qr[:, sl] * sm) * mla_scale).astype(BF16)

    ckvn = _rms(proj(_C_CKV), kvng_ref[...]).astype(BF16)
    kv = _dot(ckvn, wukv_ref[...])
    k_rope = proj(_C_KR) * cm + proj(_C_KRR) * sm
    ones_col = (lax.broadcasted_iota(jnp.int32, (1, LANES), 1) == MLA_V).astype(F32)
    for hd in range(MLA_HEADS):
        sl = slice(hd * LANES, (hd + 1) * LANES)
        k_ref[:, sl] = (kv[:, sl] + k_rope).astype(BF16)
        vs = slice(MLA_HEADS * LANES + hd * LANES, MLA_HEADS * LANES + (hd + 1) * LANES)
        v_ref[:, sl] = (kv[:, vs] + ones_col).astype(BF16)

    xf_ref[...] = proj(_C_XF)

    gq, gqr = proj(_C_GQ), proj(_C_GQR)
    gqa_scale = GQA_HEAD_DIM ** -0.5
    for hd in range(GQA_HEADS):
        sl = slice(hd * LANES, (hd + 1) * LANES)
        qg_ref[:, sl] = ((gq[:, sl] * cg + gqr[:, sl] * sg) * gqa_scale).astype(BF16)
    gk, gkr = proj(_C_GK), proj(_C_GKR)
    for hd in range(GQA_KV_HEADS):
        sl = slice(hd * LANES, (hd + 1) * LANES)
        kg_ref[:, sl] = (gk[:, sl] * cg + gkr[:, sl] * sg).astype(BF16)
    vg_ref[...] = proj(_C_GV).astype(BF16)

    gate = proj(_C_GATE)
    gate_ref[...] = (1.0 / (1.0 + jnp.exp(-gate))).astype(gate_ref.dtype)


def _in_proj(x2, lw, tabs, S, tm):
    T = x2.shape[0]
    nt = T // tm
    ns = S // tm
    full = lambda shape: pl.BlockSpec(shape, lambda i: (0,) * len(shape))
    row = lambda w: pl.BlockSpec((tm, w), lambda i: (i, 0))
    tab = pl.BlockSpec((tm, LANES), lambda i: (i % ns, 0))
    outs = [(1024, BF16), (1024, BF16), (1024, BF16), (F_WIDTH, F32), (512, BF16), (256, BF16),
            (256, BF16), (3 * D_MODEL, BF16)]
    return pl.pallas_call(
        _in_proj_kernel,
        grid=(nt,),
        in_specs=[row(D_MODEL), full((1, D_MODEL)), full((D_MODEL, _W1_COLS)),
                  full((1, MLA_Q_LORA)), full((MLA_Q_LORA, 1024)), full((MLA_Q_LORA, 1024)),
                  full((1, MLA_KV_LORA)), full((MLA_KV_LORA, 2048)),
                  tab, tab, tab, tab],
        out_specs=[row(w) for w, _ in outs],
        out_shape=[jax.ShapeDtypeStruct((T, w), dt) for w, dt in outs],
        compiler_params=_cparams(("parallel",)),
        name="in_proj",
    )(x2, lw["norm_mix_g"], lw["w1"], lw["q_norm_g"], lw["w_uq"], lw["w_uq_rot"],
      lw["kv_norm_g"], lw["w_ukv"], tabs["cm"], tabs["sm"], tabs["cg"], tabs["sg"])


def _mla_attn_kernel(q_ref, k_ref, v_ref, o_ref, s_ref, p_ref, *, tk):
    tq = q_ref.shape[0]
    nk = k_ref.shape[0] // tk
    q = q_ref[...]

    def chunk(ref, c):
        return ref[pl.ds(pl.multiple_of(c * tk, tk), tk), :]

    def step(slot, v_prev, k_next, carry):
        m, alpha, acc = carry
        pv = _dot(p_ref[1 - slot], v_prev)
        s_ref[1 - slot] = _dot_nt(q, k_next)
        s = s_ref[slot]
        m_new = jnp.maximum(m, jnp.max(s, axis=-1, keepdims=True))
        p_ref[slot] = jnp.exp2(s - m_new).astype(BF16)
        return m_new, jnp.exp2(m - m_new), acc * alpha + pv

    def static_chunk(ref, c):
        return ref[c * tk:(c + 1) * tk, :]

    s_ref[0] = _dot_nt(q, static_chunk(k_ref, 0))
    p_ref[1] = jnp.zeros(p_ref.shape[1:], BF16)
    carry = (jnp.full((tq, 1), NEG, F32), jnp.ones((tq, 1), F32), jnp.zeros((tq, LANES), F32))
    for c in range(nk):
        carry = step(c % 2, static_chunk(v_ref, max(c - 1, 0)),
                     static_chunk(k_ref, min(c + 1, nk - 1)), carry)
    _, alpha, acc = carry
    acc = acc * alpha + _dot(p_ref[1], static_chunk(v_ref, nk - 1))
    o_ref[...] = (acc / acc[:, MLA_V:MLA_V + 1]).astype(o_ref.dtype)


def _mla_attn(q, k, v, B, S, tq, tk):
    assert (S // tk) % 2 == 0
    q3, k3, v3 = (a.reshape(B, S, MLA_HEADS * LANES) for a in (q, k, v))
    out = pl.pallas_call(
        functools.partial(_mla_attn_kernel, tk=tk),
        scratch_shapes=[pltpu.VMEM((2, tq, tk), F32), pltpu.VMEM((2, tq, tk), BF16)],
        grid=(B, MLA_HEADS, S // tq),
        in_specs=[pl.BlockSpec((None, tq, LANES), lambda b, h, i: (b, i, h)),
                  pl.BlockSpec((None, S, LANES), lambda b, h, i: (b, 0, h)),
                  pl.BlockSpec((None, S, LANES), lambda b, h, i: (b, 0, h))],
        out_specs=pl.BlockSpec((None, tq, LANES), lambda b, h, i: (b, i, h)),
        out_shape=jax.ShapeDtypeStruct((B, S, MLA_HEADS * LANES), BF16),
        compiler_params=_cparams(("parallel", "parallel", "parallel")),
        name="mla_attn",
    )(q3, k3, v3)
    return out.reshape(B * S, MLA_HEADS * LANES)


def _fft1_kernel(w_ref, x_ref, o_ref):
    o_ref[...] = jnp.dot(w_ref[...], x_ref[...], preferred_element_type=F32, precision=HIGHEST)


def _fft2_kernel(m_ref, a_ref, cc_ref, cs_ref, o_ref, *, scale):
    aa = jnp.concatenate([a_ref[0], a_ref[1]], axis=0)
    y = jnp.dot(m_ref[...], aa, preferred_element_type=F32, precision=HIGHEST)
    n2 = y.shape[0] // 2
    out = (jnp.dot(y[:n2], cc_ref[...], preferred_element_type=F32, precision=HIGHEST)
           + jnp.dot(y[n2:], cs_ref[...], preferred_element_type=F32, precision=HIGHEST))
    o_ref[...] = out * scale


def _fourier(xf, ft, B, S):
    n1, n2 = ft["n1"], ft["n2"]
    cols = n2 * F_WIDTH
    tc = min(cols, 4096)
    x3 = xf.reshape(B, n1, cols)
    a = pl.pallas_call(
        _fft1_kernel,
        grid=(B, cols // tc),
        in_specs=[pl.BlockSpec((2 * n1, n1), lambda b, j: (0, 0)),
                  pl.BlockSpec((None, n1, tc), lambda b, j: (b, 0, j))],
        out_specs=pl.BlockSpec((None, 2 * n1, tc), lambda b, j: (b, 0, j)),
        out_shape=jax.ShapeDtypeStruct((B, 2 * n1, cols), F32),
        compiler_params=_cparams(("parallel", "parallel")),
        name="fourier_stage1",
    )(ft["w1"], x3)
    a5 = a.reshape(B, 2, n1, n2, F_WIDTH)
    y = pl.pallas_call(
        functools.partial(_fft2_kernel, scale=1.0 / math.sqrt(S * F_GROUP_DIM)),
        grid=(B, n1),
        in_specs=[pl.BlockSpec((None, 2 * n2, 2 * n2), lambda b, k: (k, 0, 0)),
                  pl.BlockSpec((None, 2, None, n2, F_WIDTH), lambda b, k: (b, 0, k, 0, 0)),
                  pl.BlockSpec((F_WIDTH, F_WIDTH), lambda b, k: (0, 0)),
                  pl.BlockSpec((F_WIDTH, F_WIDTH), lambda b, k: (0, 0))],
        out_specs=pl.BlockSpec((None, None, n2, F_WIDTH), lambda b, k: (b, k, 0, 0)),
        out_shape=jax.ShapeDtypeStruct((B, n1, n2, F_WIDTH), F32),
        compiler_params=_cparams(("parallel", "parallel")),
        name="fourier_stage2",
    )(ft["m2"], a5, ft["cc"], ft["cs"])
    return jnp.transpose(y, (0, 2, 1, 3)).reshape(B * S, F_WIDTH)


def _gqa_kernel(sink_ref, q_ref, kp_ref, kc_ref, kn_ref, vp_ref, vc_ref, vn_ref, o_ref, *, S):
    n = pl.program_id(1)
    tq, halo = q_ref.shape[0], kp_ref.shape[0]
    width = tq + 2 * halo
    qpos = n * tq + lax.broadcasted_iota(jnp.int32, (tq, width), 0)
    kpos = n * tq - halo + lax.broadcasted_iota(jnp.int32, (tq, width), 1)
    mask = (jnp.abs(qpos - kpos) <= WINDOW) & (kpos >= 0) & (kpos < S)
    for hd in range(GQA_HEADS):
        kh = hd // GQA_GROUP
        qs = slice(hd * LANES, (hd + 1) * LANES)
        ks = slice(kh * LANES, (kh + 1) * LANES)
        q = q_ref[:, qs]
        s = jnp.concatenate([_dot_nt(q, kp_ref[:, ks]), _dot_nt(q, kc_ref[:, ks]),
                             _dot_nt(q, kn_ref[:, ks])], axis=-1)
        s = jnp.where(mask, s, NEG)
        sk = sink_ref[hd]
        m = jnp.maximum(jnp.max(s, axis=-1, keepdims=True), sk)
        p = jnp.exp(s - m)
        den = jnp.sum(p, axis=-1, keepdims=True) + jnp.exp(sk - m)
        pb = (p / den).astype(BF16)
        o = (_dot(pb[:, :halo], vp_ref[:, ks]) + _dot(pb[:, halo:halo + tq], vc_ref[:, ks])
             + _dot(pb[:, halo + tq:], vn_ref[:, ks]))
        o_ref[:, qs] = o.astype(o_ref.dtype)


def _gqa(qg, kg, vg, sink, B, S, tq):
    halo = WINDOW
    per = tq // halo
    nb = S // halo
    q3 = qg.reshape(B, S, GQA_HEADS * LANES)
    k3 = kg.reshape(B, S, GQA_KV_HEADS * LANES)
    v3 = vg.reshape(B, S, GQA_KV_HEADS * LANES)
    kvw = GQA_KV_HEADS * LANES
    prev = pl.BlockSpec((None, halo, kvw), lambda b, n: (b, jnp.maximum(n * per - 1, 0), 0))
    cur = pl.BlockSpec((None, tq, kvw), lambda b, n: (b, n, 0))
    nxt = pl.BlockSpec((None, halo, kvw), lambda b, n: (b, jnp.minimum((n + 1) * per, nb - 1), 0))
    out = pl.pallas_call(
        functools.partial(_gqa_kernel, S=S),
        grid=(B, S // tq),
        in_specs=[pl.BlockSpec(memory_space=pltpu.SMEM),
                  pl.BlockSpec((None, tq, GQA_HEADS * LANES), lambda b, n: (b, n, 0)),
                  prev, cur, nxt, prev, cur, nxt],
        out_specs=pl.BlockSpec((None, tq, GQA_HEADS * LANES), lambda b, n: (b, n, 0)),
        out_shape=jax.ShapeDtypeStruct((B, S, GQA_HEADS * LANES), BF16),
        compiler_params=_cparams(("parallel", "parallel")),
        name="gqa_window",
    )(sink, q3, k3, k3, k3, v3, v3, v3)
    return out.reshape(B * S, GQA_HEADS * LANES)


def _merge_kernel(x_ref, attn_ref, four_ref, og_ref, gate_ref, wo_ref, fw_ref, fb_ref, gwo_ref,
                  wout_ref, o_ref):
    mla_o = _dot(attn_ref[...], wo_ref[...])
    fnet_o = _dot(four_ref[...].astype(BF16), fw_ref[...]) + fb_ref[...]
    gqa_o = _dot(og_ref[...], gwo_ref[...])
    d = D_MODEL
    merged = (gate_ref[:, 0:d] * mla_o + gate_ref[:, d:2 * d] * fnet_o
              + gate_ref[:, 2 * d:3 * d] * gqa_o)
    o_ref[...] = x_ref[...] + _dot(merged.astype(BF16), wout_ref[...])


def _merge(x2, attn, four, og, gate, lw, tm):
    T = x2.shape[0]
    full = lambda shape: pl.BlockSpec(shape, lambda i: (0,) * len(shape))
    row = lambda w: pl.BlockSpec((tm, w), lambda i: (i, 0))
    return pl.pallas_call(
        _merge_kernel,
        grid=(T // tm,),
        in_specs=[row(D_MODEL), row(1024), row(F_WIDTH), row(512), row(3 * D_MODEL),
                  full((1024, D_MODEL)), full((F_WIDTH, D_MODEL)), full((1, D_MODEL)),
                  full((512, D_MODEL)), full((D_MODEL, D_MODEL))],
        out_specs=row(D_MODEL),
        out_shape=jax.ShapeDtypeStruct((T, D_MODEL), F32),
        compiler_params=_cparams(("parallel",)),
        name="merge",
    )(x2, attn, four, og, gate, lw["mla_w_o"], lw["fnet_w"], lw["fnet_b"], lw["gqa_w_o"],
      lw["w_out"])


LOG2E = math.log2(math.e)
PEER_BLK = 256
PEER_RANKS = PEER_TOPK + 1


def _sort_network(n):
    pairs = []

    def merge(lo, m, r):
        step = 2 * r
        if step < m:
            merge(lo, m, step)
            merge(lo + r, m, step)
            pairs.extend((i, i + r) for i in range(lo + r, lo + m - r, step))
        else:
            pairs.append((lo, lo + r))

    def sort(lo, m):
        if m > 1:
            sort(lo, m // 2)
            sort(lo + m // 2, m // 2)
            merge(lo, m, 1)

    sort(0, n)
    return pairs


def _top_of_sorted_lists(lists, k, extra=None):
    lists = list(lists)
    neg = jnp.full(lists[0].shape, -jnp.inf, F32)
    vals = []
    for r in range(k):
        mx = jnp.max(lists[0], axis=0, keepdims=True)
        if extra is not None:
            mx = jnp.maximum(mx, jnp.max(extra, axis=0, keepdims=True))
        vals.append(mx)
        if r == k - 1:
            break
        win = lists[0] == mx
        for v in range(min(len(lists), k - 1 - r)):
            lists[v] = jnp.where(win, lists[v + 1] if v + 1 < len(lists) else neg, lists[v])
        if extra is not None:
            extra = jnp.where(extra == mx, -jnp.inf, extra)
    return vals


def _top_sorted(s, k):
    lists = [s[8 * v:8 * v + 8] for v in range(s.shape[0] // 8)]
    for i, j in _sort_network(len(lists)):
        lists[i], lists[j] = jnp.maximum(lists[i], lists[j]), jnp.minimum(lists[i], lists[j])
    return _top_of_sorted_lists(lists, k)


def _peer_prep_kernel(x_ref, g_ref, wq_ref, keys_ref, hn_ref, a1_ref, th_ref, e2_ref):
    hn = _rms(x_ref[...], g_ref[...])
    hn_ref[...] = hn.T.astype(BF16)
    q = _dot(hn.astype(BF16), wq_ref[...]).astype(BF16)
    tm = q.shape[0]
    for hd in range(PEER_HEADS):
        sts, tops = [], []
        for p in range(2):
            hp = 2 * hd + p
            st = _dot_nt(keys_ref[hp], q[:, hp * PEER_HALF:(hp + 1) * PEER_HALF])
            sts.append(st)
            tops.append(_top_sorted(st, PEER_RANKS))
        a, b = tops
        a_lo = jnp.concatenate(a[:8], axis=0)
        a_hi = jnp.concatenate(a[8:] + [jnp.full((7, tm), -jnp.inf, F32)], axis=0)
        top = _top_of_sorted_lists([a_lo + b[r] for r in range(PEER_RANKS)], PEER_RANKS,
                                   extra=a_hi + b[0])
        m = top[0]
        log_z = jnp.log(sum(jnp.exp(t - m) for t in top[:PEER_TOPK]))
        thr = 0.5 * (top[PEER_TOPK - 1] + top[PEER_TOPK])
        st1 = (sts[0] - a[0]) * LOG2E
        s2k = (sts[1] - b[0] - log_z) * LOG2E
        a1_ref[hd] = jnp.exp2(st1)
        th_ref[hd] = jnp.exp2((thr - m - log_z) * LOG2E - st1 - 1.0)
        e2 = jnp.exp2(s2k - 1.0)
        for c in range(tm // LANES):
            e2_ref[hd, c] = e2[:, c * LANES:(c + 1) * LANES]


def _peer_prep(x2, lw, tm):
    T = x2.shape[0]
    sc = pl.BlockSpec((PEER_HEADS, PEER_N_KEYS, tm), lambda i: (0, 0, i))
    return pl.pallas_call(
        _peer_prep_kernel,
        grid=(T // tm,),
        in_specs=[pl.BlockSpec((tm, D_MODEL), lambda i: (i, 0)),
                  pl.BlockSpec((1, D_MODEL), lambda i: (0, 0)),
                  pl.BlockSpec((D_MODEL, 2 * PEER_HEADS * PEER_HALF), lambda i: (0, 0)),
                  pl.BlockSpec((2 * PEER_HEADS, PEER_N_KEYS, PEER_HALF), lambda i: (0, 0, 0))],
        out_specs=[pl.BlockSpec((D_MODEL, tm), lambda i: (0, i)), sc, sc,
                   pl.BlockSpec((PEER_HEADS, tm // LANES, PEER_N_KEYS, LANES),
                                lambda i: (0, i, 0, 0))],
        out_shape=[jax.ShapeDtypeStruct((D_MODEL, T), BF16)]
        + [jax.ShapeDtypeStruct((PEER_HEADS, PEER_N_KEYS, T), F32)] * 2
        + [jax.ShapeDtypeStruct((PEER_HEADS, T // LANES, PEER_N_KEYS, LANES), F32)],
        compiler_params=_cparams(("parallel",)),
        name="peer_prep",
    )(x2, lw["norm_ffn_g"], lw["peer_w_q"], lw["peer_keys"])


def _peer_main_kernel(x_ref, hn_ref, a1_ref, th_ref, e2_ref, u_ref, vt_ref, fg_ref, o_ref,
                      acc_ref, act_ref, w_ref, *, n_i, final_norm):
    e = pl.program_id(1)
    tm = hn_ref.shape[1]

    @pl.when(e == 0)
    def _():
        acc_ref[...] = jnp.zeros_like(acc_ref)

    n_blk = n_i * PEER_N_KEYS // PEER_BLK
    per_blk = PEER_BLK // PEER_N_KEYS

    n_strip = tm // LANES

    def store_activations(p, slot):
        a = _dot(u_ref[p * PEER_BLK:(p + 1) * PEER_BLK, :], hn_ref[...])
        act = a * (1.0 + lax.erf(a * (1.0 / math.sqrt(2.0))))
        for c in range(n_strip):
            act_ref[slot, c] = act[:, c * LANES:(c + 1) * LANES]

    def fold(p, slot):
        w = jnp.concatenate([w_ref[slot, c] for c in range(n_strip)], axis=1)
        acc_ref[...] += _dot(vt_ref[p], w)

    store_activations(0, 0)
    for p in range(n_blk):
        slot = p % 2
        if p + 1 < n_blk:
            store_activations(p + 1, 1 - slot)
        if p > 0:
            fold(p - 1, 1 - slot)
        for ii in range(per_blk):
            i = p * per_blk + ii
            rows = slice(ii * PEER_N_KEYS, (ii + 1) * PEER_N_KEYS)
            for c in range(n_strip):
                ls = slice(c * LANES, (c + 1) * LANES)
                g = None
                for hd in range(PEER_HEADS):
                    e2 = e2_ref[hd, c]
                    t = jnp.where(e2 > th_ref[hd, i:i + 1, ls], e2, 0.0) * a1_ref[hd, i:i + 1, ls]
                    g = t if g is None else g + t
                w_ref[slot, c, rows, :] = (g * act_ref[slot, c, rows, :]).astype(BF16)
    fold(n_blk - 1, (n_blk - 1) % 2)

    @pl.when(e == pl.num_programs(1) - 1)
    def _():
        y = x_ref[...] + acc_ref[...].T
        if final_norm:
            y = _rms(y, fg_ref[...])
        o_ref[...] = y


def _peer_main(x2, hn, a1, th, e2, lw, final_g, tm, n_i, final_norm):
    T = x2.shape[0]
    et = n_i * PEER_N_KEYS
    sc = pl.BlockSpec((PEER_HEADS, PEER_N_KEYS, tm), lambda t, e: (0, 0, t))
    return pl.pallas_call(
        functools.partial(_peer_main_kernel, n_i=n_i, final_norm=final_norm),
        grid=(T // tm, PEER_N_EXPERTS // et),
        in_specs=[pl.BlockSpec((tm, D_MODEL), lambda t, e: (t, 0)),
                  pl.BlockSpec((D_MODEL, tm), lambda t, e: (0, t)),
                  pl.BlockSpec((PEER_HEADS, n_i, tm), lambda t, e: (0, e, t)),
                  pl.BlockSpec((PEER_HEADS, n_i, tm), lambda t, e: (0, e, t)),
                  pl.BlockSpec((PEER_HEADS, tm // LANES, PEER_N_KEYS, LANES),
                               lambda t, e: (0, t, 0, 0)),
                  pl.BlockSpec((et, D_MODEL), lambda t, e: (e, 0)),
                  pl.BlockSpec((et // PEER_BLK, D_MODEL, PEER_BLK), lambda t, e: (e, 0, 0)),
                  pl.BlockSpec((1, D_MODEL), lambda t, e: (0, 0))],
        out_specs=pl.BlockSpec((tm, D_MODEL), lambda t, e: (t, 0)),
        out_shape=jax.ShapeDtypeStruct((T, D_MODEL), F32),
        scratch_shapes=[pltpu.VMEM((D_MODEL, tm), F32),
                        pltpu.VMEM((2, tm // LANES, PEER_BLK, LANES), F32),
                        pltpu.VMEM((2, tm // LANES, PEER_BLK, LANES), BF16)],
        compiler_params=_cparams(("parallel", "arbitrary")),
        name="peer_main",
    )(x2, hn, a1, th, e2, lw["peer_u"], lw["peer_vt"], final_g)


def _rot_half_cols(w):
    d = w.shape[-1]
    return jnp.concatenate([-w[..., d // 2:], w[..., :d // 2]], axis=-1)


def _pad_cols(w, left, total):
    return jnp.pad(w, ((0, 0), (left, total - left - w.shape[-1])))


def _head_groups(w, n_heads, width, left=0):
    return jnp.concatenate(
        [_pad_cols(w[:, h * width:(h + 1) * width], left, LANES) for h in range(n_heads)], axis=-1)


def _layer_weights(p, l):
    w_in = p["w_in"][l]
    widths = (MLA_Q_LORA, MLA_KV_LORA, MLA_ROPE, F_WIDTH, GQA_HEADS * GQA_HEAD_DIM,
              GQA_KV_HEADS * GQA_HEAD_DIM, GQA_KV_HEADS * GQA_HEAD_DIM, 3 * D_MODEL)
    offs = np.cumsum((0,) + widths)
    wcq, wckv, wkr, wxf, wgq, wgk, wgv, wgate = (w_in[:, offs[i]:offs[i + 1]] for i in range(8))

    def rot_heads(w, n_heads, width):
        return jnp.concatenate(
            [_rot_half_cols(w[:, h * width:(h + 1) * width]) for h in range(n_heads)], axis=-1)

    w1 = jnp.concatenate([
        wcq, wckv,
        _pad_cols(wkr, MLA_NOPE, LANES), _pad_cols(_rot_half_cols(wkr), MLA_NOPE, LANES),
        wxf,
        _head_groups(wgq, GQA_HEADS, GQA_HEAD_DIM),
        _head_groups(rot_heads(wgq, GQA_HEADS, GQA_HEAD_DIM), GQA_HEADS, GQA_HEAD_DIM),
        _head_groups(wgk, GQA_KV_HEADS, GQA_HEAD_DIM),
        _head_groups(rot_heads(wgk, GQA_KV_HEADS, GQA_HEAD_DIM), GQA_KV_HEADS, GQA_HEAD_DIM),
        _head_groups(wgv, GQA_KV_HEADS, GQA_HEAD_DIM),
        wgate], axis=-1).astype(BF16)
    assert w1.shape[-1] == _W1_COLS

    w_uq = p["mla_w_uq"][l]
    qd = MLA_NOPE + MLA_ROPE
    uq, uqr = [], []
    for h in range(MLA_HEADS):
        blk = w_uq[:, h * qd:(h + 1) * qd]
        uq.append(_pad_cols(blk, 0, LANES))
        uqr.append(_pad_cols(_rot_half_cols(blk[:, MLA_NOPE:]), MLA_NOPE, LANES))
    w_ukv = p["mla_w_ukv"][l]
    kd = MLA_NOPE + MLA_V
    uk = [_pad_cols(w_ukv[:, h * kd:h * kd + MLA_NOPE], 0, LANES) for h in range(MLA_HEADS)]
    uv = [_pad_cols(w_ukv[:, h * kd + MLA_NOPE:(h + 1) * kd], 0, LANES) for h in range(MLA_HEADS)]

    def pad_rows(w, n_heads, width):
        return jnp.concatenate(
            [jnp.pad(w[h * width:(h + 1) * width], ((0, LANES - width), (0, 0)))
             for h in range(n_heads)], axis=0)

    return {
        "norm_mix_g": p["norm_mix_g"][l][None, :],
        "w1": w1,
        "q_norm_g": p["mla_q_norm_g"][l][None, :],
        "w_uq": jnp.concatenate(uq, axis=-1).astype(BF16),
        "w_uq_rot": jnp.concatenate(uqr, axis=-1).astype(BF16),
        "kv_norm_g": p["mla_kv_norm_g"][l][None, :],
        "w_ukv": jnp.concatenate(uk + uv, axis=-1).astype(BF16),
        "mla_w_o": pad_rows(p["mla_w_o"][l], MLA_HEADS, MLA_V).astype(BF16),
        "fnet_w": p["fnet_w"][l].astype(BF16),
        "fnet_b": p["fnet_b"][l][None, :],
        "gqa_sink": p["gqa_sink"][l],
        "gqa_w_o": pad_rows(p["gqa_w_o"][l], GQA_HEADS, GQA_HEAD_DIM).astype(BF16),
        "w_out": p["w_out"][l].astype(BF16),
        "norm_ffn_g": p["norm_ffn_g"][l][None, :],
        "peer_w_q": p["peer_w_q"][l].astype(BF16),
        "peer_keys": p["peer_keys"][l].reshape(2 * PEER_HEADS, PEER_N_KEYS, PEER_HALF).astype(BF16),
        "peer_u": p["peer_u"][l].astype(BF16),
        "peer_vt": p["peer_v"][l].astype(BF16).reshape(
            PEER_N_EXPERTS // PEER_BLK, PEER_BLK, D_MODEL).transpose(0, 2, 1),
    }


def _rope_tables(S):
    pos = jnp.arange(S, dtype=F32)[:, None]

    def cs(d):
        inv = 1.0 / (ROPE_THETA ** (jnp.arange(0, d, 2, dtype=F32) / d))
        ang = pos * inv[None, :]
        return (jnp.concatenate([jnp.cos(ang)] * 2, axis=-1),
                jnp.concatenate([jnp.sin(ang)] * 2, axis=-1))

    c, s = cs(MLA_ROPE)
    rest = LANES - MLA_NOPE - MLA_ROPE
    cm = jnp.concatenate([jnp.ones((S, MLA_NOPE), F32), c, jnp.zeros((S, rest), F32)], axis=-1)
    sm = jnp.concatenate([jnp.zeros((S, MLA_NOPE), F32), s, jnp.zeros((S, rest), F32)], axis=-1)
    c, s = cs(GQA_HEAD_DIM)
    zero = jnp.zeros((S, LANES - GQA_HEAD_DIM), F32)
    return {"cm": cm, "sm": sm, "cg": jnp.concatenate([c, zero], axis=-1),
            "sg": jnp.concatenate([s, zero], axis=-1)}


def _fourier_tables(S):
    n2 = 128
    n1 = S // n2

    def cos_sin(num, den):
        ang = (2.0 * math.pi / den) * (num % den).astype(F32)
        return jnp.cos(ang), jnp.sin(ang)

    i1 = jnp.arange(n1, dtype=jnp.int32)
    c1, s1 = cos_sin(i1[:, None] * i1[None, :], n1)
    w1 = jnp.concatenate([c1, -s1], axis=0)
    i2 = jnp.arange(n2, dtype=jnp.int32)
    kk = i1[:, None, None] + n1 * i2[None, :, None]
    mr, ms = cos_sin(kk * i2[None, None, :], S)
    m2 = jnp.concatenate([jnp.concatenate([mr, ms], axis=-1),
                          jnp.concatenate([-ms, mr], axis=-1)], axis=1)
    ic = jnp.arange(F_GROUP_DIM, dtype=jnp.int32)
    cc, cs = cos_sin(ic[:, None] * ic[None, :], F_GROUP_DIM)
    eye = jnp.eye(F_GROUPS, dtype=F32)
    return {"n1": n1, "n2": n2, "w1": w1, "m2": m2, "cc": jnp.kron(eye, cc),
            "cs": jnp.kron(eye, cs)}


def _pick(S, pref):
    t = pref
    while S % t:
        t //= 2
    return t


def _trunk(x, layers, final_g, rope_t, four_t):
    B, S, _ = x.shape
    T = B * S
    x2 = x.reshape(T, D_MODEL)
    tm = _pick(S, 256)
    n_layers = len(layers)
    for l, lw in enumerate(layers):
        q, k, v, xf, qg, kg, vg, gate = _in_proj(x2, lw, rope_t, S, tm)
        attn = _mla_attn(q, k, v, B, S, _pick(S, 1024), _pick(S // 2, 512))
        four = _fourier(xf, four_t, B, S)
        og = _gqa(qg, kg, vg, lw["gqa_sink"], B, S, _pick(S, 512))
        x2 = _merge(x2, attn, four, og, gate, lw, tm)
        hn, a1, th, e2 = _peer_prep(x2, lw, _pick(T, 256))
        x2 = _peer_main(x2, hn, a1, th, e2, lw, final_g, _pick(T, 1024), 16,
                        final_norm=(l == n_layers - 1))
    return x2.reshape(B, S, D_MODEL)


def kernel(x_prompt, x_sample, norm_mix_g, w_in, mla_q_norm_g, mla_w_uq, mla_kv_norm_g, mla_w_ukv, mla_w_o, fnet_w, fnet_b, gqa_sink, gqa_w_o, w_out, norm_ffn_g, peer_w_q, peer_keys, peer_u, peer_v, final_norm_g):
    p = dict(norm_mix_g=norm_mix_g, w_in=w_in, mla_q_norm_g=mla_q_norm_g, mla_w_uq=mla_w_uq,
             mla_kv_norm_g=mla_kv_norm_g, mla_w_ukv=mla_w_ukv, mla_w_o=mla_w_o, fnet_w=fnet_w,
             fnet_b=fnet_b, gqa_sink=gqa_sink, gqa_w_o=gqa_w_o, w_out=w_out,
             norm_ffn_g=norm_ffn_g, peer_w_q=peer_w_q, peer_keys=peer_keys, peer_u=peer_u,
             peer_v=peer_v)
    layers = [_layer_weights(p, l) for l in range(w_in.shape[0])]
    final_g = final_norm_g[None, :]
    outs = []
    for x in (x_prompt, x_sample):
        S = x.shape[1]
        outs.append(_trunk(x, layers, final_g, _rope_tables(S), _fourier_tables(S)))
    return tuple(outs)
```

```python
import functools
import math

import jax
import jax.numpy as jnp
import numpy as np
from jax import lax
from jax.experimental import pallas as pl
from jax.experimental.pallas import tpu as pltpu

D_MODEL = 1024
EPS = 1e-6
ROPE_THETA = 10000.0
NEG = -1e30
LANES = 128

MLA_HEADS = 8
MLA_NOPE = 64
MLA_ROPE = 32
MLA_V = 64
MLA_Q_LORA = 384
MLA_KV_LORA = 256
F_GROUPS = 4
F_GROUP_DIM = 64
F_WIDTH = F_GROUPS * F_GROUP_DIM
GQA_HEADS = 4
GQA_KV_HEADS = 2
GQA_GROUP = GQA_HEADS // GQA_KV_HEADS
GQA_HEAD_DIM = 64
WINDOW = 128
PEER_HEADS = 8
PEER_N_KEYS = 128
PEER_N_EXPERTS = PEER_N_KEYS * PEER_N_KEYS
PEER_HALF = 128
PEER_TOPK = 16

VMEM_LIMIT = 56 * 1024 * 1024

BF16 = jnp.bfloat16
F32 = jnp.float32
HIGHEST = lax.Precision.HIGHEST


def _cparams(sem):
    return pltpu.CompilerParams(dimension_semantics=sem, vmem_limit_bytes=VMEM_LIMIT)


def _rms(x, g):
    return x * lax.rsqrt(jnp.mean(x * x, axis=-1, keepdims=True) + EPS) * g


def _dot(a, b):
    return jnp.dot(a, b, preferred_element_type=F32)


def _dot_nt(a, b):
    return lax.dot_general(a, b, (((1,), (1,)), ((), ())), preferred_element_type=F32)


_C_CQ = (0, 384)
_C_CKV = (384, 640)
_C_KR = (640, 768)
_C_KRR = (768, 896)
_C_XF = (896, 1152)
_C_GQ = (1152, 1664)
_C_GQR = (1664, 2176)
_C_GK = (2176, 2432)
_C_GKR = (2432, 2688)
_C_GV = (2688, 2944)
_C_GATE = (2944, 6016)
_W1_COLS = 6016


def _in_proj_kernel(x_ref, g_ref, w1_ref, qng_ref, wuq_ref, wuqr_ref, kvng_ref, wukv_ref,
                    cm_ref, sm_ref, cg_ref, sg_ref,
                    q_ref, k_ref, v_ref, xf_ref, qg_ref, kg_ref, vg_ref, gate_ref):
    h = _rms(x_ref[...], g_ref[...]).astype(BF16)

    def proj(c):
        return _dot(h, w1_ref[:, c[0]:c[1]])

    cm, sm = cm_ref[...], sm_ref[...]
    cg, sg = cg_ref[...], sg_ref[...]

    cqn = _rms(proj(_C_CQ), qng_ref[...]).astype(BF16)
    q = _dot(cqn, wuq_ref[...])
    qr = _dot(cqn, wuqr_ref[...])
    mla_scale = (MLA_NOPE + MLA_ROPE) ** -0.5 * math.log2(math.e)
    for hd in range(MLA_HEADS):
        sl = slice(hd * LANES, (hd + 1) * LANES)
        q_ref[:, sl] = ((q[:, sl] * cm + qr[:, sl] * sm) * mla_scale).astype(BF16)

    ckvn = _rms(proj(_C_CKV), kvng_ref[...]).astype(BF16)
    kv = _dot(ckvn, wukv_ref[...])
    k_rope = proj(_C_KR) * cm + proj(_C_KRR) * sm
    for hd in range(MLA_HEADS):
        sl = slice(hd * LANES, (hd + 1) * LANES)
        k_ref[:, sl] = (kv[:, sl] + k_rope).astype(BF16)
    lane = lax.broadcasted_iota(jnp.int32, (1, MLA_HEADS * LANES), 1)
    ones_cols = (lane % LANES == MLA_V).astype(F32)
    v_ref[...] = (kv[:, MLA_HEADS * LANES:] + ones_cols).T.astype(BF16)

    xf_ref[...] = proj(_C_XF)

    gq, gqr = proj(_C_GQ), proj(_C_GQR)
    gqa_scale = GQA_HEAD_DIM ** -0.5
    for hd in range(GQA_HEADS):
        sl = slice(hd * LANES, (hd + 1) * LANES)
        qg_ref[:, sl] = ((gq[:, sl] * cg + gqr[:, sl] * sg) * gqa_scale).astype(BF16)
    gk, gkr = proj(_C_GK), proj(_C_GKR)
    for hd in range(GQA_KV_HEADS):
        sl = slice(hd * LANES, (hd + 1) * LANES)
        kg_ref[:, sl] = (gk[:, sl] * cg + gkr[:, sl] * sg).astype(BF16)
    vg_ref[...] = proj(_C_GV).astype(BF16)

    gate = proj(_C_GATE)
    gate_ref[...] = (1.0 / (1.0 + jnp.exp(-gate))).astype(gate_ref.dtype)


def _in_proj(x2, lw, tabs, S, tm):
    T = x2.shape[0]
    nt = T // tm
    ns = S // tm
    full = lambda shape: pl.BlockSpec(shape, lambda i: (0,) * len(shape))
    row = lambda w: pl.BlockSpec((tm, w), lambda i: (i, 0))
    tab = pl.BlockSpec((tm, LANES), lambda i: (i % ns, 0))
    outs = [(1024, BF16), (1024, BF16), (1024, BF16), (F_WIDTH, F32), (512, BF16), (256, BF16),
            (256, BF16), (3 * D_MODEL, BF16)]
    return pl.pallas_call(
        _in_proj_kernel,
        grid=(nt,),
        in_specs=[row(D_MODEL), full((1, D_MODEL)), full((D_MODEL, _W1_COLS)),
                  full((1, MLA_Q_LORA)), full((MLA_Q_LORA, 1024)), full((MLA_Q_LORA, 1024)),
                  full((1, MLA_KV_LORA)), full((MLA_KV_LORA, 2048)),
                  tab, tab, tab, tab],
        out_specs=[pl.BlockSpec((w, tm), lambda i: (0, i)) if n == 2 else row(w)
                   for n, (w, _) in enumerate(outs)],
        out_shape=[jax.ShapeDtypeStruct((w, T) if n == 2 else (T, w), dt)
                   for n, (w, dt) in enumerate(outs)],
        compiler_params=_cparams(("parallel",)),
        name="in_proj",
    )(x2, lw["norm_mix_g"], lw["w1"], lw["q_norm_g"], lw["w_uq"], lw["w_uq_rot"],
      lw["kv_norm_g"], lw["w_ukv"], tabs["cm"], tabs["sm"], tabs["cg"], tabs["sg"])


def _mla_attn_kernel(q_ref, k_ref, vt_ref, o_ref, s_ref, p_ref, *, tk):
    tq = q_ref.shape[0]
    nk = k_ref.shape[0] // tk
    q = q_ref[...]

    def k_chunk(c):
        return k_ref[c * tk:(c + 1) * tk, :]

    def vt_chunk(c):
        return vt_ref[:, c * tk:(c + 1) * tk]

    def step(slot, vt_prev, k_next, carry):
        m, alpha, acc = carry
        pv = _dot(vt_prev, p_ref[1 - slot])
        s_ref[1 - slot] = _dot_nt(k_next, q)
        s = s_ref[slot]
        m_new = jnp.maximum(m, jnp.max(s, axis=0, keepdims=True))
        p_ref[slot] = jnp.exp2(s - m_new).astype(BF16)
        return m_new, jnp.exp2(m - m_new), acc * alpha + pv

    s_ref[0] = _dot_nt(k_chunk(0), q)
    p_ref[1] = jnp.zeros(p_ref.shape[1:], BF16)
    carry = (jnp.full((1, tq), NEG, F32), jnp.ones((1, tq), F32), jnp.zeros((LANES, tq), F32))
    for c in range(nk):
        carry = step(c % 2, vt_chunk(max(c - 1, 0)), k_chunk(min(c + 1, nk - 1)), carry)
    _, alpha, acc = carry
    acc = acc * alpha + _dot(vt_chunk(nk - 1), p_ref[1])
    o_ref[...] = (acc / acc[MLA_V:MLA_V + 1, :]).T.astype(o_ref.dtype)


def _mla_attn(q, k, vt, B, S, tq, tk):
    assert (S // tk) % 2 == 0
    q3, k3 = (a.reshape(B, S, MLA_HEADS * LANES) for a in (q, k))
    out = pl.pallas_call(
        functools.partial(_mla_attn_kernel, tk=tk),
        scratch_shapes=[pltpu.VMEM((2, tk, tq), F32), pltpu.VMEM((2, tk, tq), BF16)],
        grid=(B, MLA_HEADS, S // tq),
        in_specs=[pl.BlockSpec((None, tq, LANES), lambda b, h, i: (b, i, h)),
                  pl.BlockSpec((None, S, LANES), lambda b, h, i: (b, 0, h)),
                  pl.BlockSpec((LANES, S), lambda b, h, i: (h, b))],
        out_specs=pl.BlockSpec((None, tq, LANES), lambda b, h, i: (b, i, h)),
        out_shape=jax.ShapeDtypeStruct((B, S, MLA_HEADS * LANES), BF16),
        compiler_params=_cparams(("parallel", "parallel", "parallel")),
        name="mla_attn",
    )(q3, k3, vt)
    return out.reshape(B * S, MLA_HEADS * LANES)


def _fft1_kernel(w_ref, x_ref, o_ref):
    o_ref[...] = jnp.dot(w_ref[...], x_ref[...], preferred_element_type=F32, precision=HIGHEST)


def _fft2_kernel(m_ref, a_ref, cc_ref, cs_ref, o_ref, *, scale):
    aa = jnp.concatenate([a_ref[0], a_ref[1]], axis=0)
    y = jnp.dot(m_ref[...], aa, preferred_element_type=F32, precision=HIGHEST)
    n2 = y.shape[0] // 2
    out = (jnp.dot(y[:n2], cc_ref[...], preferred_element_type=F32, precision=HIGHEST)
           + jnp.dot(y[n2:], cs_ref[...], preferred_element_type=F32, precision=HIGHEST))
    o_ref[...] = out * scale


def _fourier(xf, ft, B, S):
    n1, n2 = ft["n1"], ft["n2"]
    cols = n2 * F_WIDTH
    tc = min(cols, 4096)
    x3 = xf.reshape(B, n1, cols)
    a = pl.pallas_call(
        _fft1_kernel,
        grid=(B, cols // tc),
        in_specs=[pl.BlockSpec((2 * n1, n1), lambda b, j: (0, 0)),
                  pl.BlockSpec((None, n1, tc), lambda b, j: (b, 0, j))],
        out_specs=pl.BlockSpec((None, 2 * n1, tc), lambda b, j: (b, 0, j)),
        out_shape=jax.ShapeDtypeStruct((B, 2 * n1, cols), F32),
        compiler_params=_cparams(("parallel", "parallel")),
        name="fourier_stage1",
    )(ft["w1"], x3)
    a5 = a.reshape(B, 2, n1, n2, F_WIDTH)
    y = pl.pallas_call(
        functools.partial(_fft2_kernel, scale=1.0 / math.sqrt(S * F_GROUP_DIM)),
        grid=(B, n1),
        in_specs=[pl.BlockSpec((None, 2 * n2, 2 * n2), lambda b, k: (k, 0, 0)),
                  pl.BlockSpec((None, 2, None, n2, F_WIDTH), lambda b, k: (b, 0, k, 0, 0)),
                  pl.BlockSpec((F_WIDTH, F_WIDTH), lambda b, k: (0, 0)),
                  pl.BlockSpec((F_WIDTH, F_WIDTH), lambda b, k: (0, 0))],
        out_specs=pl.BlockSpec((None, None, n2, F_WIDTH), lambda b, k: (b, k, 0, 0)),
        out_shape=jax.ShapeDtypeStruct((B, n1, n2, F_WIDTH), F32),
        compiler_params=_cparams(("parallel", "parallel")),
        name="fourier_stage2",
    )(ft["m2"], a5, ft["cc"], ft["cs"])
    return jnp.transpose(y, (0, 2, 1, 3)).reshape(B * S, F_WIDTH)


def _gqa_kernel(sink_ref, q_ref, kp_ref, kc_ref, kn_ref, vp_ref, vc_ref, vn_ref, o_ref, *, S):
    n = pl.program_id(1)
    tq, halo = q_ref.shape[0], kp_ref.shape[0]
    width = tq + 2 * halo
    qpos = n * tq + lax.broadcasted_iota(jnp.int32, (tq, width), 0)
    kpos = n * tq - halo + lax.broadcasted_iota(jnp.int32, (tq, width), 1)
    mask = (jnp.abs(qpos - kpos) <= WINDOW) & (kpos >= 0) & (kpos < S)
    for hd in range(GQA_HEADS):
        kh = hd // GQA_GROUP
        qs = slice(hd * LANES, (hd + 1) * LANES)
        ks = slice(kh * LANES, (kh + 1) * LANES)
        q = q_ref[:, qs]
        s = jnp.concatenate([_dot_nt(q, kp_ref[:, ks]), _dot_nt(q, kc_ref[:, ks]),
                             _dot_nt(q, kn_ref[:, ks])], axis=-1)
        s = jnp.where(mask, s, NEG)
        sk = sink_ref[hd]
        m = jnp.maximum(jnp.max(s, axis=-1, keepdims=True), sk)
        p = jnp.exp(s - m)
        den = jnp.sum(p, axis=-1, keepdims=True) + jnp.exp(sk - m)
        pb = (p / den).astype(BF16)
        o = (_dot(pb[:, :halo], vp_ref[:, ks]) + _dot(pb[:, halo:halo + tq], vc_ref[:, ks])
             + _dot(pb[:, halo + tq:], vn_ref[:, ks]))
        o_ref[:, qs] = o.astype(o_ref.dtype)


def _gqa(qg, kg, vg, sink, B, S, tq):
    halo = WINDOW
    per = tq // halo
    nb = S // halo
    q3 = qg.reshape(B, S, GQA_HEADS * LANES)
    k3 = kg.reshape(B, S, GQA_KV_HEADS * LANES)
    v3 = vg.reshape(B, S, GQA_KV_HEADS * LANES)
    kvw = GQA_KV_HEADS * LANES
    prev = pl.BlockSpec((None, halo, kvw), lambda b, n: (b, jnp.maximum(n * per - 1, 0), 0))
    cur = pl.BlockSpec((None, tq, kvw), lambda b, n: (b, n, 0))
    nxt = pl.BlockSpec((None, halo, kvw), lambda b, n: (b, jnp.minimum((n + 1) * per, nb - 1), 0))
    out = pl.pallas_call(
        functools.partial(_gqa_kernel, S=S),
        grid=(B, S // tq),
        in_specs=[pl.BlockSpec(memory_space=pltpu.SMEM),
                  pl.BlockSpec((None, tq, GQA_HEADS * LANES), lambda b, n: (b, n, 0)),
                  prev, cur, nxt, prev, cur, nxt],
        out_specs=pl.BlockSpec((None, tq, GQA_HEADS * LANES), lambda b, n: (b, n, 0)),
        out_shape=jax.ShapeDtypeStruct((B, S, GQA_HEADS * LANES), BF16),
        compiler_params=_cparams(("parallel", "parallel")),
        name="gqa_window",
    )(sink, q3, k3, k3, k3, v3, v3, v3)
    return out.reshape(B * S, GQA_HEADS * LANES)


def _merge_kernel(x_ref, attn_ref, four_ref, og_ref, gate_ref, wo_ref, fw_ref, fb_ref, gwo_ref,
                  wout_ref, o_ref):
    mla_o = _dot(attn_ref[...], wo_ref[...])
    fnet_o = _dot(four_ref[...].astype(BF16), fw_ref[...]) + fb_ref[...]
    gqa_o = _dot(og_ref[...], gwo_ref[...])
    d = D_MODEL
    merged = (gate_ref[:, 0:d] * mla_o + gate_ref[:, d:2 * d] * fnet_o
              + gate_ref[:, 2 * d:3 * d] * gqa_o)
    o_ref[...] = x_ref[...] + _dot(merged.astype(BF16), wout_ref[...])


def _merge(x2, attn, four, og, gate, lw, tm):
    T = x2.shape[0]
    full = lambda shape: pl.BlockSpec(shape, lambda i: (0,) * len(shape))
    row = lambda w: pl.BlockSpec((tm, w), lambda i: (i, 0))
    return pl.pallas_call(
        _merge_kernel,
        grid=(T // tm,),
        in_specs=[row(D_MODEL), row(1024), row(F_WIDTH), row(512), row(3 * D_MODEL),
                  full((1024, D_MODEL)), full((F_WIDTH, D_MODEL)), full((1, D_MODEL)),
                  full((512, D_MODEL)), full((D_MODEL, D_MODEL))],
        out_specs=row(D_MODEL),
        out_shape=jax.ShapeDtypeStruct((T, D_MODEL), F32),
        compiler_params=_cparams(("parallel",)),
        name="merge",
    )(x2, attn, four, og, gate, lw["mla_w_o"], lw["fnet_w"], lw["fnet_b"], lw["gqa_w_o"],
      lw["w_out"])


LOG2E = math.log2(math.e)
PEER_BLK = 256
PEER_RANKS = PEER_TOPK + 1


def _sort_network(n):
    pairs = []

    def merge(lo, m, r):
        step = 2 * r
        if step < m:
            merge(lo, m, step)
            merge(lo + r, m, step)
            pairs.extend((i, i + r) for i in range(lo + r, lo + m - r, step))
        else:
            pairs.append((lo, lo + r))

    def sort(lo, m):
        if m > 1:
            sort(lo, m // 2)
            sort(lo + m // 2, m // 2)
            merge(lo, m, 1)

    sort(0, n)
    return pairs


def _top_of_sorted_lists(lists, k, extra=None):
    lists = list(lists)
    neg = jnp.full(lists[0].shape, -jnp.inf, F32)
    vals = []
    for r in range(k):
        mx = jnp.max(lists[0], axis=0, keepdims=True)
        if extra is not None:
            mx = jnp.maximum(mx, jnp.max(extra, axis=0, keepdims=True))
        vals.append(mx)
        if r == k - 1:
            break
        win = lists[0] == mx
        for v in range(min(len(lists), k - 1 - r)):
            lists[v] = jnp.where(win, lists[v + 1] if v + 1 < len(lists) else neg, lists[v])
        if extra is not None:
            extra = jnp.where(extra == mx, -jnp.inf, extra)
    return vals


def _top_sorted(s, k):
    lists = [s[8 * v:8 * v + 8] for v in range(s.shape[0] // 8)]
    for i, j in _sort_network(len(lists)):
        lists[i], lists[j] = jnp.maximum(lists[i], lists[j]), jnp.minimum(lists[i], lists[j])
    return _top_of_sorted_lists(lists, k)


def _peer_prep_kernel(x_ref, g_ref, wq_ref, keys_ref, hn_ref, a1_ref, th_ref, e2_ref):
    hn = _rms(x_ref[...], g_ref[...])
    hn_ref[...] = hn.T.astype(BF16)
    q = _dot(hn.astype(BF16), wq_ref[...]).astype(BF16)
    tm = q.shape[0]
    for hd in range(PEER_HEADS):
        sts, tops = [], []
        for p in range(2):
            hp = 2 * hd + p
            st = _dot_nt(keys_ref[hp], q[:, hp * PEER_HALF:(hp + 1) * PEER_HALF])
            sts.append(st)
            tops.append(_top_sorted(st, PEER_RANKS))
        a, b = tops
        a_lo = jnp.concatenate(a[:8], axis=0)
        a_hi = jnp.concatenate(a[8:] + [jnp.full((7, tm), -jnp.inf, F32)], axis=0)
        top = _top_of_sorted_lists([a_lo + b[r] for r in range(PEER_RANKS)], PEER_RANKS,
                                   extra=a_hi + b[0])
        m = top[0]
        log_z = jnp.log(sum(jnp.exp(t - m) for t in top[:PEER_TOPK]))
        thr = 0.5 * (top[PEER_TOPK - 1] + top[PEER_TOPK])
        st1 = (sts[0] - a[0]) * LOG2E
        s2k = (sts[1] - b[0] - log_z) * LOG2E
        a1_ref[hd] = jnp.exp2(st1)
        th_ref[hd] = jnp.exp2((thr - m - log_z) * LOG2E - st1 - 1.0)
        e2 = jnp.exp2(s2k - 1.0)
        for c in range(tm // LANES):
            e2_ref[hd, c] = e2[:, c * LANES:(c + 1) * LANES]


def _peer_prep(x2, lw, tm):
    T = x2.shape[0]
    sc = pl.BlockSpec((PEER_HEADS, PEER_N_KEYS, tm), lambda i: (0, 0, i))
    return pl.pallas_call(
        _peer_prep_kernel,
        grid=(T // tm,),
        in_specs=[pl.BlockSpec((tm, D_MODEL), lambda i: (i, 0)),
                  pl.BlockSpec((1, D_MODEL), lambda i: (0, 0)),
                  pl.BlockSpec((D_MODEL, 2 * PEER_HEADS * PEER_HALF), lambda i: (0, 0)),
                  pl.BlockSpec((2 * PEER_HEADS, PEER_N_KEYS, PEER_HALF), lambda i: (0, 0, 0))],
        out_specs=[pl.BlockSpec((D_MODEL, tm), lambda i: (0, i)), sc, sc,
                   pl.BlockSpec((PEER_HEADS, tm // LANES, PEER_N_KEYS, LANES),
                                lambda i: (0, i, 0, 0))],
        out_shape=[jax.ShapeDtypeStruct((D_MODEL, T), BF16)]
        + [jax.ShapeDtypeStruct((PEER_HEADS, PEER_N_KEYS, T), F32)] * 2
        + [jax.ShapeDtypeStruct((PEER_HEADS, T // LANES, PEER_N_KEYS, LANES), F32)],
        compiler_params=_cparams(("parallel",)),
        name="peer_prep",
    )(x2, lw["norm_ffn_g"], lw["peer_w_q"], lw["peer_keys"])


def _peer_main_kernel(x_ref, hn_ref, a1_ref, th_ref, e2_ref, u_ref, vt_ref, fg_ref, o_ref,
                      acc_ref, act_ref, w_ref, *, n_i, final_norm):
    e = pl.program_id(1)
    tm = hn_ref.shape[1]

    @pl.when(e == 0)
    def _():
        acc_ref[...] = jnp.zeros_like(acc_ref)

    n_blk = n_i * PEER_N_KEYS // PEER_BLK
    per_blk = PEER_BLK // PEER_N_KEYS

    n_strip = tm // LANES

    def store_activations(p, slot):
        a = _dot(u_ref[p * PEER_BLK:(p + 1) * PEER_BLK, :], hn_ref[...])
        act = a * (1.0 + lax.erf(a * (1.0 / math.sqrt(2.0))))
        for c in range(n_strip):
            act_ref[slot, c] = act[:, c * LANES:(c + 1) * LANES]

    def fold(p, slot):
        w = jnp.concatenate([w_ref[slot, c] for c in range(n_strip)], axis=1)
        acc_ref[...] += _dot(vt_ref[p], w)

    store_activations(0, 0)
    for p in range(n_blk):
        slot = p % 2
        if p + 1 < n_blk:
            store_activations(p + 1, 1 - slot)
        if p > 0:
            fold(p - 1, 1 - slot)
        for ii in range(per_blk):
            i = p * per_blk + ii
            rows = slice(ii * PEER_N_KEYS, (ii + 1) * PEER_N_KEYS)
            for c in range(n_strip):
                ls = slice(c * LANES, (c + 1) * LANES)
                g = None
                for hd in range(PEER_HEADS):
                    e2 = e2_ref[hd, c]
                    t = jnp.where(e2 > th_ref[hd, i:i + 1, ls], e2, 0.0) * a1_ref[hd, i:i + 1, ls]
                    g = t if g is None else g + t
                w_ref[slot, c, rows, :] = (g * act_ref[slot, c, rows, :]).astype(BF16)
    fold(n_blk - 1, (n_blk - 1) % 2)

    @pl.when(e == pl.num_programs(1) - 1)
    def _():
        y = x_ref[...] + acc_ref[...].T
        if final_norm:
            y = _rms(y, fg_ref[...])
        o_ref[...] = y


def _peer_main(x2, hn, a1, th, e2, lw, final_g, tm, n_i, final_norm):
    T = x2.shape[0]
    et = n_i * PEER_N_KEYS
    sc = pl.BlockSpec((PEER_HEADS, PEER_N_KEYS, tm), lambda t, e: (0, 0, t))
    return pl.pallas_call(
        functools.partial(_peer_main_kernel, n_i=n_i, final_norm=final_norm),
        grid=(T // tm, PEER_N_EXPERTS // et),
        in_specs=[pl.BlockSpec((tm, D_MODEL), lambda t, e: (t, 0)),
                  pl.BlockSpec((D_MODEL, tm), lambda t, e: (0, t)),
                  pl.BlockSpec((PEER_HEADS, n_i, tm), lambda t, e: (0, e, t)),
                  pl.BlockSpec((PEER_HEADS, n_i, tm), lambda t, e: (0, e, t)),
                  pl.BlockSpec((PEER_HEADS, tm // LANES, PEER_N_KEYS, LANES),
                               lambda t, e: (0, t, 0, 0)),
                  pl.BlockSpec((et, D_MODEL), lambda t, e: (e, 0)),
                  pl.BlockSpec((et // PEER_BLK, D_MODEL, PEER_BLK), lambda t, e: (e, 0, 0)),
                  pl.BlockSpec((1, D_MODEL), lambda t, e: (0, 0))],
        out_specs=pl.BlockSpec((tm, D_MODEL), lambda t, e: (t, 0)),
        out_shape=jax.ShapeDtypeStruct((T, D_MODEL), F32),
        scratch_shapes=[pltpu.VMEM((D_MODEL, tm), F32),
                        pltpu.VMEM((2, tm // LANES, PEER_BLK, LANES), F32),
                        pltpu.VMEM((2, tm // LANES, PEER_BLK, LANES), BF16)],
        compiler_params=_cparams(("parallel", "arbitrary")),
        name="peer_main",
    )(x2, hn, a1, th, e2, lw["peer_u"], lw["peer_vt"], final_g)


def _rot_half_cols(w):
    d = w.shape[-1]
    return jnp.concatenate([-w[..., d // 2:], w[..., :d // 2]], axis=-1)


def _pad_cols(w, left, total):
    return jnp.pad(w, ((0, 0), (left, total - left - w.shape[-1])))


def _head_groups(w, n_heads, width, left=0):
    return jnp.concatenate(
        [_pad_cols(w[:, h * width:(h + 1) * width], left, LANES) for h in range(n_heads)], axis=-1)


def _layer_weights(p, l):
    w_in = p["w_in"][l]
    widths = (MLA_Q_LORA, MLA_KV_LORA, MLA_ROPE, F_WIDTH, GQA_HEADS * GQA_HEAD_DIM,
              GQA_KV_HEADS * GQA_HEAD_DIM, GQA_KV_HEADS * GQA_HEAD_DIM, 3 * D_MODEL)
    offs = np.cumsum((0,) + widths)
    wcq, wckv, wkr, wxf, wgq, wgk, wgv, wgate = (w_in[:, offs[i]:offs[i + 1]] for i in range(8))

    def rot_heads(w, n_heads, width):
        return jnp.concatenate(
            [_rot_half_cols(w[:, h * width:(h + 1) * width]) for h in range(n_heads)], axis=-1)

    w1 = jnp.concatenate([
        wcq, wckv,
        _pad_cols(wkr, MLA_NOPE, LANES), _pad_cols(_rot_half_cols(wkr), MLA_NOPE, LANES),
        wxf,
        _head_groups(wgq, GQA_HEADS, GQA_HEAD_DIM),
        _head_groups(rot_heads(wgq, GQA_HEADS, GQA_HEAD_DIM), GQA_HEADS, GQA_HEAD_DIM),
        _head_groups(wgk, GQA_KV_HEADS, GQA_HEAD_DIM),
        _head_groups(rot_heads(wgk, GQA_KV_HEADS, GQA_HEAD_DIM), GQA_KV_HEADS, GQA_HEAD_DIM),
        _head_groups(wgv, GQA_KV_HEADS, GQA_HEAD_DIM),
        wgate], axis=-1).astype(BF16)
    assert w1.shape[-1] == _W1_COLS

    w_uq = p["mla_w_uq"][l]
    qd = MLA_NOPE + MLA_ROPE
    uq, uqr = [], []
    for h in range(MLA_HEADS):
        blk = w_uq[:, h * qd:(h + 1) * qd]
        uq.append(_pad_cols(blk, 0, LANES))
        uqr.append(_pad_cols(_rot_half_cols(blk[:, MLA_NOPE:]), MLA_NOPE, LANES))
    w_ukv = p["mla_w_ukv"][l]
    kd = MLA_NOPE + MLA_V
    uk = [_pad_cols(w_ukv[:, h * kd:h * kd + MLA_NOPE], 0, LANES) for h in range(MLA_HEADS)]
    uv = [_pad_cols(w_ukv[:, h * kd + MLA_NOPE:(h + 1) * kd], 0, LANES) for h in range(MLA_HEADS)]

    def pad_rows(w, n_heads, width):
        return jnp.concatenate(
            [jnp.pad(w[h * width:(h + 1) * width], ((0, LANES - width), (0, 0)))
             for h in range(n_heads)], axis=0)

    return {
        "norm_mix_g": p["norm_mix_g"][l][None, :],
        "w1": w1,
        "q_norm_g": p["mla_q_norm_g"][l][None, :],
        "w_uq": jnp.concatenate(uq, axis=-1).astype(BF16),
        "w_uq_rot": jnp.concatenate(uqr, axis=-1).astype(BF16),
        "kv_norm_g": p["mla_kv_norm_g"][l][None, :],
        "w_ukv": jnp.concatenate(uk + uv, axis=-1).astype(BF16),
        "mla_w_o": pad_rows(p["mla_w_o"][l], MLA_HEADS, MLA_V).astype(BF16),
        "fnet_w": p["fnet_w"][l].astype(BF16),
        "fnet_b": p["fnet_b"][l][None, :],
        "gqa_sink": p["gqa_sink"][l],
        "gqa_w_o": pad_rows(p["gqa_w_o"][l], GQA_HEADS, GQA_HEAD_DIM).astype(BF16),
        "w_out": p["w_out"][l].astype(BF16),
        "norm_ffn_g": p["norm_ffn_g"][l][None, :],
        "peer_w_q": p["peer_w_q"][l].astype(BF16),
        "peer_keys": p["peer_keys"][l].reshape(2 * PEER_HEADS, PEER_N_KEYS, PEER_HALF).astype(BF16),
        "peer_u": p["peer_u"][l].astype(BF16),
        "peer_vt": p["peer_v"][l].astype(BF16).reshape(
            PEER_N_EXPERTS // PEER_BLK, PEER_BLK, D_MODEL).transpose(0, 2, 1),
    }


def _rope_tables(S):
    pos = jnp.arange(S, dtype=F32)[:, None]

    def cs(d):
        inv = 1.0 / (ROPE_THETA ** (jnp.arange(0, d, 2, dtype=F32) / d))
        ang = pos * inv[None, :]
        return (jnp.concatenate([jnp.cos(ang)] * 2, axis=-1),
                jnp.concatenate([jnp.sin(ang)] * 2, axis=-1))

    c, s = cs(MLA_ROPE)
    rest = LANES - MLA_NOPE - MLA_ROPE
    cm = jnp.concatenate([jnp.ones((S, MLA_NOPE), F32), c, jnp.zeros((S, rest), F32)], axis=-1)
    sm = jnp.concatenate([jnp.zeros((S, MLA_NOPE), F32), s, jnp.zeros((S, rest), F32)], axis=-1)
    c, s = cs(GQA_HEAD_DIM)
    zero = jnp.zeros((S, LANES - GQA_HEAD_DIM), F32)
    return {"cm": cm, "sm": sm, "cg": jnp.concatenate([c, zero], axis=-1),
            "sg": jnp.concatenate([s, zero], axis=-1)}


def _fourier_tables(S):
    n2 = 128
    n1 = S // n2

    def cos_sin(num, den):
        ang = (2.0 * math.pi / den) * (num % den).astype(F32)
        return jnp.cos(ang), jnp.sin(ang)

    i1 = jnp.arange(n1, dtype=jnp.int32)
    c1, s1 = cos_sin(i1[:, None] * i1[None, :], n1)
    w1 = jnp.concatenate([c1, -s1], axis=0)
    i2 = jnp.arange(n2, dtype=jnp.int32)
    kk = i1[:, None, None] + n1 * i2[None, :, None]
    mr, ms = cos_sin(kk * i2[None, None, :], S)
    m2 = jnp.concatenate([jnp.concatenate([mr, ms], axis=-1),
                          jnp.concatenate([-ms, mr], axis=-1)], axis=1)
    ic = jnp.arange(F_GROUP_DIM, dtype=jnp.int32)
    cc, cs = cos_sin(ic[:, None] * ic[None, :], F_GROUP_DIM)
    eye = jnp.eye(F_GROUPS, dtype=F32)
    return {"n1": n1, "n2": n2, "w1": w1, "m2": m2, "cc": jnp.kron(eye, cc),
            "cs": jnp.kron(eye, cs)}


def _pick(S, pref):
    t = pref
    while S % t:
        t //= 2
    return t


def _trunk(x, layers, final_g, rope_t, four_t):
    B, S, _ = x.shape
    T = B * S
    x2 = x.reshape(T, D_MODEL)
    tm = _pick(S, 256)
    n_layers = len(layers)
    for l, lw in enumerate(layers):
        q, k, v, xf, qg, kg, vg, gate = _in_proj(x2, lw, rope_t, S, tm)
        attn = _mla_attn(q, k, v, B, S, _pick(S, 1024), _pick(S // 2, 512))
        four = _fourier(xf, four_t, B, S)
        og = _gqa(qg, kg, vg, lw["gqa_sink"], B, S, _pick(S, 512))
        x2 = _merge(x2, attn, four, og, gate, lw, tm)
        hn, a1, th, e2 = _peer_prep(x2, lw, _pick(T, 256))
        x2 = _peer_main(x2, hn, a1, th, e2, lw, final_g, _pick(T, 512), 16,
                        final_norm=(l == n_layers - 1))
    return x2.reshape(B, S, D_MODEL)


def kernel(x_prompt, x_sample, norm_mix_g, w_in, mla_q_norm_g, mla_w_uq, mla_kv_norm_g, mla_w_ukv, mla_w_o, fnet_w, fnet_b, gqa_sink, gqa_w_o, w_out, norm_ffn_g, peer_w_q, peer_keys, peer_u, peer_v, final_norm_g):
    p = dict(norm_mix_g=norm_mix_g, w_in=w_in, mla_q_norm_g=mla_q_norm_g, mla_w_uq=mla_w_uq,
             mla_kv_norm_g=mla_kv_norm_g, mla_w_ukv=mla_w_ukv, mla_w_o=mla_w_o, fnet_w=fnet_w,
             fnet_b=fnet_b, gqa_sink=gqa_sink, gqa_w_o=gqa_w_o, w_out=w_out,
             norm_ffn_g=norm_ffn_g, peer_w_q=peer_w_q, peer_keys=peer_keys, peer_u=peer_u,
             peer_v=peer_v)
    layers = [_layer_weights(p, l) for l in range(w_in.shape[0])]
    final_g = final_norm_g[None, :]
    outs = []
    for x in (x_prompt, x_sample):
        S = x.shape[1]
        outs.append(_trunk(x, layers, final_g, _rope_tables(S), _fourier_tables(S)))
    return tuple(outs)
```

```python
import functools
import math

import jax
import jax.numpy as jnp
import numpy as np
from jax import lax
from jax.experimental import pallas as pl
from jax.experimental.pallas import tpu as pltpu

D_MODEL = 1024
EPS = 1e-6
ROPE_THETA = 10000.0
NEG = -1e30
LANES = 128

MLA_HEADS = 8
MLA_NOPE = 64
MLA_ROPE = 32
MLA_V = 64
MLA_Q_LORA = 384
MLA_KV_LORA = 256
F_GROUPS = 4
F_GROUP_DIM = 64
F_WIDTH = F_GROUPS * F_GROUP_DIM
GQA_HEADS = 4
GQA_KV_HEADS = 2
GQA_GROUP = GQA_HEADS // GQA_KV_HEADS
GQA_HEAD_DIM = 64
WINDOW = 128
PEER_HEADS = 8
PEER_N_KEYS = 128
PEER_N_EXPERTS = PEER_N_KEYS * PEER_N_KEYS
PEER_HALF = 128
PEER_TOPK = 16

VMEM_LIMIT = 56 * 1024 * 1024

BF16 = jnp.bfloat16
F32 = jnp.float32
HIGHEST = lax.Precision.HIGHEST


def _cparams(sem):
    return pltpu.CompilerParams(dimension_semantics=sem, vmem_limit_bytes=VMEM_LIMIT)


def _rms(x, g):
    return x * lax.rsqrt(jnp.mean(x * x, axis=-1, keepdims=True) + EPS) * g


def _dot(a, b):
    return jnp.dot(a, b, preferred_element_type=F32)


def _dot_nt(a, b):
    return lax.dot_general(a, b, (((1,), (1,)), ((), ())), preferred_element_type=F32)


_C_CQ = (0, 384)
_C_CKV = (384, 640)
_C_KR = (640, 768)
_C_KRR = (768, 896)
_C_XF = (896, 1152)
_C_GQ = (1152, 1664)
_C_GQR = (1664, 2176)
_C_GK = (2176, 2432)
_C_GKR = (2432, 2688)
_C_GV = (2688, 2944)
_C_GATE = (2944, 6016)
_W1_COLS = 6016


def _in_proj_kernel(x_ref, g_ref, w1_ref, qng_ref, wuq_ref, wuqr_ref, kvng_ref, wukv_ref,
                    cm_ref, sm_ref, cg_ref, sg_ref,
                    q_ref, k_ref, v_ref, xf_ref, qg_ref, kg_ref, vg_ref, gate_ref):
    h = _rms(x_ref[...], g_ref[...]).astype(BF16)

    def proj(c):
        return _dot(h, w1_ref[:, c[0]:c[1]])

    cm, sm = cm_ref[...], sm_ref[...]
    cg, sg = cg_ref[...], sg_ref[...]

    cqn = _rms(proj(_C_CQ), qng_ref[...]).astype(BF16)
    q = _dot(cqn, wuq_ref[...])
    qr = _dot(cqn, wuqr_ref[...])
    mla_scale = (MLA_NOPE + MLA_ROPE) ** -0.5 * math.log2(math.e)
    for hd in range(MLA_HEADS):
        sl = slice(hd * LANES, (hd + 1) * LANES)
        q_ref[:, sl] = ((q[:, sl] * cm + qr[:, sl] * sm) * mla_scale).astype(BF16)

    ckvn = _rms(proj(_C_CKV), kvng_ref[...]).astype(BF16)
    kv = _dot(ckvn, wukv_ref[...])
    k_rope = proj(_C_KR) * cm + proj(_C_KRR) * sm
    for hd in range(MLA_HEADS):
        sl = slice(hd * LANES, (hd + 1) * LANES)
        k_ref[:, sl] = (kv[:, sl] + k_rope).astype(BF16)
    lane = lax.broadcasted_iota(jnp.int32, (1, MLA_HEADS * LANES), 1)
    ones_cols = (lane % LANES == MLA_V).astype(F32)
    v_ref[...] = (kv[:, MLA_HEADS * LANES:] + ones_cols).T.astype(BF16)

    xf_ref[...] = proj(_C_XF)

    gq, gqr = proj(_C_GQ), proj(_C_GQR)
    gqa_scale = GQA_HEAD_DIM ** -0.5
    for hd in range(GQA_HEADS):
        sl = slice(hd * LANES, (hd + 1) * LANES)
        qg_ref[:, sl] = ((gq[:, sl] * cg + gqr[:, sl] * sg) * gqa_scale).astype(BF16)
    gk, gkr = proj(_C_GK), proj(_C_GKR)
    for hd in range(GQA_KV_HEADS):
        sl = slice(hd * LANES, (hd + 1) * LANES)
        kg_ref[:, sl] = (gk[:, sl] * cg + gkr[:, sl] * sg).astype(BF16)
    vg_ref[...] = proj(_C_GV).astype(BF16)

    gate = proj(_C_GATE)
    gate_ref[...] = (1.0 / (1.0 + jnp.exp(-gate))).astype(gate_ref.dtype)


def _in_proj(x2, lw, tabs, S, tm):
    T = x2.shape[0]
    nt = T // tm
    ns = S // tm
    full = lambda shape: pl.BlockSpec(shape, lambda i: (0,) * len(shape))
    row = lambda w: pl.BlockSpec((tm, w), lambda i: (i, 0))
    tab = pl.BlockSpec((tm, LANES), lambda i: (i % ns, 0))
    outs = [(1024, BF16), (1024, BF16), (1024, BF16), (F_WIDTH, F32), (512, BF16), (256, BF16),
            (256, BF16), (3 * D_MODEL, BF16)]
    return pl.pallas_call(
        _in_proj_kernel,
        grid=(nt,),
        in_specs=[row(D_MODEL), full((1, D_MODEL)), full((D_MODEL, _W1_COLS)),
                  full((1, MLA_Q_LORA)), full((MLA_Q_LORA, 1024)), full((MLA_Q_LORA, 1024)),
                  full((1, MLA_KV_LORA)), full((MLA_KV_LORA, 2048)),
                  tab, tab, tab, tab],
        out_specs=[pl.BlockSpec((w, tm), lambda i: (0, i)) if n == 2 else row(w)
                   for n, (w, _) in enumerate(outs)],
        out_shape=[jax.ShapeDtypeStruct((w, T) if n == 2 else (T, w), dt)
                   for n, (w, dt) in enumerate(outs)],
        compiler_params=_cparams(("parallel",)),
        name="in_proj",
    )(x2, lw["norm_mix_g"], lw["w1"], lw["q_norm_g"], lw["w_uq"], lw["w_uq_rot"],
      lw["kv_norm_g"], lw["w_ukv"], tabs["cm"], tabs["sm"], tabs["cg"], tabs["sg"])


def _mla_attn_kernel(q_ref, k_ref, vt_ref, o_ref, s_ref, p_ref, *, tk):
    tq = q_ref.shape[0]
    nk = k_ref.shape[0] // tk
    q = q_ref[...]

    def k_chunk(c):
        return k_ref[c * tk:(c + 1) * tk, :]

    def vt_chunk(c):
        return vt_ref[:, c * tk:(c + 1) * tk]

    def step(slot, vt_prev, k_next, carry):
        m, alpha, acc = carry
        pv = _dot(vt_prev, p_ref[1 - slot])
        s_ref[1 - slot] = _dot_nt(k_next, q)
        s = s_ref[slot]
        m_new = jnp.maximum(m, jnp.max(s, axis=0, keepdims=True))
        p_ref[slot] = jnp.exp2(s - m_new).astype(BF16)
        return m_new, jnp.exp2(m - m_new), acc * alpha + pv

    s_ref[0] = _dot_nt(k_chunk(0), q)
    p_ref[1] = jnp.zeros(p_ref.shape[1:], BF16)
    carry = (jnp.full((1, tq), NEG, F32), jnp.ones((1, tq), F32), jnp.zeros((LANES, tq), F32))
    for c in range(nk):
        carry = step(c % 2, vt_chunk(max(c - 1, 0)), k_chunk(min(c + 1, nk - 1)), carry)
    _, alpha, acc = carry
    acc = acc * alpha + _dot(vt_chunk(nk - 1), p_ref[1])
    o_ref[...] = (acc / acc[MLA_V:MLA_V + 1, :]).T.astype(o_ref.dtype)


def _mla_attn(q, k, vt, B, S, tq, tk):
    assert (S // tk) % 2 == 0
    q3, k3 = (a.reshape(B, S, MLA_HEADS * LANES) for a in (q, k))
    out = pl.pallas_call(
        functools.partial(_mla_attn_kernel, tk=tk),
        scratch_shapes=[pltpu.VMEM((2, tk, tq), F32), pltpu.VMEM((2, tk, tq), BF16)],
        grid=(B, MLA_HEADS, S // tq),
        in_specs=[pl.BlockSpec((None, tq, LANES), lambda b, h, i: (b, i, h)),
                  pl.BlockSpec((None, S, LANES), lambda b, h, i: (b, 0, h)),
                  pl.BlockSpec((LANES, S), lambda b, h, i: (h, b))],
        out_specs=pl.BlockSpec((None, tq, LANES), lambda b, h, i: (b, i, h)),
        out_shape=jax.ShapeDtypeStruct((B, S, MLA_HEADS * LANES), BF16),
        compiler_params=_cparams(("parallel", "parallel", "parallel")),
        name="mla_attn",
    )(q3, k3, vt)
    return out.reshape(B * S, MLA_HEADS * LANES)


def _split(x):
    hi = x.astype(BF16)
    return hi, (x - hi.astype(F32)).astype(BF16)


def _dot3(a, b):
    (ah, al), (bh, bl) = a, b
    return _dot(ah, bh) + (_dot(ah, bl) + _dot(al, bh))


def _fft1_kernel(wh_ref, wl_ref, x_ref, o_ref):
    o_ref[...] = _dot3((wh_ref[...], wl_ref[...]), _split(x_ref[...]))


def _fft2_kernel(mh_ref, ml_ref, a_ref, cch_ref, ccl_ref, csh_ref, csl_ref, o_ref, *, scale):
    cc = (cch_ref[...], ccl_ref[...])
    cs = (csh_ref[...], csl_ref[...])
    for j in range(o_ref.shape[0]):
        aa = jnp.concatenate([a_ref[0, j], a_ref[1, j]], axis=0)
        y = _dot3((mh_ref[j], ml_ref[j]), _split(aa))
        n2 = y.shape[0] // 2
        o_ref[j] = (_dot3(_split(y[:n2]), cc) + _dot3(_split(y[n2:]), cs)) * scale


def _fourier(xf, ft, B, S):
    n1, n2 = ft["n1"], ft["n2"]
    cols = n2 * F_WIDTH
    tc = min(cols, 4096)
    x3 = xf.reshape(B, n1, cols)
    a = pl.pallas_call(
        _fft1_kernel,
        grid=(B, cols // tc),
        in_specs=[pl.BlockSpec((2 * n1, n1), lambda b, j: (0, 0)),
                  pl.BlockSpec((2 * n1, n1), lambda b, j: (0, 0)),
                  pl.BlockSpec((None, n1, tc), lambda b, j: (b, 0, j))],
        out_specs=pl.BlockSpec((None, 2 * n1, tc), lambda b, j: (b, 0, j)),
        out_shape=jax.ShapeDtypeStruct((B, 2 * n1, cols), F32),
        compiler_params=_cparams(("parallel", "parallel")),
        name="fourier_stage1",
    )(*ft["w1"], x3)
    a5 = a.reshape(B, 2, n1, n2, F_WIDTH)
    kb = math.gcd(4, n1)
    table = pl.BlockSpec((F_WIDTH, F_WIDTH), lambda b, k: (0, 0))
    twiddle = pl.BlockSpec((kb, 2 * n2, 2 * n2), lambda b, k: (k, 0, 0))
    y = pl.pallas_call(
        functools.partial(_fft2_kernel, scale=1.0 / math.sqrt(S * F_GROUP_DIM)),
        grid=(B, n1 // kb),
        in_specs=[twiddle, twiddle,
                  pl.BlockSpec((None, 2, kb, n2, F_WIDTH), lambda b, k: (b, 0, k, 0, 0)),
                  table, table, table, table],
        out_specs=pl.BlockSpec((None, kb, n2, F_WIDTH), lambda b, k: (b, k, 0, 0)),
        out_shape=jax.ShapeDtypeStruct((B, n1, n2, F_WIDTH), F32),
        compiler_params=_cparams(("parallel", "parallel")),
        name="fourier_stage2",
    )(*ft["m2"], a5, *ft["cc"], *ft["cs"])
    return jnp.transpose(y, (0, 2, 1, 3)).reshape(B * S, F_WIDTH)


def _gqa_kernel(sink_ref, q_ref, kp_ref, kc_ref, kn_ref, vp_ref, vc_ref, vn_ref, o_ref, *, S):
    n = pl.program_id(1)
    tq, halo = q_ref.shape[0], kp_ref.shape[0]
    width = tq + 2 * halo
    qpos = n * tq + lax.broadcasted_iota(jnp.int32, (tq, width), 0)
    kpos = n * tq - halo + lax.broadcasted_iota(jnp.int32, (tq, width), 1)
    mask = (jnp.abs(qpos - kpos) <= WINDOW) & (kpos >= 0) & (kpos < S)
    for hd in range(GQA_HEADS):
        kh = hd // GQA_GROUP
        qs = slice(hd * LANES, (hd + 1) * LANES)
        ks = slice(kh * LANES, (kh + 1) * LANES)
        q = q_ref[:, qs]
        s = jnp.concatenate([_dot_nt(q, kp_ref[:, ks]), _dot_nt(q, kc_ref[:, ks]),
                             _dot_nt(q, kn_ref[:, ks])], axis=-1)
        s = jnp.where(mask, s, NEG)
        sk = sink_ref[hd]
        m = jnp.maximum(jnp.max(s, axis=-1, keepdims=True), sk)
        p = jnp.exp(s - m)
        den = jnp.sum(p, axis=-1, keepdims=True) + jnp.exp(sk - m)
        pb = (p / den).astype(BF16)
        o = (_dot(pb[:, :halo], vp_ref[:, ks]) + _dot(pb[:, halo:halo + tq], vc_ref[:, ks])
             + _dot(pb[:, halo + tq:], vn_ref[:, ks]))
        o_ref[:, qs] = o.astype(o_ref.dtype)


def _gqa(qg, kg, vg, sink, B, S, tq):
    halo = WINDOW
    per = tq // halo
    nb = S // halo
    q3 = qg.reshape(B, S, GQA_HEADS * LANES)
    k3 = kg.reshape(B, S, GQA_KV_HEADS * LANES)
    v3 = vg.reshape(B, S, GQA_KV_HEADS * LANES)
    kvw = GQA_KV_HEADS * LANES
    prev = pl.BlockSpec((None, halo, kvw), lambda b, n: (b, jnp.maximum(n * per - 1, 0), 0))
    cur = pl.BlockSpec((None, tq, kvw), lambda b, n: (b, n, 0))
    nxt = pl.BlockSpec((None, halo, kvw), lambda b, n: (b, jnp.minimum((n + 1) * per, nb - 1), 0))
    out = pl.pallas_call(
        functools.partial(_gqa_kernel, S=S),
        grid=(B, S // tq),
        in_specs=[pl.BlockSpec(memory_space=pltpu.SMEM),
                  pl.BlockSpec((None, tq, GQA_HEADS * LANES), lambda b, n: (b, n, 0)),
                  prev, cur, nxt, prev, cur, nxt],
        out_specs=pl.BlockSpec((None, tq, GQA_HEADS * LANES), lambda b, n: (b, n, 0)),
        out_shape=jax.ShapeDtypeStruct((B, S, GQA_HEADS * LANES), BF16),
        compiler_params=_cparams(("parallel", "parallel")),
        name="gqa_window",
    )(sink, q3, k3, k3, k3, v3, v3, v3)
    return out.reshape(B * S, GQA_HEADS * LANES)


def _merge_kernel(x_ref, attn_ref, four_ref, og_ref, gate_ref, wo_ref, fw_ref, fb_ref, gwo_ref,
                  wout_ref, o_ref):
    mla_o = _dot(attn_ref[...], wo_ref[...])
    fnet_o = _dot(four_ref[...].astype(BF16), fw_ref[...]) + fb_ref[...]
    gqa_o = _dot(og_ref[...], gwo_ref[...])
    d = D_MODEL
    merged = (gate_ref[:, 0:d] * mla_o + gate_ref[:, d:2 * d] * fnet_o
              + gate_ref[:, 2 * d:3 * d] * gqa_o)
    o_ref[...] = x_ref[...] + _dot(merged.astype(BF16), wout_ref[...])


def _merge(x2, attn, four, og, gate, lw, tm):
    T = x2.shape[0]
    full = lambda shape: pl.BlockSpec(shape, lambda i: (0,) * len(shape))
    row = lambda w: pl.BlockSpec((tm, w), lambda i: (i, 0))
    return pl.pallas_call(
        _merge_kernel,
        grid=(T // tm,),
        in_specs=[row(D_MODEL), row(1024), row(F_WIDTH), row(512), row(3 * D_MODEL),
                  full((1024, D_MODEL)), full((F_WIDTH, D_MODEL)), full((1, D_MODEL)),
                  full((512, D_MODEL)), full((D_MODEL, D_MODEL))],
        out_specs=row(D_MODEL),
        out_shape=jax.ShapeDtypeStruct((T, D_MODEL), F32),
        compiler_params=_cparams(("parallel",)),
        name="merge",
    )(x2, attn, four, og, gate, lw["mla_w_o"], lw["fnet_w"], lw["fnet_b"], lw["gqa_w_o"],
      lw["w_out"])


LOG2E = math.log2(math.e)
PEER_BLK = 256
PEER_RANKS = PEER_TOPK + 1


def _sort_network(n):
    pairs = []

    def merge(lo, m, r):
        step = 2 * r
        if step < m:
            merge(lo, m, step)
            merge(lo + r, m, step)
            pairs.extend((i, i + r) for i in range(lo + r, lo + m - r, step))
        else:
            pairs.append((lo, lo + r))

    def sort(lo, m):
        if m > 1:
            sort(lo, m // 2)
            sort(lo + m // 2, m // 2)
            merge(lo, m, 1)

    sort(0, n)
    return pairs


def _top_of_sorted_lists(lists, k, extra=None):
    lists = list(lists)
    neg = jnp.full(lists[0].shape, -jnp.inf, F32)
    vals = []
    for r in range(k):
        mx = jnp.max(lists[0], axis=0, keepdims=True)
        if extra is not None:
            mx = jnp.maximum(mx, jnp.max(extra, axis=0, keepdims=True))
        vals.append(mx)
        if r == k - 1:
            break
        win = lists[0] == mx
        for v in range(min(len(lists), k - 1 - r)):
            lists[v] = jnp.where(win, lists[v + 1] if v + 1 < len(lists) else neg, lists[v])
        if extra is not None:
            extra = jnp.where(extra == mx, -jnp.inf, extra)
    return vals


def _top_sorted(s, k):
    lists = [s[8 * v:8 * v + 8] for v in range(s.shape[0] // 8)]
    for i, j in _sort_network(len(lists)):
        lists[i], lists[j] = jnp.maximum(lists[i], lists[j]), jnp.minimum(lists[i], lists[j])
    return _top_of_sorted_lists(lists, k)


def _peer_prep_kernel(x_ref, g_ref, wq_ref, keys_ref, hn_ref, a1_ref, th_ref, e2_ref):
    hn = _rms(x_ref[...], g_ref[...])
    hn_ref[...] = hn.T.astype(BF16)
    q = _dot(hn.astype(BF16), wq_ref[...]).astype(BF16)
    tm = q.shape[0]
    for hd in range(PEER_HEADS):
        sts, tops = [], []
        for p in range(2):
            hp = 2 * hd + p
            st = _dot_nt(keys_ref[hp], q[:, hp * PEER_HALF:(hp + 1) * PEER_HALF])
            sts.append(st)
            tops.append(_top_sorted(st, PEER_RANKS))
        a, b = tops
        a_lo = jnp.concatenate(a[:8], axis=0)
        a_hi = jnp.concatenate(a[8:] + [jnp.full((7, tm), -jnp.inf, F32)], axis=0)
        top = _top_of_sorted_lists([a_lo + b[r] for r in range(PEER_RANKS)], PEER_RANKS,
                                   extra=a_hi + b[0])
        m = top[0]
        log_z = jnp.log(sum(jnp.exp(t - m) for t in top[:PEER_TOPK]))
        thr = 0.5 * (top[PEER_TOPK - 1] + top[PEER_TOPK])
        st1 = (sts[0] - a[0]) * LOG2E
        s2k = (sts[1] - b[0] - log_z) * LOG2E
        a1_ref[hd] = jnp.exp2(st1)
        th_ref[hd] = jnp.exp2((thr - m - log_z) * LOG2E - st1 - 1.0)
        e2 = jnp.exp2(s2k - 1.0)
        for c in range(tm // LANES):
            e2_ref[hd, c] = e2[:, c * LANES:(c + 1) * LANES]


def _peer_prep(x2, lw, tm):
    T = x2.shape[0]
    sc = pl.BlockSpec((PEER_HEADS, PEER_N_KEYS, tm), lambda i: (0, 0, i))
    return pl.pallas_call(
        _peer_prep_kernel,
        grid=(T // tm,),
        in_specs=[pl.BlockSpec((tm, D_MODEL), lambda i: (i, 0)),
                  pl.BlockSpec((1, D_MODEL), lambda i: (0, 0)),
                  pl.BlockSpec((D_MODEL, 2 * PEER_HEADS * PEER_HALF), lambda i: (0, 0)),
                  pl.BlockSpec((2 * PEER_HEADS, PEER_N_KEYS, PEER_HALF), lambda i: (0, 0, 0))],
        out_specs=[pl.BlockSpec((D_MODEL, tm), lambda i: (0, i)), sc, sc,
                   pl.BlockSpec((PEER_HEADS, tm // LANES, PEER_N_KEYS, LANES),
                                lambda i: (0, i, 0, 0))],
        out_shape=[jax.ShapeDtypeStruct((D_MODEL, T), BF16)]
        + [jax.ShapeDtypeStruct((PEER_HEADS, PEER_N_KEYS, T), F32)] * 2
        + [jax.ShapeDtypeStruct((PEER_HEADS, T // LANES, PEER_N_KEYS, LANES), F32)],
        compiler_params=_cparams(("parallel",)),
        name="peer_prep",
    )(x2, lw["norm_ffn_g"], lw["peer_w_q"], lw["peer_keys"])


def _peer_main_kernel(x_ref, hn_ref, a1_ref, th_ref, e2_ref, u_ref, vt_ref, fg_ref, o_ref,
                      acc_ref, act_ref, w_ref, *, n_i, final_norm):
    e = pl.program_id(1)
    tm = hn_ref.shape[1]

    @pl.when(e == 0)
    def _():
        acc_ref[...] = jnp.zeros_like(acc_ref)

    n_blk = n_i * PEER_N_KEYS // PEER_BLK
    per_blk = PEER_BLK // PEER_N_KEYS

    n_strip = tm // LANES

    def store_activations(p, slot):
        a = _dot(u_ref[p * PEER_BLK:(p + 1) * PEER_BLK, :], hn_ref[...])
        act = a * (1.0 + lax.erf(a * (1.0 / math.sqrt(2.0))))
        for c in range(n_strip):
            act_ref[slot, c] = act[:, c * LANES:(c + 1) * LANES]

    def fold(p, slot):
        w = jnp.concatenate([w_ref[slot, c] for c in range(n_strip)], axis=1)
        acc_ref[...] += _dot(vt_ref[p], w)

    store_activations(0, 0)
    for p in range(n_blk):
        slot = p % 2
        if p + 1 < n_blk:
            store_activations(p + 1, 1 - slot)
        if p > 0:
            fold(p - 1, 1 - slot)
        for ii in range(per_blk):
            i = p * per_blk + ii
            rows = slice(ii * PEER_N_KEYS, (ii + 1) * PEER_N_KEYS)
            for c in range(n_strip):
                ls = slice(c * LANES, (c + 1) * LANES)
                g = None
                for hd in range(PEER_HEADS):
                    e2 = e2_ref[hd, c]
                    t = jnp.where(e2 > th_ref[hd, i:i + 1, ls], e2, 0.0) * a1_ref[hd, i:i + 1, ls]
                    g = t if g is None else g + t
                w_ref[slot, c, rows, :] = (g * act_ref[slot, c, rows, :]).astype(BF16)
    fold(n_blk - 1, (n_blk - 1) % 2)

    @pl.when(e == pl.num_programs(1) - 1)
    def _():
        y = x_ref[...] + acc_ref[...].T
        if final_norm:
            y = _rms(y, fg_ref[...])
        o_ref[...] = y


def _peer_main(x2, hn, a1, th, e2, lw, final_g, tm, n_i, final_norm):
    T = x2.shape[0]
    et = n_i * PEER_N_KEYS
    sc = pl.BlockSpec((PEER_HEADS, PEER_N_KEYS, tm), lambda t, e: (0, 0, t))
    return pl.pallas_call(
        functools.partial(_peer_main_kernel, n_i=n_i, final_norm=final_norm),
        grid=(T // tm, PEER_N_EXPERTS // et),
        in_specs=[pl.BlockSpec((tm, D_MODEL), lambda t, e: (t, 0)),
                  pl.BlockSpec((D_MODEL, tm), lambda t, e: (0, t)),
                  pl.BlockSpec((PEER_HEADS, n_i, tm), lambda t, e: (0, e, t)),
                  pl.BlockSpec((PEER_HEADS, n_i, tm), lambda t, e: (0, e, t)),
                  pl.BlockSpec((PEER_HEADS, tm // LANES, PEER_N_KEYS, LANES),
                               lambda t, e: (0, t, 0, 0)),
                  pl.BlockSpec((et, D_MODEL), lambda t, e: (e, 0)),
                  pl.BlockSpec((et // PEER_BLK, D_MODEL, PEER_BLK), lambda t, e: (e, 0, 0)),
                  pl.BlockSpec((1, D_MODEL), lambda t, e: (0, 0))],
        out_specs=pl.BlockSpec((tm, D_MODEL), lambda t, e: (t, 0)),
        out_shape=jax.ShapeDtypeStruct((T, D_MODEL), F32),
        scratch_shapes=[pltpu.VMEM((D_MODEL, tm), F32),
                        pltpu.VMEM((2, tm // LANES, PEER_BLK, LANES), F32),
                        pltpu.VMEM((2, tm // LANES, PEER_BLK, LANES), BF16)],
        compiler_params=_cparams(("parallel", "arbitrary")),
        name="peer_main",
    )(x2, hn, a1, th, e2, lw["peer_u"], lw["peer_vt"], final_g)


def _rot_half_cols(w):
    d = w.shape[-1]
    return jnp.concatenate([-w[..., d // 2:], w[..., :d // 2]], axis=-1)


def _pad_cols(w, left, total):
    return jnp.pad(w, ((0, 0), (left, total - left - w.shape[-1])))


def _head_groups(w, n_heads, width, left=0):
    return jnp.concatenate(
        [_pad_cols(w[:, h * width:(h + 1) * width], left, LANES) for h in range(n_heads)], axis=-1)


def _layer_weights(p, l):
    w_in = p["w_in"][l]
    widths = (MLA_Q_LORA, MLA_KV_LORA, MLA_ROPE, F_WIDTH, GQA_HEADS * GQA_HEAD_DIM,
              GQA_KV_HEADS * GQA_HEAD_DIM, GQA_KV_HEADS * GQA_HEAD_DIM, 3 * D_MODEL)
    offs = np.cumsum((0,) + widths)
    wcq, wckv, wkr, wxf, wgq, wgk, wgv, wgate = (w_in[:, offs[i]:offs[i + 1]] for i in range(8))

    def rot_heads(w, n_heads, width):
        return jnp.concatenate(
            [_rot_half_cols(w[:, h * width:(h + 1) * width]) for h in range(n_heads)], axis=-1)

    w1 = jnp.concatenate([
        wcq, wckv,
        _pad_cols(wkr, MLA_NOPE, LANES), _pad_cols(_rot_half_cols(wkr), MLA_NOPE, LANES),
        wxf,
        _head_groups(wgq, GQA_HEADS, GQA_HEAD_DIM),
        _head_groups(rot_heads(wgq, GQA_HEADS, GQA_HEAD_DIM), GQA_HEADS, GQA_HEAD_DIM),
        _head_groups(wgk, GQA_KV_HEADS, GQA_HEAD_DIM),
        _head_groups(rot_heads(wgk, GQA_KV_HEADS, GQA_HEAD_DIM), GQA_KV_HEADS, GQA_HEAD_DIM),
        _head_groups(wgv, GQA_KV_HEADS, GQA_HEAD_DIM),
        wgate], axis=-1).astype(BF16)
    assert w1.shape[-1] == _W1_COLS

    w_uq = p["mla_w_uq"][l]
    qd = MLA_NOPE + MLA_ROPE
    uq, uqr = [], []
    for h in range(MLA_HEADS):
        blk = w_uq[:, h * qd:(h + 1) * qd]
        uq.append(_pad_cols(blk, 0, LANES))
        uqr.append(_pad_cols(_rot_half_cols(blk[:, MLA_NOPE:]), MLA_NOPE, LANES))
    w_ukv = p["mla_w_ukv"][l]
    kd = MLA_NOPE + MLA_V
    uk = [_pad_cols(w_ukv[:, h * kd:h * kd + MLA_NOPE], 0, LANES) for h in range(MLA_HEADS)]
    uv = [_pad_cols(w_ukv[:, h * kd + MLA_NOPE:(h + 1) * kd], 0, LANES) for h in range(MLA_HEADS)]

    def pad_rows(w, n_heads, width):
        return jnp.concatenate(
            [jnp.pad(w[h * width:(h + 1) * width], ((0, LANES - width), (0, 0)))
             for h in range(n_heads)], axis=0)

    return {
        "norm_mix_g": p["norm_mix_g"][l][None, :],
        "w1": w1,
        "q_norm_g": p["mla_q_norm_g"][l][None, :],
        "w_uq": jnp.concatenate(uq, axis=-1).astype(BF16),
        "w_uq_rot": jnp.concatenate(uqr, axis=-1).astype(BF16),
        "kv_norm_g": p["mla_kv_norm_g"][l][None, :],
        "w_ukv": jnp.concatenate(uk + uv, axis=-1).astype(BF16),
        "mla_w_o": pad_rows(p["mla_w_o"][l], MLA_HEADS, MLA_V).astype(BF16),
        "fnet_w": p["fnet_w"][l].astype(BF16),
        "fnet_b": p["fnet_b"][l][None, :],
        "gqa_sink": p["gqa_sink"][l],
        "gqa_w_o": pad_rows(p["gqa_w_o"][l], GQA_HEADS, GQA_HEAD_DIM).astype(BF16),
        "w_out": p["w_out"][l].astype(BF16),
        "norm_ffn_g": p["norm_ffn_g"][l][None, :],
        "peer_w_q": p["peer_w_q"][l].astype(BF16),
        "peer_keys": p["peer_keys"][l].reshape(2 * PEER_HEADS, PEER_N_KEYS, PEER_HALF).astype(BF16),
        "peer_u": p["peer_u"][l].astype(BF16),
        "peer_vt": p["peer_v"][l].astype(BF16).reshape(
            PEER_N_EXPERTS // PEER_BLK, PEER_BLK, D_MODEL).transpose(0, 2, 1),
    }


def _rope_tables(S):
    pos = jnp.arange(S, dtype=F32)[:, None]

    def cs(d):
        inv = 1.0 / (ROPE_THETA ** (jnp.arange(0, d, 2, dtype=F32) / d))
        ang = pos * inv[None, :]
        return (jnp.concatenate([jnp.cos(ang)] * 2, axis=-1),
                jnp.concatenate([jnp.sin(ang)] * 2, axis=-1))

    c, s = cs(MLA_ROPE)
    rest = LANES - MLA_NOPE - MLA_ROPE
    cm = jnp.concatenate([jnp.ones((S, MLA_NOPE), F32), c, jnp.zeros((S, rest), F32)], axis=-1)
    sm = jnp.concatenate([jnp.zeros((S, MLA_NOPE), F32), s, jnp.zeros((S, rest), F32)], axis=-1)
    c, s = cs(GQA_HEAD_DIM)
    zero = jnp.zeros((S, LANES - GQA_HEAD_DIM), F32)
    return {"cm": cm, "sm": sm, "cg": jnp.concatenate([c, zero], axis=-1),
            "sg": jnp.concatenate([s, zero], axis=-1)}


def _fourier_tables(S):
    n2 = 128
    n1 = S // n2

    def cos_sin(num, den):
        ang = (2.0 * math.pi / den) * (num % den).astype(F32)
        return jnp.cos(ang), jnp.sin(ang)

    i1 = jnp.arange(n1, dtype=jnp.int32)
    c1, s1 = cos_sin(i1[:, None] * i1[None, :], n1)
    w1 = jnp.concatenate([c1, -s1], axis=0)
    i2 = jnp.arange(n2, dtype=jnp.int32)
    kk = i1[:, None, None] + n1 * i2[None, :, None]
    mr, ms = cos_sin(kk * i2[None, None, :], S)
    m2 = jnp.concatenate([jnp.concatenate([mr, ms], axis=-1),
                          jnp.concatenate([-ms, mr], axis=-1)], axis=1)
    ic = jnp.arange(F_GROUP_DIM, dtype=jnp.int32)
    cc, cs = cos_sin(ic[:, None] * ic[None, :], F_GROUP_DIM)
    eye = jnp.eye(F_GROUPS, dtype=F32)
    return {"n1": n1, "n2": n2, "w1": _split(w1), "m2": _split(m2),
            "cc": _split(jnp.kron(eye, cc)), "cs": _split(jnp.kron(eye, cs))}


def _pick(S, pref):
    t = pref
    while S % t:
        t //= 2
    return t


def _trunk(x, layers, final_g, rope_t, four_t):
    B, S, _ = x.shape
    T = B * S
    x2 = x.reshape(T, D_MODEL)
    tm = _pick(S, 256)
    n_layers = len(layers)
    for l, lw in enumerate(layers):
        q, k, v, xf, qg, kg, vg, gate = _in_proj(x2, lw, rope_t, S, tm)
        attn = _mla_attn(q, k, v, B, S, _pick(S, 1024), _pick(S // 2, 512))
        four = _fourier(xf, four_t, B, S)
        og = _gqa(qg, kg, vg, lw["gqa_sink"], B, S, _pick(S, 512))
        x2 = _merge(x2, attn, four, og, gate, lw, tm)
        hn, a1, th, e2 = _peer_prep(x2, lw, _pick(T, 256))
        x2 = _peer_main(x2, hn, a1, th, e2, lw, final_g, _pick(T, 512), 16,
                        final_norm=(l == n_layers - 1))
    return x2.reshape(B, S, D_MODEL)


def kernel(x_prompt, x_sample, norm_mix_g, w_in, mla_q_norm_g, mla_w_uq, mla_kv_norm_g, mla_w_ukv, mla_w_o, fnet_w, fnet_b, gqa_sink, gqa_w_o, w_out, norm_ffn_g, peer_w_q, peer_keys, peer_u, peer_v, final_norm_g):
    p = dict(norm_mix_g=norm_mix_g, w_in=w_in, mla_q_norm_g=mla_q_norm_g, mla_w_uq=mla_w_uq,
             mla_kv_norm_g=mla_kv_norm_g, mla_w_ukv=mla_w_ukv, mla_w_o=mla_w_o, fnet_w=fnet_w,
             fnet_b=fnet_b, gqa_sink=gqa_sink, gqa_w_o=gqa_w_o, w_out=w_out,
             norm_ffn_g=norm_ffn_g, peer_w_q=peer_w_q, peer_keys=peer_keys, peer_u=peer_u,
             peer_v=peer_v)
    layers = [_layer_weights(p, l) for l in range(w_in.shape[0])]
    final_g = final_norm_g[None, :]
    outs = []
    for x in (x_prompt, x_sample):
        S = x.shape[1]
        outs.append(_trunk(x, layers, final_g, _rope_tables(S), _fourier_tables(S)))
    return tuple(outs)
```

```python
import functools
import math

import jax
import jax.numpy as jnp
import numpy as np
from jax import lax
from jax.experimental import pallas as pl
from jax.experimental.pallas import tpu as pltpu

D_MODEL = 1024
EPS = 1e-6
ROPE_THETA = 10000.0
NEG = -1e30
LANES = 128

MLA_HEADS = 8
MLA_NOPE = 64
MLA_ROPE = 32
MLA_V = 64
MLA_Q_LORA = 384
MLA_KV_LORA = 256
F_GROUPS = 4
F_GROUP_DIM = 64
F_WIDTH = F_GROUPS * F_GROUP_DIM
GQA_HEADS = 4
GQA_KV_HEADS = 2
GQA_GROUP = GQA_HEADS // GQA_KV_HEADS
GQA_HEAD_DIM = 64
WINDOW = 128
PEER_HEADS = 8
PEER_N_KEYS = 128
PEER_N_EXPERTS = PEER_N_KEYS * PEER_N_KEYS
PEER_HALF = 128
PEER_TOPK = 16

VMEM_LIMIT = 56 * 1024 * 1024

BF16 = jnp.bfloat16
F32 = jnp.float32
HIGHEST = lax.Precision.HIGHEST


def _cparams(sem):
    return pltpu.CompilerParams(dimension_semantics=sem, vmem_limit_bytes=VMEM_LIMIT)


def _rms(x, g):
    return x * lax.rsqrt(jnp.mean(x * x, axis=-1, keepdims=True) + EPS) * g


def _dot(a, b):
    return jnp.dot(a, b, preferred_element_type=F32)


def _dot_nt(a, b):
    return lax.dot_general(a, b, (((1,), (1,)), ((), ())), preferred_element_type=F32)


_C_CQ = (0, 384)
_C_CKV = (384, 640)
_C_KR = (640, 768)
_C_XF = (768, 1024)
_C_GQ = (1024, 1536)
_C_GK = (1536, 1792)
_C_GV = (1792, 2048)
_C_GATE = (2048, 5120)
_W1_COLS = 5120


def _rotary(x, half, cos, sin_lo, sin_hi):
    n = x.shape[1]
    up = pltpu.roll(x, n - half, axis=1)
    dn = pltpu.roll(x, half, axis=1)
    out = []
    for g in range(n // LANES):
        sl = slice(g * LANES, (g + 1) * LANES)
        out.append(x[:, sl] * cos + up[:, sl] * sin_lo + dn[:, sl] * sin_hi)
    return out


def _in_proj_kernel(x_ref, g_ref, w1_ref, qng_ref, wuq_ref, kvng_ref, wukv_ref,
                    cm_ref, sml_ref, smh_ref, cg_ref, sgl_ref, sgh_ref,
                    q_ref, k_ref, v_ref, xf_ref, qg_ref, kg_ref, vg_ref, gate_ref):
    h = _rms(x_ref[...], g_ref[...]).astype(BF16)

    def proj(c):
        return _dot(h, w1_ref[:, c[0]:c[1]])

    mla_tabs = (cm_ref[...], sml_ref[...], smh_ref[...])
    gqa_tabs = (cg_ref[...], sgl_ref[...], sgh_ref[...])

    cqn = _rms(proj(_C_CQ), qng_ref[...]).astype(BF16)
    q = _rotary(_dot(cqn, wuq_ref[...]), MLA_ROPE // 2, *mla_tabs)
    mla_scale = (MLA_NOPE + MLA_ROPE) ** -0.5 * math.log2(math.e)
    for hd in range(MLA_HEADS):
        q_ref[:, hd * LANES:(hd + 1) * LANES] = (q[hd] * mla_scale).astype(BF16)

    ckvn = _rms(proj(_C_CKV), kvng_ref[...]).astype(BF16)
    kv = _dot(ckvn, wukv_ref[...])
    k_rope, = _rotary(proj(_C_KR), MLA_ROPE // 2, *mla_tabs)
    for hd in range(MLA_HEADS):
        sl = slice(hd * LANES, (hd + 1) * LANES)
        k_ref[:, sl] = (kv[:, sl] + k_rope).astype(BF16)
    lane = lax.broadcasted_iota(jnp.int32, (1, MLA_HEADS * LANES), 1)
    ones_cols = (lane % LANES == MLA_V).astype(F32)
    v_ref[...] = (kv[:, MLA_HEADS * LANES:] + ones_cols).T.astype(BF16)

    xf_ref[...] = proj(_C_XF)

    gq = _rotary(proj(_C_GQ), GQA_HEAD_DIM // 2, *gqa_tabs)
    gqa_scale = GQA_HEAD_DIM ** -0.5
    for hd in range(GQA_HEADS):
        qg_ref[:, hd * LANES:(hd + 1) * LANES] = (gq[hd] * gqa_scale).astype(BF16)
    gk = _rotary(proj(_C_GK), GQA_HEAD_DIM // 2, *gqa_tabs)
    for hd in range(GQA_KV_HEADS):
        kg_ref[:, hd * LANES:(hd + 1) * LANES] = gk[hd].astype(BF16)
    vg_ref[...] = proj(_C_GV).astype(BF16)

    gate = proj(_C_GATE)
    gate_ref[...] = (1.0 / (1.0 + jnp.exp(-gate))).astype(gate_ref.dtype)


def _in_proj(x2, lw, tabs, S, tm):
    T = x2.shape[0]
    nt = T // tm
    ns = S // tm
    full = lambda shape: pl.BlockSpec(shape, lambda i: (0,) * len(shape))
    row = lambda w: pl.BlockSpec((tm, w), lambda i: (i, 0))
    tab = pl.BlockSpec((tm, LANES), lambda i: (i % ns, 0))
    outs = [(1024, BF16), (1024, BF16), (1024, BF16), (F_WIDTH, F32), (512, BF16), (256, BF16),
            (256, BF16), (3 * D_MODEL, BF16)]
    return pl.pallas_call(
        _in_proj_kernel,
        grid=(nt,),
        in_specs=[row(D_MODEL), full((1, D_MODEL)), full((D_MODEL, _W1_COLS)),
                  full((1, MLA_Q_LORA)), full((MLA_Q_LORA, 1024)),
                  full((1, MLA_KV_LORA)), full((MLA_KV_LORA, 2048)),
                  tab, tab, tab, tab, tab, tab],
        out_specs=[pl.BlockSpec((w, tm), lambda i: (0, i)) if n == 2 else row(w)
                   for n, (w, _) in enumerate(outs)],
        out_shape=[jax.ShapeDtypeStruct((w, T) if n == 2 else (T, w), dt)
                   for n, (w, dt) in enumerate(outs)],
        compiler_params=_cparams(("parallel",)),
        name="in_proj",
    )(x2, lw["norm_mix_g"], lw["w1"], lw["q_norm_g"], lw["w_uq"], lw["kv_norm_g"], lw["w_ukv"],
      tabs["cm"], tabs["sm_lo"], tabs["sm_hi"], tabs["cg"], tabs["sg_lo"], tabs["sg_hi"])


def _mla_attn_kernel(q_ref, k_ref, vt_ref, o_ref, s_ref, p_ref, *, tk):
    tq = q_ref.shape[0]
    nk = k_ref.shape[0] // tk
    q = q_ref[...]

    def k_chunk(c):
        return k_ref[c * tk:(c + 1) * tk, :]

    def vt_chunk(c):
        return vt_ref[:, c * tk:(c + 1) * tk]

    def step(slot, vt_prev, k_next, carry):
        m, alpha, acc = carry
        pv = _dot(vt_prev, p_ref[1 - slot])
        s_ref[1 - slot] = _dot_nt(k_next, q)
        s = s_ref[slot]
        m_new = jnp.maximum(m, jnp.max(s, axis=0, keepdims=True))
        p_ref[slot] = jnp.exp2(s - m_new).astype(BF16)
        return m_new, jnp.exp2(m - m_new), acc * alpha + pv

    s_ref[0] = _dot_nt(k_chunk(0), q)
    p_ref[1] = jnp.zeros(p_ref.shape[1:], BF16)
    carry = (jnp.full((1, tq), NEG, F32), jnp.ones((1, tq), F32), jnp.zeros((LANES, tq), F32))
    for c in range(nk):
        carry = step(c % 2, vt_chunk(max(c - 1, 0)), k_chunk(min(c + 1, nk - 1)), carry)
    _, alpha, acc = carry
    acc = acc * alpha + _dot(vt_chunk(nk - 1), p_ref[1])
    o_ref[...] = (acc / acc[MLA_V:MLA_V + 1, :]).T.astype(o_ref.dtype)


def _mla_attn(q, k, vt, B, S, tq, tk):
    assert (S // tk) % 2 == 0
    q3, k3 = (a.reshape(B, S, MLA_HEADS * LANES) for a in (q, k))
    out = pl.pallas_call(
        functools.partial(_mla_attn_kernel, tk=tk),
        scratch_shapes=[pltpu.VMEM((2, tk, tq), F32), pltpu.VMEM((2, tk, tq), BF16)],
        grid=(B, MLA_HEADS, S // tq),
        in_specs=[pl.BlockSpec((None, tq, LANES), lambda b, h, i: (b, i, h)),
                  pl.BlockSpec((None, S, LANES), lambda b, h, i: (b, 0, h)),
                  pl.BlockSpec((LANES, S), lambda b, h, i: (h, b))],
        out_specs=pl.BlockSpec((None, tq, LANES), lambda b, h, i: (b, i, h)),
        out_shape=jax.ShapeDtypeStruct((B, S, MLA_HEADS * LANES), BF16),
        compiler_params=_cparams(("parallel", "parallel", "parallel")),
        name="mla_attn",
    )(q3, k3, vt)
    return out.reshape(B * S, MLA_HEADS * LANES)


def _split(x):
    hi = x.astype(BF16)
    return hi, (x - hi.astype(F32)).astype(BF16)


def _dot3(a, b):
    (ah, al), (bh, bl) = a, b
    return _dot(ah, bh) + (_dot(ah, bl) + _dot(al, bh))


def _fft1_kernel(wh_ref, wl_ref, x_ref, o_ref):
    o_ref[...] = _dot3((wh_ref[...], wl_ref[...]), _split(x_ref[...]))


def _fft2_kernel(mh_ref, ml_ref, a_ref, cch_ref, ccl_ref, csh_ref, csl_ref, o_ref, *, scale):
    cc = (cch_ref[...], ccl_ref[...])
    cs = (csh_ref[...], csl_ref[...])
    for j in range(o_ref.shape[0]):
        aa = jnp.concatenate([a_ref[0, j], a_ref[1, j]], axis=0)
        y = _dot3((mh_ref[j], ml_ref[j]), _split(aa))
        n2 = y.shape[0] // 2
        o_ref[j] = (_dot3(_split(y[:n2]), cc) + _dot3(_split(y[n2:]), cs)) * scale


def _fourier(xf, ft, B, S):
    n1, n2 = ft["n1"], ft["n2"]
    cols = n2 * F_WIDTH
    tc = min(cols, 4096)
    x3 = xf.reshape(B, n1, cols)
    a = pl.pallas_call(
        _fft1_kernel,
        grid=(B, cols // tc),
        in_specs=[pl.BlockSpec((2 * n1, n1), lambda b, j: (0, 0)),
                  pl.BlockSpec((2 * n1, n1), lambda b, j: (0, 0)),
                  pl.BlockSpec((None, n1, tc), lambda b, j: (b, 0, j))],
        out_specs=pl.BlockSpec((None, 2 * n1, tc), lambda b, j: (b, 0, j)),
        out_shape=jax.ShapeDtypeStruct((B, 2 * n1, cols), F32),
        compiler_params=_cparams(("parallel", "parallel")),
        name="fourier_stage1",
    )(*ft["w1"], x3)
    a5 = a.reshape(B, 2, n1, n2, F_WIDTH)
    kb = math.gcd(4, n1)
    table = pl.BlockSpec((F_WIDTH, F_WIDTH), lambda b, k: (0, 0))
    twiddle = pl.BlockSpec((kb, 2 * n2, 2 * n2), lambda b, k: (k, 0, 0))
    y = pl.pallas_call(
        functools.partial(_fft2_kernel, scale=1.0 / math.sqrt(S * F_GROUP_DIM)),
        grid=(B, n1 // kb),
        in_specs=[twiddle, twiddle,
                  pl.BlockSpec((None, 2, kb, n2, F_WIDTH), lambda b, k: (b, 0, k, 0, 0)),
                  table, table, table, table],
        out_specs=pl.BlockSpec((None, kb, n2, F_WIDTH), lambda b, k: (b, k, 0, 0)),
        out_shape=jax.ShapeDtypeStruct((B, n1, n2, F_WIDTH), F32),
        compiler_params=_cparams(("parallel", "parallel")),
        name="fourier_stage2",
    )(*ft["m2"], a5, *ft["cc"], *ft["cs"])
    return jnp.transpose(y, (0, 2, 1, 3)).reshape(B * S, F_WIDTH)


def _gqa_kernel(sink_ref, q_ref, kp_ref, kc_ref, kn_ref, vp_ref, vc_ref, vn_ref, o_ref, *, S):
    n = pl.program_id(1)
    tq, halo = q_ref.shape[0], kp_ref.shape[0]
    width = tq + 2 * halo
    qpos = n * tq + lax.broadcasted_iota(jnp.int32, (tq, width), 0)
    kpos = n * tq - halo + lax.broadcasted_iota(jnp.int32, (tq, width), 1)
    mask = (jnp.abs(qpos - kpos) <= WINDOW) & (kpos >= 0) & (kpos < S)
    for hd in range(GQA_HEADS):
        kh = hd // GQA_GROUP
        qs = slice(hd * LANES, (hd + 1) * LANES)
        ks = slice(kh * LANES, (kh + 1) * LANES)
        q = q_ref[:, qs]
        s = jnp.concatenate([_dot_nt(q, kp_ref[:, ks]), _dot_nt(q, kc_ref[:, ks]),
                             _dot_nt(q, kn_ref[:, ks])], axis=-1)
        s = jnp.where(mask, s, NEG)
        sk = sink_ref[hd]
        m = jnp.maximum(jnp.max(s, axis=-1, keepdims=True), sk)
        p = jnp.exp(s - m)
        den = jnp.sum(p, axis=-1, keepdims=True) + jnp.exp(sk - m)
        pb = (p / den).astype(BF16)
        o = (_dot(pb[:, :halo], vp_ref[:, ks]) + _dot(pb[:, halo:halo + tq], vc_ref[:, ks])
             + _dot(pb[:, halo + tq:], vn_ref[:, ks]))
        o_ref[:, qs] = o.astype(o_ref.dtype)


def _gqa(qg, kg, vg, sink, B, S, tq):
    halo = WINDOW
    per = tq // halo
    nb = S // halo
    q3 = qg.reshape(B, S, GQA_HEADS * LANES)
    k3 = kg.reshape(B, S, GQA_KV_HEADS * LANES)
    v3 = vg.reshape(B, S, GQA_KV_HEADS * LANES)
    kvw = GQA_KV_HEADS * LANES
    prev = pl.BlockSpec((None, halo, kvw), lambda b, n: (b, jnp.maximum(n * per - 1, 0), 0))
    cur = pl.BlockSpec((None, tq, kvw), lambda b, n: (b, n, 0))
    nxt = pl.BlockSpec((None, halo, kvw), lambda b, n: (b, jnp.minimum((n + 1) * per, nb - 1), 0))
    out = pl.pallas_call(
        functools.partial(_gqa_kernel, S=S),
        grid=(B, S // tq),
        in_specs=[pl.BlockSpec(memory_space=pltpu.SMEM),
                  pl.BlockSpec((None, tq, GQA_HEADS * LANES), lambda b, n: (b, n, 0)),
                  prev, cur, nxt, prev, cur, nxt],
        out_specs=pl.BlockSpec((None, tq, GQA_HEADS * LANES), lambda b, n: (b, n, 0)),
        out_shape=jax.ShapeDtypeStruct((B, S, GQA_HEADS * LANES), BF16),
        compiler_params=_cparams(("parallel", "parallel")),
        name="gqa_window",
    )(sink, q3, k3, k3, k3, v3, v3, v3)
    return out.reshape(B * S, GQA_HEADS * LANES)


def _merge_kernel(x_ref, attn_ref, four_ref, og_ref, gate_ref, wo_ref, fw_ref, fb_ref, gwo_ref,
                  wout_ref, o_ref):
    mla_o = _dot(attn_ref[...], wo_ref[...])
    fnet_o = _dot(four_ref[...].astype(BF16), fw_ref[...]) + fb_ref[...]
    gqa_o = _dot(og_ref[...], gwo_ref[...])
    d = D_MODEL
    merged = (gate_ref[:, 0:d] * mla_o + gate_ref[:, d:2 * d] * fnet_o
              + gate_ref[:, 2 * d:3 * d] * gqa_o)
    o_ref[...] = x_ref[...] + _dot(merged.astype(BF16), wout_ref[...])


def _merge(x2, attn, four, og, gate, lw, tm):
    T = x2.shape[0]
    full = lambda shape: pl.BlockSpec(shape, lambda i: (0,) * len(shape))
    row = lambda w: pl.BlockSpec((tm, w), lambda i: (i, 0))
    return pl.pallas_call(
        _merge_kernel,
        grid=(T // tm,),
        in_specs=[row(D_MODEL), row(1024), row(F_WIDTH), row(512), row(3 * D_MODEL),
                  full((1024, D_MODEL)), full((F_WIDTH, D_MODEL)), full((1, D_MODEL)),
                  full((512, D_MODEL)), full((D_MODEL, D_MODEL))],
        out_specs=row(D_MODEL),
        out_shape=jax.ShapeDtypeStruct((T, D_MODEL), F32),
        compiler_params=_cparams(("parallel",)),
        name="merge",
    )(x2, attn, four, og, gate, lw["mla_w_o"], lw["fnet_w"], lw["fnet_b"], lw["gqa_w_o"],
      lw["w_out"])


LOG2E = math.log2(math.e)
PEER_BLK = 256
PEER_RANKS = PEER_TOPK + 1


def _sort_network(n):
    pairs = []

    def merge(lo, m, r):
        step = 2 * r
        if step < m:
            merge(lo, m, step)
            merge(lo + r, m, step)
            pairs.extend((i, i + r) for i in range(lo + r, lo + m - r, step))
        else:
            pairs.append((lo, lo + r))

    def sort(lo, m):
        if m > 1:
            sort(lo, m // 2)
            sort(lo + m // 2, m // 2)
            merge(lo, m, 1)

    sort(0, n)
    return pairs


def _top_of_sorted_lists(lists, k, extra=None):
    lists = list(lists)
    neg = jnp.full(lists[0].shape, -jnp.inf, F32)
    vals = []
    for r in range(k):
        mx = jnp.max(lists[0], axis=0, keepdims=True)
        if extra is not None:
            mx = jnp.maximum(mx, jnp.max(extra, axis=0, keepdims=True))
        vals.append(mx)
        if r == k - 1:
            break
        win = lists[0] == mx
        for v in range(min(len(lists), k - 1 - r)):
            lists[v] = jnp.where(win, lists[v + 1] if v + 1 < len(lists) else neg, lists[v])
        if extra is not None:
            extra = jnp.where(extra == mx, -jnp.inf, extra)
    return vals


def _top_sorted(s, k):
    lists = [s[8 * v:8 * v + 8] for v in range(s.shape[0] // 8)]
    for i, j in _sort_network(len(lists)):
        lists[i], lists[j] = jnp.maximum(lists[i], lists[j]), jnp.minimum(lists[i], lists[j])
    return _top_of_sorted_lists(lists, k)


def _peer_prep_kernel(x_ref, g_ref, wq_ref, keys_ref, hn_ref, a1_ref, th_ref, e2_ref):
    hn = _rms(x_ref[...], g_ref[...])
    hn_ref[...] = hn.T.astype(BF16)
    q = _dot(hn.astype(BF16), wq_ref[...]).astype(BF16)
    tm = q.shape[0]
    for hd in range(PEER_HEADS):
        sts, tops = [], []
        for p in range(2):
            hp = 2 * hd + p
            st = _dot_nt(keys_ref[hp], q[:, hp * PEER_HALF:(hp + 1) * PEER_HALF])
            sts.append(st)
            tops.append(_top_sorted(st, PEER_RANKS))
        a, b = tops
        a_lo = jnp.concatenate(a[:8], axis=0)
        a_hi = jnp.concatenate(a[8:] + [jnp.full((7, tm), -jnp.inf, F32)], axis=0)
        top = _top_of_sorted_lists([a_lo + b[r] for r in range(PEER_RANKS)], PEER_RANKS,
                                   extra=a_hi + b[0])
        m = top[0]
        log_z = jnp.log(sum(jnp.exp(t - m) for t in top[:PEER_TOPK]))
        thr = 0.5 * (top[PEER_TOPK - 1] + top[PEER_TOPK])
        st1 = (sts[0] - a[0]) * LOG2E
        s2k = (sts[1] - b[0] - log_z) * LOG2E
        a1_ref[hd] = jnp.exp2(st1)
        th_ref[hd] = jnp.exp2((thr - m - log_z) * LOG2E - st1 - 1.0)
        e2 = jnp.exp2(s2k - 1.0)
        for c in range(tm // LANES):
            e2_ref[hd, c] = e2[:, c * LANES:(c + 1) * LANES]


def _peer_prep(x2, lw, tm):
    T = x2.shape[0]
    sc = pl.BlockSpec((PEER_HEADS, PEER_N_KEYS, tm), lambda i: (0, 0, i))
    return pl.pallas_call(
        _peer_prep_kernel,
        grid=(T // tm,),
        in_specs=[pl.BlockSpec((tm, D_MODEL), lambda i: (i, 0)),
                  pl.BlockSpec((1, D_MODEL), lambda i: (0, 0)),
                  pl.BlockSpec((D_MODEL, 2 * PEER_HEADS * PEER_HALF), lambda i: (0, 0)),
                  pl.BlockSpec((2 * PEER_HEADS, PEER_N_KEYS, PEER_HALF), lambda i: (0, 0, 0))],
        out_specs=[pl.BlockSpec((D_MODEL, tm), lambda i: (0, i)), sc, sc,
                   pl.BlockSpec((PEER_HEADS, tm // LANES, PEER_N_KEYS, LANES),
                                lambda i: (0, i, 0, 0))],
        out_shape=[jax.ShapeDtypeStruct((D_MODEL, T), BF16)]
        + [jax.ShapeDtypeStruct((PEER_HEADS, PEER_N_KEYS, T), F32)] * 2
        + [jax.ShapeDtypeStruct((PEER_HEADS, T // LANES, PEER_N_KEYS, LANES), F32)],
        compiler_params=_cparams(("parallel",)),
        name="peer_prep",
    )(x2, lw["norm_ffn_g"], lw["peer_w_q"], lw["peer_keys"])


def _peer_main_kernel(x_ref, hn_ref, a1_ref, th_ref, e2_ref, u_ref, vt_ref, fg_ref, o_ref,
                      acc_ref, act_ref, w_ref, *, n_i, final_norm):
    e = pl.program_id(1)
    tm = hn_ref.shape[1]

    @pl.when(e == 0)
    def _():
        acc_ref[...] = jnp.zeros_like(acc_ref)

    n_blk = n_i * PEER_N_KEYS // PEER_BLK
    per_blk = PEER_BLK // PEER_N_KEYS

    n_strip = tm // LANES

    def store_activations(p, slot):
        a = _dot(u_ref[p * PEER_BLK:(p + 1) * PEER_BLK, :], hn_ref[...])
        act = a * (1.0 + lax.erf(a * (1.0 / math.sqrt(2.0))))
        for c in range(n_strip):
            act_ref[slot, c] = act[:, c * LANES:(c + 1) * LANES]

    def fold(p, slot):
        w = jnp.concatenate([w_ref[slot, c] for c in range(n_strip)], axis=1)
        acc_ref[...] += _dot(vt_ref[p], w)

    store_activations(0, 0)
    for p in range(n_blk):
        slot = p % 2
        if p + 1 < n_blk:
            store_activations(p + 1, 1 - slot)
        if p > 0:
            fold(p - 1, 1 - slot)
        for ii in range(per_blk):
            i = p * per_blk + ii
            rows = slice(ii * PEER_N_KEYS, (ii + 1) * PEER_N_KEYS)
            for c in range(n_strip):
                ls = slice(c * LANES, (c + 1) * LANES)
                g = None
                for hd in range(PEER_HEADS):
                    e2 = e2_ref[hd, c]
                    t = jnp.where(e2 > th_ref[hd, i:i + 1, ls], e2, 0.0) * a1_ref[hd, i:i + 1, ls]
                    g = t if g is None else g + t
                w_ref[slot, c, rows, :] = (g * act_ref[slot, c, rows, :]).astype(BF16)
    fold(n_blk - 1, (n_blk - 1) % 2)

    @pl.when(e == pl.num_programs(1) - 1)
    def _():
        y = x_ref[...] + acc_ref[...].T
        if final_norm:
            y = _rms(y, fg_ref[...])
        o_ref[...] = y


def _peer_main(x2, hn, a1, th, e2, lw, final_g, tm, n_i, final_norm):
    T = x2.shape[0]
    et = n_i * PEER_N_KEYS
    sc = pl.BlockSpec((PEER_HEADS, PEER_N_KEYS, tm), lambda t, e: (0, 0, t))
    return pl.pallas_call(
        functools.partial(_peer_main_kernel, n_i=n_i, final_norm=final_norm),
        grid=(T // tm, PEER_N_EXPERTS // et),
        in_specs=[pl.BlockSpec((tm, D_MODEL), lambda t, e: (t, 0)),
                  pl.BlockSpec((D_MODEL, tm), lambda t, e: (0, t)),
                  pl.BlockSpec((PEER_HEADS, n_i, tm), lambda t, e: (0, e, t)),
                  pl.BlockSpec((PEER_HEADS, n_i, tm), lambda t, e: (0, e, t)),
                  pl.BlockSpec((PEER_HEADS, tm // LANES, PEER_N_KEYS, LANES),
                               lambda t, e: (0, t, 0, 0)),
                  pl.BlockSpec((et, D_MODEL), lambda t, e: (e, 0)),
                  pl.BlockSpec((et // PEER_BLK, D_MODEL, PEER_BLK), lambda t, e: (e, 0, 0)),
                  pl.BlockSpec((1, D_MODEL), lambda t, e: (0, 0))],
        out_specs=pl.BlockSpec((tm, D_MODEL), lambda t, e: (t, 0)),
        out_shape=jax.ShapeDtypeStruct((T, D_MODEL), F32),
        scratch_shapes=[pltpu.VMEM((D_MODEL, tm), F32),
                        pltpu.VMEM((2, tm // LANES, PEER_BLK, LANES), F32),
                        pltpu.VMEM((2, tm // LANES, PEER_BLK, LANES), BF16)],
        compiler_params=_cparams(("parallel", "arbitrary")),
        name="peer_main",
    )(x2, hn, a1, th, e2, lw["peer_u"], lw["peer_vt"], final_g)


def _pad_cols(w, left, total):
    return jnp.pad(w, ((0, 0), (left, total - left - w.shape[-1])))


def _head_groups(w, n_heads, width, left=0):
    return jnp.concatenate(
        [_pad_cols(w[:, h * width:(h + 1) * width], left, LANES) for h in range(n_heads)], axis=-1)


def _layer_weights(p, l):
    w_in = p["w_in"][l]
    widths = (MLA_Q_LORA, MLA_KV_LORA, MLA_ROPE, F_WIDTH, GQA_HEADS * GQA_HEAD_DIM,
              GQA_KV_HEADS * GQA_HEAD_DIM, GQA_KV_HEADS * GQA_HEAD_DIM, 3 * D_MODEL)
    offs = np.cumsum((0,) + widths)
    wcq, wckv, wkr, wxf, wgq, wgk, wgv, wgate = (w_in[:, offs[i]:offs[i + 1]] for i in range(8))

    w1 = jnp.concatenate([
        wcq, wckv,
        _pad_cols(wkr, MLA_NOPE, LANES),
        wxf,
        _head_groups(wgq, GQA_HEADS, GQA_HEAD_DIM),
        _head_groups(wgk, GQA_KV_HEADS, GQA_HEAD_DIM),
        _head_groups(wgv, GQA_KV_HEADS, GQA_HEAD_DIM),
        wgate], axis=-1).astype(BF16)
    assert w1.shape[-1] == _W1_COLS

    w_uq = p["mla_w_uq"][l]
    qd = MLA_NOPE + MLA_ROPE
    uq = [_pad_cols(w_uq[:, h * qd:(h + 1) * qd], 0, LANES) for h in range(MLA_HEADS)]
    w_ukv = p["mla_w_ukv"][l]
    kd = MLA_NOPE + MLA_V
    uk = [_pad_cols(w_ukv[:, h * kd:h * kd + MLA_NOPE], 0, LANES) for h in range(MLA_HEADS)]
    uv = [_pad_cols(w_ukv[:, h * kd + MLA_NOPE:(h + 1) * kd], 0, LANES) for h in range(MLA_HEADS)]

    def pad_rows(w, n_heads, width):
        return jnp.concatenate(
            [jnp.pad(w[h * width:(h + 1) * width], ((0, LANES - width), (0, 0)))
             for h in range(n_heads)], axis=0)

    return {
        "norm_mix_g": p["norm_mix_g"][l][None, :],
        "w1": w1,
        "q_norm_g": p["mla_q_norm_g"][l][None, :],
        "w_uq": jnp.concatenate(uq, axis=-1).astype(BF16),
        "kv_norm_g": p["mla_kv_norm_g"][l][None, :],
        "w_ukv": jnp.concatenate(uk + uv, axis=-1).astype(BF16),
        "mla_w_o": pad_rows(p["mla_w_o"][l], MLA_HEADS, MLA_V).astype(BF16),
        "fnet_w": p["fnet_w"][l].astype(BF16),
        "fnet_b": p["fnet_b"][l][None, :],
        "gqa_sink": p["gqa_sink"][l],
        "gqa_w_o": pad_rows(p["gqa_w_o"][l], GQA_HEADS, GQA_HEAD_DIM).astype(BF16),
        "w_out": p["w_out"][l].astype(BF16),
        "norm_ffn_g": p["norm_ffn_g"][l][None, :],
        "peer_w_q": p["peer_w_q"][l].astype(BF16),
        "peer_keys": p["peer_keys"][l].reshape(2 * PEER_HEADS, PEER_N_KEYS, PEER_HALF).astype(BF16),
        "peer_u": p["peer_u"][l].astype(BF16),
        "peer_vt": p["peer_v"][l].astype(BF16).reshape(
            PEER_N_EXPERTS // PEER_BLK, PEER_BLK, D_MODEL).transpose(0, 2, 1),
    }


def _rope_tables(S):
    pos = jnp.arange(S, dtype=F32)[:, None]

    def tables(d, left, passthrough):
        inv = 1.0 / (ROPE_THETA ** (jnp.arange(0, d, 2, dtype=F32) / d))
        ang = pos * inv[None, :]
        cos, sin, zero = jnp.cos(ang), jnp.sin(ang), jnp.zeros_like(ang)
        pad = lambda parts: _pad_cols(jnp.concatenate(parts, axis=-1), left, LANES)
        lead = jnp.pad(jnp.ones((S, passthrough), F32), ((0, 0), (0, LANES - passthrough)))
        return pad([cos, cos]) + lead, pad([-sin, zero]), pad([zero, sin])

    cm, sm_lo, sm_hi = tables(MLA_ROPE, MLA_NOPE, MLA_NOPE)
    cg, sg_lo, sg_hi = tables(GQA_HEAD_DIM, 0, 0)
    return {"cm": cm, "sm_lo": sm_lo, "sm_hi": sm_hi, "cg": cg, "sg_lo": sg_lo, "sg_hi": sg_hi}


def _fourier_tables(S):
    n2 = 128
    n1 = S // n2

    def cos_sin(num, den):
        ang = (2.0 * math.pi / den) * (num % den).astype(F32)
        return jnp.cos(ang), jnp.sin(ang)

    i1 = jnp.arange(n1, dtype=jnp.int32)
    c1, s1 = cos_sin(i1[:, None] * i1[None, :], n1)
    w1 = jnp.concatenate([c1, -s1], axis=0)
    i2 = jnp.arange(n2, dtype=jnp.int32)
    kk = i1[:, None, None] + n1 * i2[None, :, None]
    mr, ms = cos_sin(kk * i2[None, None, :], S)
    m2 = jnp.concatenate([jnp.concatenate([mr, ms], axis=-1),
                          jnp.concatenate([-ms, mr], axis=-1)], axis=1)
    ic = jnp.arange(F_GROUP_DIM, dtype=jnp.int32)
    cc, cs = cos_sin(ic[:, None] * ic[None, :], F_GROUP_DIM)
    eye = jnp.eye(F_GROUPS, dtype=F32)
    return {"n1": n1, "n2": n2, "w1": _split(w1), "m2": _split(m2),
            "cc": _split(jnp.kron(eye, cc)), "cs": _split(jnp.kron(eye, cs))}


def _pick(S, pref):
    t = pref
    while S % t:
        t //= 2
    return t


def _trunk(x, layers, final_g, rope_t, four_t):
    B, S, _ = x.shape
    T = B * S
    x2 = x.reshape(T, D_MODEL)
    tm = _pick(S, 256)
    n_layers = len(layers)
    for l, lw in enumerate(layers):
        q, k, v, xf, qg, kg, vg, gate = _in_proj(x2, lw, rope_t, S, tm)
        attn = _mla_attn(q, k, v, B, S, _pick(S, 1024), _pick(S // 2, 512))
        four = _fourier(xf, four_t, B, S)
        og = _gqa(qg, kg, vg, lw["gqa_sink"], B, S, _pick(S, 512))
        x2 = _merge(x2, attn, four, og, gate, lw, tm)
        hn, a1, th, e2 = _peer_prep(x2, lw, _pick(T, 256))
        x2 = _peer_main(x2, hn, a1, th, e2, lw, final_g, _pick(T, 512), 16,
                        final_norm=(l == n_layers - 1))
    return x2.reshape(B, S, D_MODEL)


def kernel(x_prompt, x_sample, norm_mix_g, w_in, mla_q_norm_g, mla_w_uq, mla_kv_norm_g, mla_w_ukv, mla_w_o, fnet_w, fnet_b, gqa_sink, gqa_w_o, w_out, norm_ffn_g, peer_w_q, peer_keys, peer_u, peer_v, final_norm_g):
    p = dict(norm_mix_g=norm_mix_g, w_in=w_in, mla_q_norm_g=mla_q_norm_g, mla_w_uq=mla_w_uq,
             mla_kv_norm_g=mla_kv_norm_g, mla_w_ukv=mla_w_ukv, mla_w_o=mla_w_o, fnet_w=fnet_w,
             fnet_b=fnet_b, gqa_sink=gqa_sink, gqa_w_o=gqa_w_o, w_out=w_out,
             norm_ffn_g=norm_ffn_g, peer_w_q=peer_w_q, peer_keys=peer_keys, peer_u=peer_u,
             peer_v=peer_v)
    layers = [_layer_weights(p, l) for l in range(w_in.shape[0])]
    final_g = final_norm_g[None, :]
    outs = []
    for x in (x_prompt, x_sample):
        S = x.shape[1]
        outs.append(_trunk(x, layers, final_g, _rope_tables(S), _fourier_tables(S)))
    return tuple(outs)
```

```python
import functools
import math

import jax
import jax.numpy as jnp
import numpy as np
from jax import lax
from jax.experimental import pallas as pl
from jax.experimental.pallas import tpu as pltpu

D_MODEL = 1024
EPS = 1e-6
ROPE_THETA = 10000.0
NEG = -1e30
LANES = 128

MLA_HEADS = 8
MLA_NOPE = 64
MLA_ROPE = 32
MLA_V = 64
MLA_Q_LORA = 384
MLA_KV_LORA = 256
F_GROUPS = 4
F_GROUP_DIM = 64
F_WIDTH = F_GROUPS * F_GROUP_DIM
GQA_HEADS = 4
GQA_KV_HEADS = 2
GQA_GROUP = GQA_HEADS // GQA_KV_HEADS
GQA_HEAD_DIM = 64
WINDOW = 128
PEER_HEADS = 8
PEER_N_KEYS = 128
PEER_N_EXPERTS = PEER_N_KEYS * PEER_N_KEYS
PEER_HALF = 128
PEER_TOPK = 16

VMEM_LIMIT = 56 * 1024 * 1024

BF16 = jnp.bfloat16
F32 = jnp.float32
HIGHEST = lax.Precision.HIGHEST


def _cparams(sem):
    return pltpu.CompilerParams(dimension_semantics=sem, vmem_limit_bytes=VMEM_LIMIT)


def _rms(x, g):
    return x * lax.rsqrt(jnp.mean(x * x, axis=-1, keepdims=True) + EPS) * g


def _dot(a, b):
    return jnp.dot(a, b, preferred_element_type=F32)


def _dot_nt(a, b):
    return lax.dot_general(a, b, (((1,), (1,)), ((), ())), preferred_element_type=F32)


_C_CQ = (0, 384)
_C_CKV = (384, 640)
_C_KR = (640, 768)
_C_XF = (768, 1024)
_C_GQ = (1024, 1536)
_C_GK = (1536, 1792)
_C_GV = (1792, 2048)
_C_GATE = (2048, 5120)
_W1_COLS = 5120


def _rotary(x, half, cos, sin_lo, sin_hi):
    n = x.shape[1]
    up = pltpu.roll(x, n - half, axis=1)
    dn = pltpu.roll(x, half, axis=1)
    out = []
    for g in range(n // LANES):
        sl = slice(g * LANES, (g + 1) * LANES)
        out.append(x[:, sl] * cos + up[:, sl] * sin_lo + dn[:, sl] * sin_hi)
    return out


def _in_proj_kernel(x_ref, g_ref, w1_ref, qng_ref, wuq_ref, kvng_ref, wukv_ref,
                    cm_ref, sml_ref, smh_ref, cg_ref, sgl_ref, sgh_ref,
                    q_ref, k_ref, v_ref, xf_ref, qg_ref, kg_ref, vg_ref, gate_ref):
    h = _rms(x_ref[...], g_ref[...]).astype(BF16)

    def proj(c):
        return _dot(h, w1_ref[:, c[0]:c[1]])

    mla_tabs = (cm_ref[...], sml_ref[...], smh_ref[...])
    gqa_tabs = (cg_ref[...], sgl_ref[...], sgh_ref[...])

    cqn = _rms(proj(_C_CQ), qng_ref[...]).astype(BF16)
    q = _rotary(_dot(cqn, wuq_ref[...]), MLA_ROPE // 2, *mla_tabs)
    mla_scale = (MLA_NOPE + MLA_ROPE) ** -0.5 * math.log2(math.e)
    for hd in range(MLA_HEADS):
        q_ref[:, hd * LANES:(hd + 1) * LANES] = (q[hd] * mla_scale).astype(BF16)

    ckvn = _rms(proj(_C_CKV), kvng_ref[...]).astype(BF16)
    kv = _dot(ckvn, wukv_ref[...])
    k_rope, = _rotary(proj(_C_KR), MLA_ROPE // 2, *mla_tabs)
    for hd in range(MLA_HEADS):
        sl = slice(hd * LANES, (hd + 1) * LANES)
        k_ref[:, sl] = (kv[:, sl] + k_rope).astype(BF16)
    lane = lax.broadcasted_iota(jnp.int32, (1, MLA_HEADS * LANES), 1)
    ones_cols = (lane % LANES == MLA_V).astype(F32)
    v_ref[...] = (kv[:, MLA_HEADS * LANES:] + ones_cols).T.astype(BF16)

    xf_ref[...] = proj(_C_XF)

    gq = _rotary(proj(_C_GQ), GQA_HEAD_DIM // 2, *gqa_tabs)
    gqa_scale = GQA_HEAD_DIM ** -0.5
    for hd in range(GQA_HEADS):
        qg_ref[:, hd * LANES:(hd + 1) * LANES] = (gq[hd] * gqa_scale).astype(BF16)
    gk = _rotary(proj(_C_GK), GQA_HEAD_DIM // 2, *gqa_tabs)
    for hd in range(GQA_KV_HEADS):
        kg_ref[:, hd * LANES:(hd + 1) * LANES] = gk[hd].astype(BF16)
    vg_ref[...] = proj(_C_GV).astype(BF16)

    gate = proj(_C_GATE)
    gate_ref[...] = (1.0 / (1.0 + jnp.exp(-gate))).astype(gate_ref.dtype)


def _in_proj(x2, lw, tabs, S, tm):
    T = x2.shape[0]
    nt = T // tm
    ns = S // tm
    full = lambda shape: pl.BlockSpec(shape, lambda i: (0,) * len(shape))
    row = lambda w: pl.BlockSpec((tm, w), lambda i: (i, 0))
    tab = pl.BlockSpec((tm, LANES), lambda i: (i % ns, 0))
    outs = [(1024, BF16), (1024, BF16), (1024, BF16), (F_WIDTH, F32), (512, BF16), (256, BF16),
            (256, BF16), (3 * D_MODEL, BF16)]
    return pl.pallas_call(
        _in_proj_kernel,
        grid=(nt,),
        in_specs=[row(D_MODEL), full((1, D_MODEL)), full((D_MODEL, _W1_COLS)),
                  full((1, MLA_Q_LORA)), full((MLA_Q_LORA, 1024)),
                  full((1, MLA_KV_LORA)), full((MLA_KV_LORA, 2048)),
                  tab, tab, tab, tab, tab, tab],
        out_specs=[pl.BlockSpec((w, tm), lambda i: (0, i)) if n == 2 else row(w)
                   for n, (w, _) in enumerate(outs)],
        out_shape=[jax.ShapeDtypeStruct((w, T) if n == 2 else (T, w), dt)
                   for n, (w, dt) in enumerate(outs)],
        compiler_params=_cparams(("parallel",)),
        name="in_proj",
    )(x2, lw["norm_mix_g"], lw["w1"], lw["q_norm_g"], lw["w_uq"], lw["kv_norm_g"], lw["w_ukv"],
      tabs["cm"], tabs["sm_lo"], tabs["sm_hi"], tabs["cg"], tabs["sg_lo"], tabs["sg_hi"])


def _mla_attn_kernel(q_ref, k_ref, vt_ref, o_ref, s_ref, p_ref, *, tk):
    tq = q_ref.shape[0]
    nk = k_ref.shape[0] // tk
    q = q_ref[...]

    def k_chunk(c):
        return k_ref[c * tk:(c + 1) * tk, :]

    def vt_chunk(c):
        return vt_ref[:, c * tk:(c + 1) * tk]

    def step(slot, vt_prev, k_next, carry):
        m, alpha, acc = carry
        pv = _dot(vt_prev, p_ref[1 - slot])
        s_ref[1 - slot] = _dot_nt(k_next, q)
        s = s_ref[slot]
        m_new = jnp.maximum(m, jnp.max(s, axis=0, keepdims=True))
        p_ref[slot] = jnp.exp2(s - m_new).astype(BF16)
        return m_new, jnp.exp2(m - m_new), acc * alpha + pv

    s_ref[0] = _dot_nt(k_chunk(0), q)
    p_ref[1] = jnp.zeros(p_ref.shape[1:], BF16)
    carry = (jnp.full((1, tq), NEG, F32), jnp.ones((1, tq), F32), jnp.zeros((LANES, tq), F32))
    for c in range(nk):
        carry = step(c % 2, vt_chunk(max(c - 1, 0)), k_chunk(min(c + 1, nk - 1)), carry)
    _, alpha, acc = carry
    acc = acc * alpha + _dot(vt_chunk(nk - 1), p_ref[1])
    o_ref[...] = (acc / acc[MLA_V:MLA_V + 1, :]).T.astype(o_ref.dtype)


def _mla_attn(q, k, vt, B, S, tq, tk):
    assert (S // tk) % 2 == 0
    q3, k3 = (a.reshape(B, S, MLA_HEADS * LANES) for a in (q, k))
    out = pl.pallas_call(
        functools.partial(_mla_attn_kernel, tk=tk),
        scratch_shapes=[pltpu.VMEM((2, tk, tq), F32), pltpu.VMEM((2, tk, tq), BF16)],
        grid=(B, MLA_HEADS, S // tq),
        in_specs=[pl.BlockSpec((None, tq, LANES), lambda b, h, i: (b, i, h)),
                  pl.BlockSpec((None, S, LANES), lambda b, h, i: (b, 0, h)),
                  pl.BlockSpec((LANES, S), lambda b, h, i: (h, b))],
        out_specs=pl.BlockSpec((None, tq, LANES), lambda b, h, i: (b, i, h)),
        out_shape=jax.ShapeDtypeStruct((B, S, MLA_HEADS * LANES), BF16),
        compiler_params=_cparams(("parallel", "parallel", "parallel")),
        name="mla_attn",
    )(q3, k3, vt)
    return out.reshape(B * S, MLA_HEADS * LANES)


def _split(x):
    hi = x.astype(BF16)
    return hi, (x - hi.astype(F32)).astype(BF16)


def _dot3(a, b):
    (ah, al), (bh, bl) = a, b
    return _dot(ah, bh) + (_dot(ah, bl) + _dot(al, bh))


def _fft1_kernel(wh_ref, wl_ref, x_ref, o_ref):
    o_ref[...] = _dot3((wh_ref[...], wl_ref[...]), _split(x_ref[...]))


def _fft2_kernel(mh_ref, ml_ref, a_ref, cch_ref, ccl_ref, csh_ref, csl_ref, o_ref, *, scale):
    cc = (cch_ref[...], ccl_ref[...])
    cs = (csh_ref[...], csl_ref[...])
    for j in range(o_ref.shape[0]):
        aa = jnp.concatenate([a_ref[0, j], a_ref[1, j]], axis=0)
        y = _dot3((mh_ref[j], ml_ref[j]), _split(aa))
        n2 = y.shape[0] // 2
        o_ref[j] = (_dot3(_split(y[:n2]), cc) + _dot3(_split(y[n2:]), cs)) * scale


def _fourier(xf, ft, B, S):
    n1, n2 = ft["n1"], ft["n2"]
    cols = n2 * F_WIDTH
    tc = min(cols, 4096)
    x3 = xf.reshape(B, n1, cols)
    a = pl.pallas_call(
        _fft1_kernel,
        grid=(B, cols // tc),
        in_specs=[pl.BlockSpec((2 * n1, n1), lambda b, j: (0, 0)),
                  pl.BlockSpec((2 * n1, n1), lambda b, j: (0, 0)),
                  pl.BlockSpec((None, n1, tc), lambda b, j: (b, 0, j))],
        out_specs=pl.BlockSpec((None, 2 * n1, tc), lambda b, j: (b, 0, j)),
        out_shape=jax.ShapeDtypeStruct((B, 2 * n1, cols), F32),
        compiler_params=_cparams(("parallel", "parallel")),
        name="fourier_stage1",
    )(*ft["w1"], x3)
    a5 = a.reshape(B, 2, n1, n2, F_WIDTH)
    kb = math.gcd(4, n1)
    table = pl.BlockSpec((F_WIDTH, F_WIDTH), lambda b, k: (0, 0))
    twiddle = pl.BlockSpec((kb, 2 * n2, 2 * n2), lambda b, k: (k, 0, 0))
    y = pl.pallas_call(
        functools.partial(_fft2_kernel, scale=1.0 / math.sqrt(S * F_GROUP_DIM)),
        grid=(B, n1 // kb),
        in_specs=[twiddle, twiddle,
                  pl.BlockSpec((None, 2, kb, n2, F_WIDTH), lambda b, k: (b, 0, k, 0, 0)),
                  table, table, table, table],
        out_specs=pl.BlockSpec((None, kb, n2, F_WIDTH), lambda b, k: (b, k, 0, 0)),
        out_shape=jax.ShapeDtypeStruct((B, n1, n2, F_WIDTH), F32),
        compiler_params=_cparams(("parallel", "parallel")),
        name="fourier_stage2",
    )(*ft["m2"], a5, *ft["cc"], *ft["cs"])
    return jnp.transpose(y, (0, 2, 1, 3)).reshape(B * S, F_WIDTH)


def _gqa_kernel(sink_ref, q_ref, kp_ref, kc_ref, kn_ref, vp_ref, vc_ref, vn_ref, o_ref, *, S):
    n = pl.program_id(1)
    tq, halo = q_ref.shape[0], kp_ref.shape[0]
    width = tq + 2 * halo
    qpos = n * tq + lax.broadcasted_iota(jnp.int32, (tq, width), 0)
    kpos = n * tq - halo + lax.broadcasted_iota(jnp.int32, (tq, width), 1)
    mask = (jnp.abs(qpos - kpos) <= WINDOW) & (kpos >= 0) & (kpos < S)
    for hd in range(GQA_HEADS):
        kh = hd // GQA_GROUP
        qs = slice(hd * LANES, (hd + 1) * LANES)
        ks = slice(kh * LANES, (kh + 1) * LANES)
        q = q_ref[:, qs]
        s = jnp.concatenate([_dot_nt(q, kp_ref[:, ks]), _dot_nt(q, kc_ref[:, ks]),
                             _dot_nt(q, kn_ref[:, ks])], axis=-1)
        s = jnp.where(mask, s, NEG)
        sk = sink_ref[hd]
        m = jnp.maximum(jnp.max(s, axis=-1, keepdims=True), sk)
        p = jnp.exp(s - m)
        den = jnp.sum(p, axis=-1, keepdims=True) + jnp.exp(sk - m)
        pb = (p / den).astype(BF16)
        o = (_dot(pb[:, :halo], vp_ref[:, ks]) + _dot(pb[:, halo:halo + tq], vc_ref[:, ks])
             + _dot(pb[:, halo + tq:], vn_ref[:, ks]))
        o_ref[:, qs] = o.astype(o_ref.dtype)


def _gqa(qg, kg, vg, sink, B, S, tq):
    halo = WINDOW
    per = tq // halo
    nb = S // halo
    q3 = qg.reshape(B, S, GQA_HEADS * LANES)
    k3 = kg.reshape(B, S, GQA_KV_HEADS * LANES)
    v3 = vg.reshape(B, S, GQA_KV_HEADS * LANES)
    kvw = GQA_KV_HEADS * LANES
    prev = pl.BlockSpec((None, halo, kvw), lambda b, n: (b, jnp.maximum(n * per - 1, 0), 0))
    cur = pl.BlockSpec((None, tq, kvw), lambda b, n: (b, n, 0))
    nxt = pl.BlockSpec((None, halo, kvw), lambda b, n: (b, jnp.minimum((n + 1) * per, nb - 1), 0))
    out = pl.pallas_call(
        functools.partial(_gqa_kernel, S=S),
        grid=(B, S // tq),
        in_specs=[pl.BlockSpec(memory_space=pltpu.SMEM),
                  pl.BlockSpec((None, tq, GQA_HEADS * LANES), lambda b, n: (b, n, 0)),
                  prev, cur, nxt, prev, cur, nxt],
        out_specs=pl.BlockSpec((None, tq, GQA_HEADS * LANES), lambda b, n: (b, n, 0)),
        out_shape=jax.ShapeDtypeStruct((B, S, GQA_HEADS * LANES), BF16),
        compiler_params=_cparams(("parallel", "parallel")),
        name="gqa_window",
    )(sink, q3, k3, k3, k3, v3, v3, v3)
    return out.reshape(B * S, GQA_HEADS * LANES)


def _merge_kernel(x_ref, attn_ref, four_ref, og_ref, gate_ref, wo_ref, fw_ref, fb_ref, gwo_ref,
                  wout_ref, o_ref):
    mla_o = _dot(attn_ref[...], wo_ref[...])
    fnet_o = _dot(four_ref[...].astype(BF16), fw_ref[...]) + fb_ref[...]
    gqa_o = _dot(og_ref[...], gwo_ref[...])
    d = D_MODEL
    merged = (gate_ref[:, 0:d] * mla_o + gate_ref[:, d:2 * d] * fnet_o
              + gate_ref[:, 2 * d:3 * d] * gqa_o)
    o_ref[...] = x_ref[...] + _dot(merged.astype(BF16), wout_ref[...])


def _merge(x2, attn, four, og, gate, lw, tm):
    T = x2.shape[0]
    full = lambda shape: pl.BlockSpec(shape, lambda i: (0,) * len(shape))
    row = lambda w: pl.BlockSpec((tm, w), lambda i: (i, 0))
    return pl.pallas_call(
        _merge_kernel,
        grid=(T // tm,),
        in_specs=[row(D_MODEL), row(1024), row(F_WIDTH), row(512), row(3 * D_MODEL),
                  full((1024, D_MODEL)), full((F_WIDTH, D_MODEL)), full((1, D_MODEL)),
                  full((512, D_MODEL)), full((D_MODEL, D_MODEL))],
        out_specs=row(D_MODEL),
        out_shape=jax.ShapeDtypeStruct((T, D_MODEL), F32),
        compiler_params=_cparams(("parallel",)),
        name="merge",
    )(x2, attn, four, og, gate, lw["mla_w_o"], lw["fnet_w"], lw["fnet_b"], lw["gqa_w_o"],
      lw["w_out"])


LOG2E = math.log2(math.e)
PEER_BLK = 256
PEER_RANKS = PEER_TOPK + 1


def _sort_network(n):
    pairs = []

    def merge(lo, m, r):
        step = 2 * r
        if step < m:
            merge(lo, m, step)
            merge(lo + r, m, step)
            pairs.extend((i, i + r) for i in range(lo + r, lo + m - r, step))
        else:
            pairs.append((lo, lo + r))

    def sort(lo, m):
        if m > 1:
            sort(lo, m // 2)
            sort(lo + m // 2, m // 2)
            merge(lo, m, 1)

    sort(0, n)
    return pairs


def _top_of_sorted_lists(lists, k, extra=None):
    lists = list(lists)
    neg = jnp.full(lists[0].shape, -jnp.inf, F32)
    vals = []
    for r in range(k):
        mx = jnp.max(lists[0], axis=0, keepdims=True)
        if extra is not None:
            mx = jnp.maximum(mx, jnp.max(extra, axis=0, keepdims=True))
        vals.append(mx)
        if r == k - 1:
            break
        win = lists[0] == mx
        for v in range(min(len(lists), k - 1 - r)):
            lists[v] = jnp.where(win, lists[v + 1] if v + 1 < len(lists) else neg, lists[v])
        if extra is not None:
            extra = jnp.where(extra == mx, -jnp.inf, extra)
    return vals


def _top_sorted(s, k):
    lists = [s[8 * v:8 * v + 8] for v in range(s.shape[0] // 8)]
    for i, j in _sort_network(len(lists)):
        lists[i], lists[j] = jnp.maximum(lists[i], lists[j]), jnp.minimum(lists[i], lists[j])
    return _top_of_sorted_lists(lists, k)


def _peer_prep_kernel(x_ref, g_ref, wq_ref, keys_ref, hn_ref, a1_ref, th_ref, e2_ref):
    hn = _rms(x_ref[...], g_ref[...])
    hn_ref[...] = hn.T.astype(BF16)
    q = _dot(hn.astype(BF16), wq_ref[...]).astype(BF16)
    tm = q.shape[0]
    for hd in range(PEER_HEADS):
        sts, tops = [], []
        for p in range(2):
            hp = 2 * hd + p
            st = _dot_nt(keys_ref[hp], q[:, hp * PEER_HALF:(hp + 1) * PEER_HALF])
            sts.append(st)
            tops.append(_top_sorted(st, PEER_RANKS))
        a, b = tops
        a_lo = jnp.concatenate(a[:8], axis=0)
        a_hi = jnp.concatenate(a[8:] + [jnp.full((7, tm), -jnp.inf, F32)], axis=0)
        top = _top_of_sorted_lists([a_lo + b[r] for r in range(PEER_RANKS)], PEER_RANKS,
                                   extra=a_hi + b[0])
        m = top[0]
        log_z = jnp.log(sum(jnp.exp(t - m) for t in top[:PEER_TOPK]))
        thr = 0.5 * (top[PEER_TOPK - 1] + top[PEER_TOPK])
        st1 = (sts[0] - a[0]) * LOG2E
        s2k = (sts[1] - b[0] - log_z) * LOG2E
        a1_ref[hd] = jnp.exp2(st1)
        th_ref[hd] = jnp.exp2((thr - m - log_z) * LOG2E - st1 - 1.0)
        e2 = jnp.exp2(s2k - 1.0)
        for c in range(tm // LANES):
            e2_ref[hd, c] = e2[:, c * LANES:(c + 1) * LANES]


def _peer_prep(x2, lw, tm):
    T = x2.shape[0]
    sc = pl.BlockSpec((PEER_HEADS, PEER_N_KEYS, tm), lambda i: (0, 0, i))
    return pl.pallas_call(
        _peer_prep_kernel,
        grid=(T // tm,),
        in_specs=[pl.BlockSpec((tm, D_MODEL), lambda i: (i, 0)),
                  pl.BlockSpec((1, D_MODEL), lambda i: (0, 0)),
                  pl.BlockSpec((D_MODEL, 2 * PEER_HEADS * PEER_HALF), lambda i: (0, 0)),
                  pl.BlockSpec((2 * PEER_HEADS, PEER_N_KEYS, PEER_HALF), lambda i: (0, 0, 0))],
        out_specs=[pl.BlockSpec((D_MODEL, tm), lambda i: (0, i)), sc, sc,
                   pl.BlockSpec((PEER_HEADS, tm // LANES, PEER_N_KEYS, LANES),
                                lambda i: (0, i, 0, 0))],
        out_shape=[jax.ShapeDtypeStruct((D_MODEL, T), BF16)]
        + [jax.ShapeDtypeStruct((PEER_HEADS, PEER_N_KEYS, T), F32)] * 2
        + [jax.ShapeDtypeStruct((PEER_HEADS, T // LANES, PEER_N_KEYS, LANES), F32)],
        compiler_params=_cparams(("parallel",)),
        name="peer_prep",
    )(x2, lw["norm_ffn_g"], lw["peer_w_q"], lw["peer_keys"])


def _peer_main_kernel(x_ref, hn_ref, a1_ref, th_ref, e2_ref, u_ref, vt_ref, fg_ref, o_ref,
                      acc_ref, act_ref, w_ref, *, n_i, final_norm):
    e = pl.program_id(1)
    tm = hn_ref.shape[1]

    @pl.when(e == 0)
    def _():
        acc_ref[...] = jnp.zeros_like(acc_ref)

    n_blk = n_i * PEER_N_KEYS // PEER_BLK
    per_blk = PEER_BLK // PEER_N_KEYS

    n_strip = tm // LANES

    def store_activations(p, slot):
        a = _dot(u_ref[p * PEER_BLK:(p + 1) * PEER_BLK, :], hn_ref[...])
        act = a * (1.0 + lax.erf(a * (1.0 / math.sqrt(2.0))))
        for c in range(n_strip):
            act_ref[slot, c] = act[:, c * LANES:(c + 1) * LANES]

    def fold(p, slot):
        w = jnp.concatenate([w_ref[slot, c] for c in range(n_strip)], axis=1)
        acc_ref[...] += _dot(vt_ref[p], w)

    store_activations(0, 0)
    for p in range(n_blk):
        slot = p % 2
        if p + 1 < n_blk:
            store_activations(p + 1, 1 - slot)
        if p > 0:
            fold(p - 1, 1 - slot)
        for ii in range(per_blk):
            i = p * per_blk + ii
            rows = slice(ii * PEER_N_KEYS, (ii + 1) * PEER_N_KEYS)
            for c in range(n_strip):
                ls = slice(c * LANES, (c + 1) * LANES)
                g = None
                for hd in range(PEER_HEADS):
                    e2 = e2_ref[hd, c]
                    t = jnp.where(e2 > th_ref[hd, i:i + 1, ls], e2, 0.0) * a1_ref[hd, i:i + 1, ls]
                    g = t if g is None else g + t
                w_ref[slot, c, rows, :] = (g * act_ref[slot, c, rows, :]).astype(BF16)
    fold(n_blk - 1, (n_blk - 1) % 2)

    @pl.when(e == pl.num_programs(1) - 1)
    def _():
        y = x_ref[...] + acc_ref[...].T
        if final_norm:
            y = _rms(y, fg_ref[...])
        o_ref[...] = y


def _peer_main(x2, hn, a1, th, e2, lw, final_g, tm, n_i, final_norm):
    T = x2.shape[0]
    et = n_i * PEER_N_KEYS
    sc = pl.BlockSpec((PEER_HEADS, PEER_N_KEYS, tm), lambda t, e: (0, 0, t))
    return pl.pallas_call(
        functools.partial(_peer_main_kernel, n_i=n_i, final_norm=final_norm),
        grid=(T // tm, PEER_N_EXPERTS // et),
        in_specs=[pl.BlockSpec((tm, D_MODEL), lambda t, e: (t, 0)),
                  pl.BlockSpec((D_MODEL, tm), lambda t, e: (0, t)),
                  pl.BlockSpec((PEER_HEADS, n_i, tm), lambda t, e: (0, e, t)),
                  pl.BlockSpec((PEER_HEADS, n_i, tm), lambda t, e: (0, e, t)),
                  pl.BlockSpec((PEER_HEADS, tm // LANES, PEER_N_KEYS, LANES),
                               lambda t, e: (0, t, 0, 0)),
                  pl.BlockSpec((et, D_MODEL), lambda t, e: (e, 0)),
                  pl.BlockSpec((et // PEER_BLK, D_MODEL, PEER_BLK), lambda t, e: (e, 0, 0)),
                  pl.BlockSpec((1, D_MODEL), lambda t, e: (0, 0))],
        out_specs=pl.BlockSpec((tm, D_MODEL), lambda t, e: (t, 0)),
        out_shape=jax.ShapeDtypeStruct((T, D_MODEL), F32),
        scratch_shapes=[pltpu.VMEM((D_MODEL, tm), F32),
                        pltpu.VMEM((2, tm // LANES, PEER_BLK, LANES), F32),
                        pltpu.VMEM((2, tm // LANES, PEER_BLK, LANES), BF16)],
        compiler_params=_cparams(("parallel", "arbitrary")),
        name="peer_main",
    )(x2, hn, a1, th, e2, lw["peer_u"], lw["peer_vt"], final_g)


def _pad_cols(w, left, total):
    return jnp.pad(w, ((0, 0), (left, total - left - w.shape[-1])))


def _head_groups(w, n_heads, width, left=0):
    return jnp.concatenate(
        [_pad_cols(w[:, h * width:(h + 1) * width], left, LANES) for h in range(n_heads)], axis=-1)


def _layer_weights(p, l):
    w_in = p["w_in"][l]
    widths = (MLA_Q_LORA, MLA_KV_LORA, MLA_ROPE, F_WIDTH, GQA_HEADS * GQA_HEAD_DIM,
              GQA_KV_HEADS * GQA_HEAD_DIM, GQA_KV_HEADS * GQA_HEAD_DIM, 3 * D_MODEL)
    offs = np.cumsum((0,) + widths)
    wcq, wckv, wkr, wxf, wgq, wgk, wgv, wgate = (w_in[:, offs[i]:offs[i + 1]] for i in range(8))

    w1 = jnp.concatenate([
        wcq, wckv,
        _pad_cols(wkr, MLA_NOPE, LANES),
        wxf,
        _head_groups(wgq, GQA_HEADS, GQA_HEAD_DIM),
        _head_groups(wgk, GQA_KV_HEADS, GQA_HEAD_DIM),
        _head_groups(wgv, GQA_KV_HEADS, GQA_HEAD_DIM),
        wgate], axis=-1).astype(BF16)
    assert w1.shape[-1] == _W1_COLS

    w_uq = p["mla_w_uq"][l]
    qd = MLA_NOPE + MLA_ROPE
    uq = [_pad_cols(w_uq[:, h * qd:(h + 1) * qd], 0, LANES) for h in range(MLA_HEADS)]
    w_ukv = p["mla_w_ukv"][l]
    kd = MLA_NOPE + MLA_V
    uk = [_pad_cols(w_ukv[:, h * kd:h * kd + MLA_NOPE], 0, LANES) for h in range(MLA_HEADS)]
    uv = [_pad_cols(w_ukv[:, h * kd + MLA_NOPE:(h + 1) * kd], 0, LANES) for h in range(MLA_HEADS)]

    def pad_rows(w, n_heads, width):
        return jnp.concatenate(
            [jnp.pad(w[h * width:(h + 1) * width], ((0, LANES - width), (0, 0)))
             for h in range(n_heads)], axis=0)

    return {
        "norm_mix_g": p["norm_mix_g"][l][None, :],
        "w1": w1,
        "q_norm_g": p["mla_q_norm_g"][l][None, :],
        "w_uq": jnp.concatenate(uq, axis=-1).astype(BF16),
        "kv_norm_g": p["mla_kv_norm_g"][l][None, :],
        "w_ukv": jnp.concatenate(uk + uv, axis=-1).astype(BF16),
        "mla_w_o": pad_rows(p["mla_w_o"][l], MLA_HEADS, MLA_V).astype(BF16),
        "fnet_w": p["fnet_w"][l].astype(BF16),
        "fnet_b": p["fnet_b"][l][None, :],
        "gqa_sink": p["gqa_sink"][l],
        "gqa_w_o": pad_rows(p["gqa_w_o"][l], GQA_HEADS, GQA_HEAD_DIM).astype(BF16),
        "w_out": p["w_out"][l].astype(BF16),
        "norm_ffn_g": p["norm_ffn_g"][l][None, :],
        "peer_w_q": p["peer_w_q"][l].astype(BF16),
        "peer_keys": p["peer_keys"][l].reshape(2 * PEER_HEADS, PEER_N_KEYS, PEER_HALF).astype(BF16),
        "peer_u": p["peer_u"][l].astype(BF16),
        "peer_vt": p["peer_v"][l].astype(BF16).reshape(
            PEER_N_EXPERTS // PEER_BLK, PEER_BLK, D_MODEL).transpose(0, 2, 1),
    }


def _rope_tables(S):
    pos = jnp.arange(S, dtype=F32)[:, None]

    def tables(d, left, passthrough):
        inv = 1.0 / (ROPE_THETA ** (jnp.arange(0, d, 2, dtype=F32) / d))
        ang = pos * inv[None, :]
        cos, sin, zero = jnp.cos(ang), jnp.sin(ang), jnp.zeros_like(ang)
        pad = lambda parts: _pad_cols(jnp.concatenate(parts, axis=-1), left, LANES)
        lead = jnp.pad(jnp.ones((S, passthrough), F32), ((0, 0), (0, LANES - passthrough)))
        return pad([cos, cos]) + lead, pad([-sin, zero]), pad([zero, sin])

    cm, sm_lo, sm_hi = tables(MLA_ROPE, MLA_NOPE, MLA_NOPE)
    cg, sg_lo, sg_hi = tables(GQA_HEAD_DIM, 0, 0)
    return {"cm": cm, "sm_lo": sm_lo, "sm_hi": sm_hi, "cg": cg, "sg_lo": sg_lo, "sg_hi": sg_hi}


def _fourier_tables(S):
    n2 = 128
    n1 = S // n2

    def cos_sin(num, den):
        ang = (2.0 * math.pi / den) * (num % den).astype(F32)
        return jnp.cos(ang), jnp.sin(ang)

    i1 = jnp.arange(n1, dtype=jnp.int32)
    c1, s1 = cos_sin(i1[:, None] * i1[None, :], n1)
    w1 = jnp.concatenate([c1, -s1], axis=0)
    i2 = jnp.arange(n2, dtype=jnp.int32)
    kk = i1[:, None, None] + n1 * i2[None, :, None]
    mr, ms = cos_sin(kk * i2[None, None, :], S)
    m2 = jnp.concatenate([jnp.concatenate([mr, ms], axis=-1),
                          jnp.concatenate([-ms, mr], axis=-1)], axis=1)
    ic = jnp.arange(F_GROUP_DIM, dtype=jnp.int32)
    cc, cs = cos_sin(ic[:, None] * ic[None, :], F_GROUP_DIM)
    eye = jnp.eye(F_GROUPS, dtype=F32)
    return {"n1": n1, "n2": n2, "w1": _split(w1), "m2": _split(m2),
            "cc": _split(jnp.kron(eye, cc)), "cs": _split(jnp.kron(eye, cs))}


def _pick(S, pref):
    t = pref
    while S % t:
        t //= 2
    return t


def _trunk(x, layers, final_g, rope_t, four_t):
    B, S, _ = x.shape
    T = B * S
    x2 = x.reshape(T, D_MODEL)
    tm = _pick(S, 256)
    n_layers = len(layers)
    for l, lw in enumerate(layers):
        q, k, v, xf, qg, kg, vg, gate = _in_proj(x2, lw, rope_t, S, tm)
        attn = _mla_attn(q, k, v, B, S, _pick(S, 1024), _pick(S // 2, 512))
        four = _fourier(xf, four_t, B, S)
        og = _gqa(qg, kg, vg, lw["gqa_sink"], B, S, _pick(S, 512))
        x2 = _merge(x2, attn, four, og, gate, lw, tm)
        hn, a1, th, e2 = _peer_prep(x2, lw, _pick(T, 256))
        x2 = _peer_main(x2, hn, a1, th, e2, lw, final_g, _pick(T, 512), 32,
                        final_norm=(l == n_layers - 1))
    return x2.reshape(B, S, D_MODEL)


def kernel(x_prompt, x_sample, norm_mix_g, w_in, mla_q_norm_g, mla_w_uq, mla_kv_norm_g, mla_w_ukv, mla_w_o, fnet_w, fnet_b, gqa_sink, gqa_w_o, w_out, norm_ffn_g, peer_w_q, peer_keys, peer_u, peer_v, final_norm_g):
    p = dict(norm_mix_g=norm_mix_g, w_in=w_in, mla_q_norm_g=mla_q_norm_g, mla_w_uq=mla_w_uq,
             mla_kv_norm_g=mla_kv_norm_g, mla_w_ukv=mla_w_ukv, mla_w_o=mla_w_o, fnet_w=fnet_w,
             fnet_b=fnet_b, gqa_sink=gqa_sink, gqa_w_o=gqa_w_o, w_out=w_out,
             norm_ffn_g=norm_ffn_g, peer_w_q=peer_w_q, peer_keys=peer_keys, peer_u=peer_u,
             peer_v=peer_v)
    layers = [_layer_weights(p, l) for l in range(w_in.shape[0])]
    final_g = final_norm_g[None, :]
    outs = []
    for x in (x_prompt, x_sample):
        S = x.shape[1]
        outs.append(_trunk(x, layers, final_g, _rope_tables(S), _fourier_tables(S)))
    return tuple(outs)
```

```python
import functools
import math

import jax
import jax.numpy as jnp
import numpy as np
from jax import lax
from jax.experimental import pallas as pl
from jax.experimental.pallas import tpu as pltpu

D_MODEL = 1024
EPS = 1e-6
ROPE_THETA = 10000.0
NEG = -1e30
LANES = 128

MLA_HEADS = 8
MLA_NOPE = 64
MLA_ROPE = 32
MLA_V = 64
MLA_Q_LORA = 384
MLA_KV_LORA = 256
F_GROUPS = 4
F_GROUP_DIM = 64
F_WIDTH = F_GROUPS * F_GROUP_DIM
GQA_HEADS = 4
GQA_KV_HEADS = 2
GQA_GROUP = GQA_HEADS // GQA_KV_HEADS
GQA_HEAD_DIM = 64
WINDOW = 128
PEER_HEADS = 8
PEER_N_KEYS = 128
PEER_N_EXPERTS = PEER_N_KEYS * PEER_N_KEYS
PEER_HALF = 128
PEER_TOPK = 16

VMEM_LIMIT = 56 * 1024 * 1024

BF16 = jnp.bfloat16
F32 = jnp.float32


def _cparams(sem):
    return pltpu.CompilerParams(dimension_semantics=sem, vmem_limit_bytes=VMEM_LIMIT)


def _rms(x, g):
    return x * lax.rsqrt(jnp.mean(x * x, axis=-1, keepdims=True) + EPS) * g


def _dot(a, b):
    return jnp.dot(a, b, preferred_element_type=F32)


def _dot_nt(a, b):
    return lax.dot_general(a, b, (((1,), (1,)), ((), ())), preferred_element_type=F32)


_C_CQ = (0, 384)
_C_CKV = (384, 640)
_C_KR = (640, 768)
_C_XF = (768, 1024)
_C_GQ = (1024, 1536)
_C_GK = (1536, 1792)
_C_GV = (1792, 2048)
_C_GATE = (2048, 5120)
_W1_COLS = 5120


def _rotary(x, half, cos, sin_lo, sin_hi):
    n = x.shape[1]
    up = pltpu.roll(x, n - half, axis=1)
    dn = pltpu.roll(x, half, axis=1)
    out = []
    for g in range(n // LANES):
        sl = slice(g * LANES, (g + 1) * LANES)
        out.append(x[:, sl] * cos + up[:, sl] * sin_lo + dn[:, sl] * sin_hi)
    return out


def _in_proj_kernel(x_ref, g_ref, w1_ref, qng_ref, wuq_ref, kvng_ref, wukv_ref,
                    cm_ref, sml_ref, smh_ref, cg_ref, sgl_ref, sgh_ref,
                    q_ref, k_ref, v_ref, xf_ref, qg_ref, kg_ref, vg_ref, gate_ref):
    h = _rms(x_ref[...], g_ref[...]).astype(BF16)

    def proj(c):
        return _dot(h, w1_ref[:, c[0]:c[1]])

    mla_tabs = (cm_ref[...], sml_ref[...], smh_ref[...])
    gqa_tabs = (cg_ref[...], sgl_ref[...], sgh_ref[...])

    cqn = _rms(proj(_C_CQ), qng_ref[...]).astype(BF16)
    q = _rotary(_dot(cqn, wuq_ref[...]), MLA_ROPE // 2, *mla_tabs)
    mla_scale = (MLA_NOPE + MLA_ROPE) ** -0.5 * math.log2(math.e)
    for hd in range(MLA_HEADS):
        q_ref[:, hd * LANES:(hd + 1) * LANES] = (q[hd] * mla_scale).astype(BF16)

    ckvn = _rms(proj(_C_CKV), kvng_ref[...]).astype(BF16)
    kv = _dot(ckvn, wukv_ref[...])
    k_rope, = _rotary(proj(_C_KR), MLA_ROPE // 2, *mla_tabs)
    for hd in range(MLA_HEADS):
        sl = slice(hd * LANES, (hd + 1) * LANES)
        k_ref[:, sl] = (kv[:, sl] + k_rope).astype(BF16)
    lane = lax.broadcasted_iota(jnp.int32, (1, MLA_HEADS * LANES), 1)
    ones_cols = (lane % LANES == MLA_V).astype(F32)
    v_ref[...] = (kv[:, MLA_HEADS * LANES:] + ones_cols).T.astype(BF16)

    xf_ref[...] = proj(_C_XF)

    gq = _rotary(proj(_C_GQ), GQA_HEAD_DIM // 2, *gqa_tabs)
    gqa_scale = GQA_HEAD_DIM ** -0.5
    for hd in range(GQA_HEADS):
        qg_ref[:, hd * LANES:(hd + 1) * LANES] = (gq[hd] * gqa_scale).astype(BF16)
    gk = _rotary(proj(_C_GK), GQA_HEAD_DIM // 2, *gqa_tabs)
    for hd in range(GQA_KV_HEADS):
        kg_ref[:, hd * LANES:(hd + 1) * LANES] = gk[hd].astype(BF16)
    vg_ref[...] = proj(_C_GV).astype(BF16)

    gate = proj(_C_GATE)
    gate_ref[...] = (1.0 / (1.0 + jnp.exp(-gate))).astype(gate_ref.dtype)


def _in_proj(x2, lw, tabs, S, tm):
    T = x2.shape[0]
    nt = T // tm
    ns = S // tm
    full = lambda shape: pl.BlockSpec(shape, lambda i: (0,) * len(shape))
    row = lambda w: pl.BlockSpec((tm, w), lambda i: (i, 0))
    tab = pl.BlockSpec((tm, LANES), lambda i: (i % ns, 0))
    outs = [(1024, BF16), (1024, BF16), (1024, BF16), (F_WIDTH, F32), (512, BF16), (256, BF16),
            (256, BF16), (3 * D_MODEL, BF16)]
    return pl.pallas_call(
        _in_proj_kernel,
        grid=(nt,),
        in_specs=[row(D_MODEL), full((1, D_MODEL)), full((D_MODEL, _W1_COLS)),
                  full((1, MLA_Q_LORA)), full((MLA_Q_LORA, 1024)),
                  full((1, MLA_KV_LORA)), full((MLA_KV_LORA, 2048)),
                  tab, tab, tab, tab, tab, tab],
        out_specs=[pl.BlockSpec((w, tm), lambda i: (0, i)) if n == 2 else row(w)
                   for n, (w, _) in enumerate(outs)],
        out_shape=[jax.ShapeDtypeStruct((w, T) if n == 2 else (T, w), dt)
                   for n, (w, dt) in enumerate(outs)],
        compiler_params=_cparams(("parallel",)),
        name="in_proj",
    )(x2, lw["norm_mix_g"], lw["w1"], lw["q_norm_g"], lw["w_uq"], lw["kv_norm_g"], lw["w_ukv"],
      tabs["cm"], tabs["sm_lo"], tabs["sm_hi"], tabs["cg"], tabs["sg_lo"], tabs["sg_hi"])


def _mla_attn_kernel(q_ref, k_ref, vt_ref, o_ref, s_ref, p_ref, *, tk):
    tq = q_ref.shape[0]
    nk = k_ref.shape[0] // tk
    q = q_ref[...]

    def k_chunk(c):
        return k_ref[c * tk:(c + 1) * tk, :]

    def vt_chunk(c):
        return vt_ref[:, c * tk:(c + 1) * tk]

    def step(slot, vt_prev, k_next, carry):
        m, alpha, acc = carry
        pv = _dot(vt_prev, p_ref[1 - slot])
        s_ref[1 - slot] = _dot_nt(k_next, q)
        s = s_ref[slot]
        m_new = jnp.maximum(m, jnp.max(s, axis=0, keepdims=True))
        p_ref[slot] = jnp.exp2(s - m_new).astype(BF16)
        return m_new, jnp.exp2(m - m_new), acc * alpha + pv

    s_ref[0] = _dot_nt(k_chunk(0), q)
    p_ref[1] = jnp.zeros(p_ref.shape[1:], BF16)
    carry = (jnp.full((1, tq), NEG, F32), jnp.ones((1, tq), F32), jnp.zeros((LANES, tq), F32))
    for c in range(nk):
        carry = step(c % 2, vt_chunk(max(c - 1, 0)), k_chunk(min(c + 1, nk - 1)), carry)
    _, alpha, acc = carry
    acc = acc * alpha + _dot(vt_chunk(nk - 1), p_ref[1])
    o_ref[...] = (acc / acc[MLA_V:MLA_V + 1, :]).T.astype(o_ref.dtype)


def _mla_attn(q, k, vt, B, S, tq, tk):
    assert (S // tk) % 2 == 0
    q3, k3 = (a.reshape(B, S, MLA_HEADS * LANES) for a in (q, k))
    out = pl.pallas_call(
        functools.partial(_mla_attn_kernel, tk=tk),
        scratch_shapes=[pltpu.VMEM((2, tk, tq), F32), pltpu.VMEM((2, tk, tq), BF16)],
        grid=(B, MLA_HEADS, S // tq),
        in_specs=[pl.BlockSpec((None, tq, LANES), lambda b, h, i: (b, i, h)),
                  pl.BlockSpec((None, S, LANES), lambda b, h, i: (b, 0, h)),
                  pl.BlockSpec((LANES, S), lambda b, h, i: (h, b))],
        out_specs=pl.BlockSpec((None, tq, LANES), lambda b, h, i: (b, i, h)),
        out_shape=jax.ShapeDtypeStruct((B, S, MLA_HEADS * LANES), BF16),
        compiler_params=_cparams(("parallel", "parallel", "parallel")),
        name="mla_attn",
    )(q3, k3, vt)
    return out.reshape(B * S, MLA_HEADS * LANES)


def _split(x):
    hi = x.astype(BF16)
    return hi, (x - hi.astype(F32)).astype(BF16)


def _dot3(a, b):
    (ah, al), (bh, bl) = a, b
    return _dot(ah, bh) + (_dot(ah, bl) + _dot(al, bh))


def _fft1_kernel(wh_ref, wl_ref, x_ref, o_ref):
    o_ref[...] = _dot3((wh_ref[...], wl_ref[...]), _split(x_ref[...]))


def _fft2_kernel(mh_ref, ml_ref, a_ref, cch_ref, ccl_ref, csh_ref, csl_ref, o_ref, *, scale):
    cc = (cch_ref[...], ccl_ref[...])
    cs = (csh_ref[...], csl_ref[...])
    for j in range(o_ref.shape[0]):
        aa = jnp.concatenate([a_ref[0, j], a_ref[1, j]], axis=0)
        y = _dot3((mh_ref[j], ml_ref[j]), _split(aa))
        n2 = y.shape[0] // 2
        o_ref[j] = (_dot3(_split(y[:n2]), cc) + _dot3(_split(y[n2:]), cs)) * scale


def _fourier(xf, ft, B, S):
    n1, n2 = ft["n1"], ft["n2"]
    cols = n2 * F_WIDTH
    tc = min(cols, 4096)
    x3 = xf.reshape(B, n1, cols)
    a = pl.pallas_call(
        _fft1_kernel,
        grid=(B, cols // tc),
        in_specs=[pl.BlockSpec((2 * n1, n1), lambda b, j: (0, 0)),
                  pl.BlockSpec((2 * n1, n1), lambda b, j: (0, 0)),
                  pl.BlockSpec((None, n1, tc), lambda b, j: (b, 0, j))],
        out_specs=pl.BlockSpec((None, 2 * n1, tc), lambda b, j: (b, 0, j)),
        out_shape=jax.ShapeDtypeStruct((B, 2 * n1, cols), F32),
        compiler_params=_cparams(("parallel", "parallel")),
        name="fourier_stage1",
    )(*ft["w1"], x3)
    a5 = a.reshape(B, 2, n1, n2, F_WIDTH)
    kb = math.gcd(4, n1)
    table = pl.BlockSpec((F_WIDTH, F_WIDTH), lambda b, k: (0, 0))
    twiddle = pl.BlockSpec((kb, 2 * n2, 2 * n2), lambda b, k: (k, 0, 0))
    y = pl.pallas_call(
        functools.partial(_fft2_kernel, scale=1.0 / math.sqrt(S * F_GROUP_DIM)),
        grid=(B, n1 // kb),
        in_specs=[twiddle, twiddle,
                  pl.BlockSpec((None, 2, kb, n2, F_WIDTH), lambda b, k: (b, 0, k, 0, 0)),
                  table, table, table, table],
        out_specs=pl.BlockSpec((None, kb, n2, F_WIDTH), lambda b, k: (b, k, 0, 0)),
        out_shape=jax.ShapeDtypeStruct((B, n1, n2, F_WIDTH), F32),
        compiler_params=_cparams(("parallel", "parallel")),
        name="fourier_stage2",
    )(*ft["m2"], a5, *ft["cc"], *ft["cs"])
    return jnp.transpose(y, (0, 2, 1, 3)).reshape(B * S, F_WIDTH)


def _gqa_kernel(sink_ref, q_ref, kp_ref, kc_ref, kn_ref, vp_ref, vc_ref, vn_ref, o_ref, *, S):
    n = pl.program_id(1)
    tq, halo = q_ref.shape[0], kp_ref.shape[0]
    width = tq + 2 * halo
    qpos = n * tq + lax.broadcasted_iota(jnp.int32, (tq, width), 0)
    kpos = n * tq - halo + lax.broadcasted_iota(jnp.int32, (tq, width), 1)
    mask = (jnp.abs(qpos - kpos) <= WINDOW) & (kpos >= 0) & (kpos < S)
    for hd in range(GQA_HEADS):
        kh = hd // GQA_GROUP
        qs = slice(hd * LANES, (hd + 1) * LANES)
        ks = slice(kh * LANES, (kh + 1) * LANES)
        q = q_ref[:, qs]
        s = jnp.concatenate([_dot_nt(q, kp_ref[:, ks]), _dot_nt(q, kc_ref[:, ks]),
                             _dot_nt(q, kn_ref[:, ks])], axis=-1)
        s = jnp.where(mask, s, NEG)
        sk = sink_ref[hd]
        m = jnp.maximum(jnp.max(s, axis=-1, keepdims=True), sk)
        p = jnp.exp(s - m)
        den = jnp.sum(p, axis=-1, keepdims=True) + jnp.exp(sk - m)
        pb = (p / den).astype(BF16)
        o = (_dot(pb[:, :halo], vp_ref[:, ks]) + _dot(pb[:, halo:halo + tq], vc_ref[:, ks])
             + _dot(pb[:, halo + tq:], vn_ref[:, ks]))
        o_ref[:, qs] = o.astype(o_ref.dtype)


def _gqa(qg, kg, vg, sink, B, S, tq):
    halo = WINDOW
    per = tq // halo
    nb = S // halo
    q3 = qg.reshape(B, S, GQA_HEADS * LANES)
    k3 = kg.reshape(B, S, GQA_KV_HEADS * LANES)
    v3 = vg.reshape(B, S, GQA_KV_HEADS * LANES)
    kvw = GQA_KV_HEADS * LANES
    prev = pl.BlockSpec((None, halo, kvw), lambda b, n: (b, jnp.maximum(n * per - 1, 0), 0))
    cur = pl.BlockSpec((None, tq, kvw), lambda b, n: (b, n, 0))
    nxt = pl.BlockSpec((None, halo, kvw), lambda b, n: (b, jnp.minimum((n + 1) * per, nb - 1), 0))
    out = pl.pallas_call(
        functools.partial(_gqa_kernel, S=S),
        grid=(B, S // tq),
        in_specs=[pl.BlockSpec(memory_space=pltpu.SMEM),
                  pl.BlockSpec((None, tq, GQA_HEADS * LANES), lambda b, n: (b, n, 0)),
                  prev, cur, nxt, prev, cur, nxt],
        out_specs=pl.BlockSpec((None, tq, GQA_HEADS * LANES), lambda b, n: (b, n, 0)),
        out_shape=jax.ShapeDtypeStruct((B, S, GQA_HEADS * LANES), BF16),
        compiler_params=_cparams(("parallel", "parallel")),
        name="gqa_window",
    )(sink, q3, k3, k3, k3, v3, v3, v3)
    return out.reshape(B * S, GQA_HEADS * LANES)


def _merge_kernel(x_ref, attn_ref, four_ref, og_ref, gate_ref, wo_ref, fw_ref, fb_ref, gwo_ref,
                  wout_ref, o_ref):
    mla_o = _dot(attn_ref[...], wo_ref[...])
    fnet_o = _dot(four_ref[...].astype(BF16), fw_ref[...]) + fb_ref[...]
    gqa_o = _dot(og_ref[...], gwo_ref[...])
    d = D_MODEL
    merged = (gate_ref[:, 0:d] * mla_o + gate_ref[:, d:2 * d] * fnet_o
              + gate_ref[:, 2 * d:3 * d] * gqa_o)
    o_ref[...] = x_ref[...] + _dot(merged.astype(BF16), wout_ref[...])


def _merge(x2, attn, four, og, gate, lw, tm):
    T = x2.shape[0]
    full = lambda shape: pl.BlockSpec(shape, lambda i: (0,) * len(shape))
    row = lambda w: pl.BlockSpec((tm, w), lambda i: (i, 0))
    return pl.pallas_call(
        _merge_kernel,
        grid=(T // tm,),
        in_specs=[row(D_MODEL), row(1024), row(F_WIDTH), row(512), row(3 * D_MODEL),
                  full((1024, D_MODEL)), full((F_WIDTH, D_MODEL)), full((1, D_MODEL)),
                  full((512, D_MODEL)), full((D_MODEL, D_MODEL))],
        out_specs=row(D_MODEL),
        out_shape=jax.ShapeDtypeStruct((T, D_MODEL), F32),
        compiler_params=_cparams(("parallel",)),
        name="merge",
    )(x2, attn, four, og, gate, lw["mla_w_o"], lw["fnet_w"], lw["fnet_b"], lw["gqa_w_o"],
      lw["w_out"])


LOG2E = math.log2(math.e)
PEER_BLK = 256
PEER_RANKS = PEER_TOPK + 1


def _sort_network(n):
    pairs = []

    def merge(lo, m, r):
        step = 2 * r
        if step < m:
            merge(lo, m, step)
            merge(lo + r, m, step)
            pairs.extend((i, i + r) for i in range(lo + r, lo + m - r, step))
        else:
            pairs.append((lo, lo + r))

    def sort(lo, m):
        if m > 1:
            sort(lo, m // 2)
            sort(lo + m // 2, m // 2)
            merge(lo, m, 1)

    sort(0, n)
    return pairs


def _top_of_sorted_lists(lists, k, extra=None):
    lists = list(lists)
    neg = jnp.full(lists[0].shape, -jnp.inf, F32)
    vals = []
    for r in range(k):
        mx = jnp.max(lists[0], axis=0, keepdims=True)
        if extra is not None:
            mx = jnp.maximum(mx, jnp.max(extra, axis=0, keepdims=True))
        vals.append(mx)
        if r == k - 1:
            break
        win = lists[0] == mx
        for v in range(min(len(lists), k - 1 - r)):
            lists[v] = jnp.where(win, lists[v + 1] if v + 1 < len(lists) else neg, lists[v])
        if extra is not None:
            extra = jnp.where(extra == mx, -jnp.inf, extra)
    return vals


def _top_sorted(s, k):
    lists = [s[8 * v:8 * v + 8] for v in range(s.shape[0] // 8)]
    for i, j in _sort_network(len(lists)):
        lists[i], lists[j] = jnp.maximum(lists[i], lists[j]), jnp.minimum(lists[i], lists[j])
    return _top_of_sorted_lists(lists, k)


def _peer_prep_kernel(x_ref, g_ref, wq_ref, keys_ref, hn_ref, a1_ref, th_ref, e2_ref):
    hn = _rms(x_ref[...], g_ref[...])
    hn_ref[...] = hn.T.astype(BF16)
    q = _dot(hn.astype(BF16), wq_ref[...]).astype(BF16)
    tm = q.shape[0]
    for hd in range(PEER_HEADS):
        sts, tops = [], []
        for p in range(2):
            hp = 2 * hd + p
            st = _dot_nt(keys_ref[hp], q[:, hp * PEER_HALF:(hp + 1) * PEER_HALF])
            sts.append(st)
            tops.append(_top_sorted(st, PEER_RANKS))
        a, b = tops
        a_lo = jnp.concatenate(a[:8], axis=0)
        a_hi = jnp.concatenate(a[8:] + [jnp.full((7, tm), -jnp.inf, F32)], axis=0)
        top = _top_of_sorted_lists([a_lo + b[r] for r in range(PEER_RANKS)], PEER_RANKS,
                                   extra=a_hi + b[0])
        m = top[0]
        log_z = jnp.log(sum(jnp.exp(t - m) for t in top[:PEER_TOPK]))
        thr = 0.5 * (top[PEER_TOPK - 1] + top[PEER_TOPK])
        st1 = (sts[0] - a[0]) * LOG2E
        s2k = (sts[1] - b[0] - log_z) * LOG2E
        a1_ref[hd] = jnp.exp2(st1)
        th_ref[hd] = jnp.exp2((thr - m - log_z) * LOG2E - st1 - 1.0)
        e2 = jnp.exp2(s2k - 1.0)
        for c in range(tm // LANES):
            e2_ref[hd, c] = e2[:, c * LANES:(c + 1) * LANES]


def _peer_prep(x2, lw, tm):
    T = x2.shape[0]
    sc = pl.BlockSpec((PEER_HEADS, PEER_N_KEYS, tm), lambda i: (0, 0, i))
    return pl.pallas_call(
        _peer_prep_kernel,
        grid=(T // tm,),
        in_specs=[pl.BlockSpec((tm, D_MODEL), lambda i: (i, 0)),
                  pl.BlockSpec((1, D_MODEL), lambda i: (0, 0)),
                  pl.BlockSpec((D_MODEL, 2 * PEER_HEADS * PEER_HALF), lambda i: (0, 0)),
                  pl.BlockSpec((2 * PEER_HEADS, PEER_N_KEYS, PEER_HALF), lambda i: (0, 0, 0))],
        out_specs=[pl.BlockSpec((D_MODEL, tm), lambda i: (0, i)), sc, sc,
                   pl.BlockSpec((PEER_HEADS, tm // LANES, PEER_N_KEYS, LANES),
                                lambda i: (0, i, 0, 0))],
        out_shape=[jax.ShapeDtypeStruct((D_MODEL, T), BF16)]
        + [jax.ShapeDtypeStruct((PEER_HEADS, PEER_N_KEYS, T), F32)] * 2
        + [jax.ShapeDtypeStruct((PEER_HEADS, T // LANES, PEER_N_KEYS, LANES), F32)],
        compiler_params=_cparams(("parallel",)),
        name="peer_prep",
    )(x2, lw["norm_ffn_g"], lw["peer_w_q"], lw["peer_keys"])


def _peer_main_kernel(x_ref, hn_ref, a1_ref, th_ref, e2_ref, u_ref, vt_ref, fg_ref, o_ref,
                      acc_ref, act_ref, w_ref, *, n_i, final_norm):
    e = pl.program_id(1)
    tm = hn_ref.shape[1]

    @pl.when(e == 0)
    def _():
        acc_ref[...] = jnp.zeros_like(acc_ref)

    n_blk = n_i * PEER_N_KEYS // PEER_BLK
    per_blk = PEER_BLK // PEER_N_KEYS

    n_strip = tm // LANES

    def store_activations(p, slot):
        a = _dot(u_ref[p * PEER_BLK:(p + 1) * PEER_BLK, :], hn_ref[...])
        act = a * (1.0 + lax.erf(a * (1.0 / math.sqrt(2.0))))
        for c in range(n_strip):
            act_ref[slot, c] = act[:, c * LANES:(c + 1) * LANES]

    def fold(p, slot):
        w = jnp.concatenate([w_ref[slot, c] for c in range(n_strip)], axis=1)
        acc_ref[...] += _dot(vt_ref[p], w)

    store_activations(0, 0)
    for p in range(n_blk):
        slot = p % 2
        if p + 1 < n_blk:
            store_activations(p + 1, 1 - slot)
        if p > 0:
            fold(p - 1, 1 - slot)
        for ii in range(per_blk):
            i = p * per_blk + ii
            rows = slice(ii * PEER_N_KEYS, (ii + 1) * PEER_N_KEYS)
            for c in range(n_strip):
                ls = slice(c * LANES, (c + 1) * LANES)
                g = None
                for hd in range(PEER_HEADS):
                    e2 = e2_ref[hd, c]
                    t = jnp.where(e2 > th_ref[hd, i:i + 1, ls], e2, 0.0) * a1_ref[hd, i:i + 1, ls]
                    g = t if g is None else g + t
                w_ref[slot, c, rows, :] = (g * act_ref[slot, c, rows, :]).astype(BF16)
    fold(n_blk - 1, (n_blk - 1) % 2)

    @pl.when(e == pl.num_programs(1) - 1)
    def _():
        y = x_ref[...] + acc_ref[...].T
        if final_norm:
            y = _rms(y, fg_ref[...])
        o_ref[...] = y


def _peer_main(x2, hn, a1, th, e2, lw, final_g, tm, n_i, final_norm):
    T = x2.shape[0]
    et = n_i * PEER_N_KEYS
    return pl.pallas_call(
        functools.partial(_peer_main_kernel, n_i=n_i, final_norm=final_norm),
        grid=(T // tm, PEER_N_EXPERTS // et),
        in_specs=[pl.BlockSpec((tm, D_MODEL), lambda t, e: (t, 0)),
                  pl.BlockSpec((D_MODEL, tm), lambda t, e: (0, t)),
                  pl.BlockSpec((PEER_HEADS, n_i, tm), lambda t, e: (0, e, t)),
                  pl.BlockSpec((PEER_HEADS, n_i, tm), lambda t, e: (0, e, t)),
                  pl.BlockSpec((PEER_HEADS, tm // LANES, PEER_N_KEYS, LANES),
                               lambda t, e: (0, t, 0, 0)),
                  pl.BlockSpec((et, D_MODEL), lambda t, e: (e, 0)),
                  pl.BlockSpec((et // PEER_BLK, D_MODEL, PEER_BLK), lambda t, e: (e, 0, 0)),
                  pl.BlockSpec((1, D_MODEL), lambda t, e: (0, 0))],
        out_specs=pl.BlockSpec((tm, D_MODEL), lambda t, e: (t, 0)),
        out_shape=jax.ShapeDtypeStruct((T, D_MODEL), F32),
        scratch_shapes=[pltpu.VMEM((D_MODEL, tm), F32),
                        pltpu.VMEM((2, tm // LANES, PEER_BLK, LANES), F32),
                        pltpu.VMEM((2, tm // LANES, PEER_BLK, LANES), BF16)],
        compiler_params=_cparams(("parallel", "arbitrary")),
        name="peer_main",
    )(x2, hn, a1, th, e2, lw["peer_u"], lw["peer_vt"], final_g)


def _pad_cols(w, left, total):
    return jnp.pad(w, ((0, 0), (left, total - left - w.shape[-1])))


def _head_groups(w, n_heads, width, left=0):
    return jnp.concatenate(
        [_pad_cols(w[:, h * width:(h + 1) * width], left, LANES) for h in range(n_heads)], axis=-1)


def _layer_weights(p, l):
    w_in = p["w_in"][l]
    widths = (MLA_Q_LORA, MLA_KV_LORA, MLA_ROPE, F_WIDTH, GQA_HEADS * GQA_HEAD_DIM,
              GQA_KV_HEADS * GQA_HEAD_DIM, GQA_KV_HEADS * GQA_HEAD_DIM, 3 * D_MODEL)
    offs = np.cumsum((0,) + widths)
    wcq, wckv, wkr, wxf, wgq, wgk, wgv, wgate = (w_in[:, offs[i]:offs[i + 1]] for i in range(8))

    w1 = jnp.concatenate([
        wcq, wckv,
        _pad_cols(wkr, MLA_NOPE, LANES),
        wxf,
        _head_groups(wgq, GQA_HEADS, GQA_HEAD_DIM),
        _head_groups(wgk, GQA_KV_HEADS, GQA_HEAD_DIM),
        _head_groups(wgv, GQA_KV_HEADS, GQA_HEAD_DIM),
        wgate], axis=-1).astype(BF16)
    assert w1.shape[-1] == _W1_COLS

    w_uq = p["mla_w_uq"][l]
    qd = MLA_NOPE + MLA_ROPE
    uq = [_pad_cols(w_uq[:, h * qd:(h + 1) * qd], 0, LANES) for h in range(MLA_HEADS)]
    w_ukv = p["mla_w_ukv"][l]
    kd = MLA_NOPE + MLA_V
    uk = [_pad_cols(w_ukv[:, h * kd:h * kd + MLA_NOPE], 0, LANES) for h in range(MLA_HEADS)]
    uv = [_pad_cols(w_ukv[:, h * kd + MLA_NOPE:(h + 1) * kd], 0, LANES) for h in range(MLA_HEADS)]

    def pad_rows(w, n_heads, width):
        return jnp.concatenate(
            [jnp.pad(w[h * width:(h + 1) * width], ((0, LANES - width), (0, 0)))
             for h in range(n_heads)], axis=0)

    return {
        "norm_mix_g": p["norm_mix_g"][l][None, :],
        "w1": w1,
        "q_norm_g": p["mla_q_norm_g"][l][None, :],
        "w_uq": jnp.concatenate(uq, axis=-1).astype(BF16),
        "kv_norm_g": p["mla_kv_norm_g"][l][None, :],
        "w_ukv": jnp.concatenate(uk + uv, axis=-1).astype(BF16),
        "mla_w_o": pad_rows(p["mla_w_o"][l], MLA_HEADS, MLA_V).astype(BF16),
        "fnet_w": p["fnet_w"][l].astype(BF16),
        "fnet_b": p["fnet_b"][l][None, :],
        "gqa_sink": p["gqa_sink"][l],
        "gqa_w_o": pad_rows(p["gqa_w_o"][l], GQA_HEADS, GQA_HEAD_DIM).astype(BF16),
        "w_out": p["w_out"][l].astype(BF16),
        "norm_ffn_g": p["norm_ffn_g"][l][None, :],
        "peer_w_q": p["peer_w_q"][l].astype(BF16),
        "peer_keys": p["peer_keys"][l].reshape(2 * PEER_HEADS, PEER_N_KEYS, PEER_HALF).astype(BF16),
        "peer_u": p["peer_u"][l].astype(BF16),
        "peer_vt": p["peer_v"][l].astype(BF16).reshape(
            PEER_N_EXPERTS // PEER_BLK, PEER_BLK, D_MODEL).transpose(0, 2, 1),
    }


def _rope_tables(S):
    pos = jnp.arange(S, dtype=F32)[:, None]

    def tables(d, left, passthrough):
        inv = 1.0 / (ROPE_THETA ** (jnp.arange(0, d, 2, dtype=F32) / d))
        ang = pos * inv[None, :]
        cos, sin, zero = jnp.cos(ang), jnp.sin(ang), jnp.zeros_like(ang)
        pad = lambda parts: _pad_cols(jnp.concatenate(parts, axis=-1), left, LANES)
        lead = jnp.pad(jnp.ones((S, passthrough), F32), ((0, 0), (0, LANES - passthrough)))
        return pad([cos, cos]) + lead, pad([-sin, zero]), pad([zero, sin])

    cm, sm_lo, sm_hi = tables(MLA_ROPE, MLA_NOPE, MLA_NOPE)
    cg, sg_lo, sg_hi = tables(GQA_HEAD_DIM, 0, 0)
    return {"cm": cm, "sm_lo": sm_lo, "sm_hi": sm_hi, "cg": cg, "sg_lo": sg_lo, "sg_hi": sg_hi}


def _fourier_tables(S):
    n2 = 128
    n1 = S // n2

    def cos_sin(num, den):
        ang = (2.0 * math.pi / den) * (num % den).astype(F32)
        return jnp.cos(ang), jnp.sin(ang)

    i1 = jnp.arange(n1, dtype=jnp.int32)
    c1, s1 = cos_sin(i1[:, None] * i1[None, :], n1)
    w1 = jnp.concatenate([c1, -s1], axis=0)
    i2 = jnp.arange(n2, dtype=jnp.int32)
    kk = i1[:, None, None] + n1 * i2[None, :, None]
    mr, ms = cos_sin(kk * i2[None, None, :], S)
    m2 = jnp.concatenate([jnp.concatenate([mr, ms], axis=-1),
                          jnp.concatenate([-ms, mr], axis=-1)], axis=1)
    ic = jnp.arange(F_GROUP_DIM, dtype=jnp.int32)
    cc, cs = cos_sin(ic[:, None] * ic[None, :], F_GROUP_DIM)
    eye = jnp.eye(F_GROUPS, dtype=F32)
    return {"n1": n1, "n2": n2, "w1": _split(w1), "m2": _split(m2),
            "cc": _split(jnp.kron(eye, cc)), "cs": _split(jnp.kron(eye, cs))}


def _pick(S, pref):
    t = pref
    while S % t:
        t //= 2
    return t


def _trunk(x, layers, final_g, rope_t, four_t):
    B, S, _ = x.shape
    T = B * S
    x2 = x.reshape(T, D_MODEL)
    tm = _pick(S, 512)
    n_layers = len(layers)
    for l, lw in enumerate(layers):
        q, k, v, xf, qg, kg, vg, gate = _in_proj(x2, lw, rope_t, S, tm)
        attn = _mla_attn(q, k, v, B, S, _pick(S, 1024), _pick(S // 2, 512))
        four = _fourier(xf, four_t, B, S)
        og = _gqa(qg, kg, vg, lw["gqa_sink"], B, S, _pick(S, 512))
        x2 = _merge(x2, attn, four, og, gate, lw, tm)
        hn, a1, th, e2 = _peer_prep(x2, lw, _pick(T, 256))
        x2 = _peer_main(x2, hn, a1, th, e2, lw, final_g, _pick(T, 512), 32,
                        final_norm=(l == n_layers - 1))
    return x2.reshape(B, S, D_MODEL)


def kernel(x_prompt, x_sample, norm_mix_g, w_in, mla_q_norm_g, mla_w_uq, mla_kv_norm_g, mla_w_ukv, mla_w_o, fnet_w, fnet_b, gqa_sink, gqa_w_o, w_out, norm_ffn_g, peer_w_q, peer_keys, peer_u, peer_v, final_norm_g):
    p = dict(norm_mix_g=norm_mix_g, w_in=w_in, mla_q_norm_g=mla_q_norm_g, mla_w_uq=mla_w_uq,
             mla_kv_norm_g=mla_kv_norm_g, mla_w_ukv=mla_w_ukv, mla_w_o=mla_w_o, fnet_w=fnet_w,
             fnet_b=fnet_b, gqa_sink=gqa_sink, gqa_w_o=gqa_w_o, w_out=w_out,
             norm_ffn_g=norm_ffn_g, peer_w_q=peer_w_q, peer_keys=peer_keys, peer_u=peer_u,
             peer_v=peer_v)
    layers = [_layer_weights(p, l) for l in range(w_in.shape[0])]
    final_g = final_norm_g[None, :]
    outs = []
    for x in (x_prompt, x_sample):
        S = x.shape[1]
        outs.append(_trunk(x, layers, final_g, _rope_tables(S), _fourier_tables(S)))
    return tuple(outs)
```

```python
import functools
import math

import jax
import jax.numpy as jnp
import numpy as np
from jax import lax
from jax.experimental import pallas as pl
from jax.experimental.pallas import tpu as pltpu

D_MODEL = 1024
EPS = 1e-6
ROPE_THETA = 10000.0
NEG = -1e30
LANES = 128

MLA_HEADS = 8
MLA_NOPE = 64
MLA_ROPE = 32
MLA_V = 64
MLA_Q_LORA = 384
MLA_KV_LORA = 256
F_GROUPS = 4
F_GROUP_DIM = 64
F_WIDTH = F_GROUPS * F_GROUP_DIM
GQA_HEADS = 4
GQA_KV_HEADS = 2
GQA_GROUP = GQA_HEADS // GQA_KV_HEADS
GQA_HEAD_DIM = 64
WINDOW = 128
PEER_HEADS = 8
PEER_N_KEYS = 128
PEER_N_EXPERTS = PEER_N_KEYS * PEER_N_KEYS
PEER_HALF = 128
PEER_TOPK = 16

VMEM_LIMIT = 56 * 1024 * 1024

BF16 = jnp.bfloat16
F32 = jnp.float32


def _cparams(sem):
    return pltpu.CompilerParams(dimension_semantics=sem, vmem_limit_bytes=VMEM_LIMIT)


def _rms(x, g):
    return x * lax.rsqrt(jnp.mean(x * x, axis=-1, keepdims=True) + EPS) * g


def _dot(a, b):
    return jnp.dot(a, b, preferred_element_type=F32)


def _dot_nt(a, b):
    return lax.dot_general(a, b, (((1,), (1,)), ((), ())), preferred_element_type=F32)


_C_CQ = (0, 384)
_C_CKV = (384, 640)
_C_KR = (640, 768)
_C_XF = (768, 1024)
_C_GQ = (1024, 1536)
_C_GK = (1536, 1792)
_C_GV = (1792, 2048)
_C_GATE = (2048, 5120)
_W1_COLS = 5120


def _rotary(x, half, cos, sin_lo, sin_hi):
    n = x.shape[1]
    up = pltpu.roll(x, n - half, axis=1)
    dn = pltpu.roll(x, half, axis=1)
    out = []
    for g in range(n // LANES):
        sl = slice(g * LANES, (g + 1) * LANES)
        out.append(x[:, sl] * cos + up[:, sl] * sin_lo + dn[:, sl] * sin_hi)
    return out


def _in_proj_kernel(x_ref, g_ref, w1_ref, qng_ref, wuq_ref, kvng_ref, wukv_ref,
                    cm_ref, sml_ref, smh_ref, cg_ref, sgl_ref, sgh_ref,
                    q_ref, k_ref, v_ref, xf_ref, qg_ref, kg_ref, vg_ref, gate_ref):
    h = _rms(x_ref[...], g_ref[...]).astype(BF16)

    def proj(c):
        return _dot(h, w1_ref[:, c[0]:c[1]])

    mla_tabs = (cm_ref[...], sml_ref[...], smh_ref[...])
    gqa_tabs = (cg_ref[...], sgl_ref[...], sgh_ref[...])

    cqn = _rms(proj(_C_CQ), qng_ref[...]).astype(BF16)
    q = _rotary(_dot(cqn, wuq_ref[...]), MLA_ROPE // 2, *mla_tabs)
    mla_scale = (MLA_NOPE + MLA_ROPE) ** -0.5 * math.log2(math.e)
    for hd in range(MLA_HEADS):
        q_ref[:, hd * LANES:(hd + 1) * LANES] = (q[hd] * mla_scale).astype(BF16)

    ckvn = _rms(proj(_C_CKV), kvng_ref[...]).astype(BF16)
    kv = _dot(ckvn, wukv_ref[...])
    k_rope, = _rotary(proj(_C_KR), MLA_ROPE // 2, *mla_tabs)
    for hd in range(MLA_HEADS):
        sl = slice(hd * LANES, (hd + 1) * LANES)
        k_ref[:, sl] = (kv[:, sl] + k_rope).astype(BF16)
    lane = lax.broadcasted_iota(jnp.int32, (1, MLA_HEADS * LANES), 1)
    ones_cols = (lane % LANES == MLA_V).astype(F32)
    v_ref[...] = (kv[:, MLA_HEADS * LANES:] + ones_cols).T.astype(BF16)

    xf_ref[...] = proj(_C_XF)

    gq = _rotary(proj(_C_GQ), GQA_HEAD_DIM // 2, *gqa_tabs)
    gqa_scale = GQA_HEAD_DIM ** -0.5
    for hd in range(GQA_HEADS):
        qg_ref[:, hd * LANES:(hd + 1) * LANES] = (gq[hd] * gqa_scale).astype(BF16)
    gk = _rotary(proj(_C_GK), GQA_HEAD_DIM // 2, *gqa_tabs)
    for hd in range(GQA_KV_HEADS):
        kg_ref[:, hd * LANES:(hd + 1) * LANES] = gk[hd].astype(BF16)
    vg_ref[...] = proj(_C_GV).astype(BF16)

    gate = proj(_C_GATE)
    gate_ref[...] = (1.0 / (1.0 + jnp.exp(-gate))).astype(gate_ref.dtype)


def _in_proj(x2, lw, tabs, S, tm):
    T = x2.shape[0]
    nt = T // tm
    ns = S // tm
    full = lambda shape: pl.BlockSpec(shape, lambda i: (0,) * len(shape))
    row = lambda w: pl.BlockSpec((tm, w), lambda i: (i, 0))
    tab = pl.BlockSpec((tm, LANES), lambda i: (i % ns, 0))
    outs = [(1024, BF16), (1024, BF16), (1024, BF16), (F_WIDTH, F32), (512, BF16), (256, BF16),
            (256, BF16), (3 * D_MODEL, BF16)]
    return pl.pallas_call(
        _in_proj_kernel,
        grid=(nt,),
        in_specs=[row(D_MODEL), full((1, D_MODEL)), full((D_MODEL, _W1_COLS)),
                  full((1, MLA_Q_LORA)), full((MLA_Q_LORA, 1024)),
                  full((1, MLA_KV_LORA)), full((MLA_KV_LORA, 2048)),
                  tab, tab, tab, tab, tab, tab],
        out_specs=[pl.BlockSpec((w, tm), lambda i: (0, i)) if n == 2 else row(w)
                   for n, (w, _) in enumerate(outs)],
        out_shape=[jax.ShapeDtypeStruct((w, T) if n == 2 else (T, w), dt)
                   for n, (w, dt) in enumerate(outs)],
        compiler_params=_cparams(("parallel",)),
        name="in_proj",
    )(x2, lw["norm_mix_g"], lw["w1"], lw["q_norm_g"], lw["w_uq"], lw["kv_norm_g"], lw["w_ukv"],
      tabs["cm"], tabs["sm_lo"], tabs["sm_hi"], tabs["cg"], tabs["sg_lo"], tabs["sg_hi"])


def _mla_attn_kernel(q_ref, k_ref, vt_ref, o_ref, s_ref, p_ref, *, tk):
    tq = q_ref.shape[0]
    nk = k_ref.shape[0] // tk
    q = q_ref[...]

    def k_chunk(c):
        return k_ref[c * tk:(c + 1) * tk, :]

    def vt_chunk(c):
        return vt_ref[:, c * tk:(c + 1) * tk]

    def step(slot, vt_prev, k_next, carry):
        m, alpha, acc = carry
        pv = _dot(vt_prev, p_ref[1 - slot])
        s_ref[1 - slot] = _dot_nt(k_next, q)
        s = s_ref[slot]
        m_new = jnp.maximum(m, jnp.max(s, axis=0, keepdims=True))
        p_ref[slot] = jnp.exp2(s - m_new).astype(BF16)
        return m_new, jnp.exp2(m - m_new), acc * alpha + pv

    s_ref[0] = _dot_nt(k_chunk(0), q)
    p_ref[1] = jnp.zeros(p_ref.shape[1:], BF16)
    carry = (jnp.full((1, tq), NEG, F32), jnp.ones((1, tq), F32), jnp.zeros((LANES, tq), F32))
    for c in range(nk):
        carry = step(c % 2, vt_chunk(max(c - 1, 0)), k_chunk(min(c + 1, nk - 1)), carry)
    _, alpha, acc = carry
    acc = acc * alpha + _dot(vt_chunk(nk - 1), p_ref[1])
    o_ref[...] = (acc / acc[MLA_V:MLA_V + 1, :]).T.astype(o_ref.dtype)


def _mla_attn(q, k, vt, B, S, tq, tk):
    assert (S // tk) % 2 == 0
    q3, k3 = (a.reshape(B, S, MLA_HEADS * LANES) for a in (q, k))
    out = pl.pallas_call(
        functools.partial(_mla_attn_kernel, tk=tk),
        scratch_shapes=[pltpu.VMEM((2, tk, tq), F32), pltpu.VMEM((2, tk, tq), BF16)],
        grid=(B, MLA_HEADS, S // tq),
        in_specs=[pl.BlockSpec((None, tq, LANES), lambda b, h, i: (b, i, h)),
                  pl.BlockSpec((None, S, LANES), lambda b, h, i: (b, 0, h)),
                  pl.BlockSpec((LANES, S), lambda b, h, i: (h, b))],
        out_specs=pl.BlockSpec((None, tq, LANES), lambda b, h, i: (b, i, h)),
        out_shape=jax.ShapeDtypeStruct((B, S, MLA_HEADS * LANES), BF16),
        compiler_params=_cparams(("parallel", "parallel", "parallel")),
        name="mla_attn",
    )(q3, k3, vt)
    return out.reshape(B * S, MLA_HEADS * LANES)


def _split(x):
    hi = x.astype(BF16)
    return hi, (x - hi.astype(F32)).astype(BF16)


def _dot3(a, b):
    (ah, al), (bh, bl) = a, b
    return _dot(ah, bh) + (_dot(ah, bl) + _dot(al, bh))


def _fft1_kernel(wh_ref, wl_ref, x_ref, o_ref):
    o_ref[...] = _dot3((wh_ref[...], wl_ref[...]), _split(x_ref[...]))


def _fft2_kernel(mh_ref, ml_ref, a_ref, cch_ref, ccl_ref, csh_ref, csl_ref, o_ref, *, scale):
    cc = (cch_ref[...], ccl_ref[...])
    cs = (csh_ref[...], csl_ref[...])
    for j in range(o_ref.shape[0]):
        aa = jnp.concatenate([a_ref[0, j], a_ref[1, j]], axis=0)
        y = _dot3((mh_ref[j], ml_ref[j]), _split(aa))
        n2 = y.shape[0] // 2
        o_ref[j] = (_dot3(_split(y[:n2]), cc) + _dot3(_split(y[n2:]), cs)) * scale


def _fourier(xf, ft, B, S):
    n1, n2 = ft["n1"], ft["n2"]
    cols = n2 * F_WIDTH
    tc = min(cols, 4096)
    x3 = xf.reshape(B, n1, cols)
    a = pl.pallas_call(
        _fft1_kernel,
        grid=(B, cols // tc),
        in_specs=[pl.BlockSpec((2 * n1, n1), lambda b, j: (0, 0)),
                  pl.BlockSpec((2 * n1, n1), lambda b, j: (0, 0)),
                  pl.BlockSpec((None, n1, tc), lambda b, j: (b, 0, j))],
        out_specs=pl.BlockSpec((None, 2 * n1, tc), lambda b, j: (b, 0, j)),
        out_shape=jax.ShapeDtypeStruct((B, 2 * n1, cols), F32),
        compiler_params=_cparams(("parallel", "parallel")),
        name="fourier_stage1",
    )(*ft["w1"], x3)
    a5 = a.reshape(B, 2, n1, n2, F_WIDTH)
    kb = math.gcd(4, n1)
    table = pl.BlockSpec((F_WIDTH, F_WIDTH), lambda b, k: (0, 0))
    twiddle = pl.BlockSpec((kb, 2 * n2, 2 * n2), lambda b, k: (k, 0, 0))
    y = pl.pallas_call(
        functools.partial(_fft2_kernel, scale=1.0 / math.sqrt(S * F_GROUP_DIM)),
        grid=(B, n1 // kb),
        in_specs=[twiddle, twiddle,
                  pl.BlockSpec((None, 2, kb, n2, F_WIDTH), lambda b, k: (b, 0, k, 0, 0)),
                  table, table, table, table],
        out_specs=pl.BlockSpec((None, kb, n2, F_WIDTH), lambda b, k: (b, k, 0, 0)),
        out_shape=jax.ShapeDtypeStruct((B, n1, n2, F_WIDTH), F32),
        compiler_params=_cparams(("parallel", "parallel")),
        name="fourier_stage2",
    )(*ft["m2"], a5, *ft["cc"], *ft["cs"])
    return jnp.transpose(y, (0, 2, 1, 3)).reshape(B * S, F_WIDTH)


def _gqa_kernel(sink_ref, q_ref, kp_ref, kc_ref, kn_ref, vp_ref, vc_ref, vn_ref, o_ref, *, S):
    n = pl.program_id(1)
    tq, halo = q_ref.shape[0], kp_ref.shape[0]
    width = tq + 2 * halo
    qpos = n * tq + lax.broadcasted_iota(jnp.int32, (tq, width), 0)
    kpos = n * tq - halo + lax.broadcasted_iota(jnp.int32, (tq, width), 1)
    mask = (jnp.abs(qpos - kpos) <= WINDOW) & (kpos >= 0) & (kpos < S)
    for hd in range(GQA_HEADS):
        kh = hd // GQA_GROUP
        qs = slice(hd * LANES, (hd + 1) * LANES)
        ks = slice(kh * LANES, (kh + 1) * LANES)
        q = q_ref[:, qs]
        s = jnp.concatenate([_dot_nt(q, kp_ref[:, ks]), _dot_nt(q, kc_ref[:, ks]),
                             _dot_nt(q, kn_ref[:, ks])], axis=-1)
        s = jnp.where(mask, s, NEG)
        sk = sink_ref[hd]
        m = jnp.maximum(jnp.max(s, axis=-1, keepdims=True), sk)
        p = jnp.exp(s - m)
        den = jnp.sum(p, axis=-1, keepdims=True) + jnp.exp(sk - m)
        pb = (p / den).astype(BF16)
        o = (_dot(pb[:, :halo], vp_ref[:, ks]) + _dot(pb[:, halo:halo + tq], vc_ref[:, ks])
             + _dot(pb[:, halo + tq:], vn_ref[:, ks]))
        o_ref[:, qs] = o.astype(o_ref.dtype)


def _gqa(qg, kg, vg, sink, B, S, tq):
    halo = WINDOW
    per = tq // halo
    nb = S // halo
    q3 = qg.reshape(B, S, GQA_HEADS * LANES)
    k3 = kg.reshape(B, S, GQA_KV_HEADS * LANES)
    v3 = vg.reshape(B, S, GQA_KV_HEADS * LANES)
    kvw = GQA_KV_HEADS * LANES
    prev = pl.BlockSpec((None, halo, kvw), lambda b, n: (b, jnp.maximum(n * per - 1, 0), 0))
    cur = pl.BlockSpec((None, tq, kvw), lambda b, n: (b, n, 0))
    nxt = pl.BlockSpec((None, halo, kvw), lambda b, n: (b, jnp.minimum((n + 1) * per, nb - 1), 0))
    out = pl.pallas_call(
        functools.partial(_gqa_kernel, S=S),
        grid=(B, S // tq),
        in_specs=[pl.BlockSpec(memory_space=pltpu.SMEM),
                  pl.BlockSpec((None, tq, GQA_HEADS * LANES), lambda b, n: (b, n, 0)),
                  prev, cur, nxt, prev, cur, nxt],
        out_specs=pl.BlockSpec((None, tq, GQA_HEADS * LANES), lambda b, n: (b, n, 0)),
        out_shape=jax.ShapeDtypeStruct((B, S, GQA_HEADS * LANES), BF16),
        compiler_params=_cparams(("parallel", "parallel")),
        name="gqa_window",
    )(sink, q3, k3, k3, k3, v3, v3, v3)
    return out.reshape(B * S, GQA_HEADS * LANES)


def _merge_kernel(x_ref, attn_ref, four_ref, og_ref, gate_ref, wo_ref, fw_ref, fb_ref, gwo_ref,
                  wout_ref, o_ref):
    mla_o = _dot(attn_ref[...], wo_ref[...])
    fnet_o = _dot(four_ref[...].astype(BF16), fw_ref[...]) + fb_ref[...]
    gqa_o = _dot(og_ref[...], gwo_ref[...])
    d = D_MODEL
    merged = (gate_ref[:, 0:d] * mla_o + gate_ref[:, d:2 * d] * fnet_o
              + gate_ref[:, 2 * d:3 * d] * gqa_o)
    o_ref[...] = x_ref[...] + _dot(merged.astype(BF16), wout_ref[...])


def _merge(x2, attn, four, og, gate, lw, tm):
    T = x2.shape[0]
    full = lambda shape: pl.BlockSpec(shape, lambda i: (0,) * len(shape))
    row = lambda w: pl.BlockSpec((tm, w), lambda i: (i, 0))
    return pl.pallas_call(
        _merge_kernel,
        grid=(T // tm,),
        in_specs=[row(D_MODEL), row(1024), row(F_WIDTH), row(512), row(3 * D_MODEL),
                  full((1024, D_MODEL)), full((F_WIDTH, D_MODEL)), full((1, D_MODEL)),
                  full((512, D_MODEL)), full((D_MODEL, D_MODEL))],
        out_specs=row(D_MODEL),
        out_shape=jax.ShapeDtypeStruct((T, D_MODEL), F32),
        compiler_params=_cparams(("parallel",)),
        name="merge",
    )(x2, attn, four, og, gate, lw["mla_w_o"], lw["fnet_w"], lw["fnet_b"], lw["gqa_w_o"],
      lw["w_out"])


LOG2E = math.log2(math.e)
PEER_BLK = 256
PEER_RANKS = PEER_TOPK + 1


def _sort_network(n):
    pairs = []

    def merge(lo, m, r):
        step = 2 * r
        if step < m:
            merge(lo, m, step)
            merge(lo + r, m, step)
            pairs.extend((i, i + r) for i in range(lo + r, lo + m - r, step))
        else:
            pairs.append((lo, lo + r))

    def sort(lo, m):
        if m > 1:
            sort(lo, m // 2)
            sort(lo + m // 2, m // 2)
            merge(lo, m, 1)

    sort(0, n)
    return pairs


def _top_of_sorted_lists(lists, k, extra=None):
    lists = list(lists)
    neg = jnp.full(lists[0].shape, -jnp.inf, F32)
    vals = []
    for r in range(k):
        mx = jnp.max(lists[0], axis=0, keepdims=True)
        if extra is not None:
            mx = jnp.maximum(mx, jnp.max(extra, axis=0, keepdims=True))
        vals.append(mx)
        if r == k - 1:
            break
        win = lists[0] == mx
        for v in range(min(len(lists), k - 1 - r)):
            lists[v] = jnp.where(win, lists[v + 1] if v + 1 < len(lists) else neg, lists[v])
        if extra is not None:
            extra = jnp.where(extra == mx, -jnp.inf, extra)
    return vals


def _top_sorted(s, k):
    lists = [s[8 * v:8 * v + 8] for v in range(s.shape[0] // 8)]
    for i, j in _sort_network(len(lists)):
        lists[i], lists[j] = jnp.maximum(lists[i], lists[j]), jnp.minimum(lists[i], lists[j])
    return _top_of_sorted_lists(lists, k)


def _peer_prep_kernel(x_ref, g_ref, wq_ref, keys_ref, hn_ref, a1_ref, th_ref, e2_ref):
    hn = _rms(x_ref[...], g_ref[...])
    hn_ref[...] = hn.T.astype(BF16)
    q = _dot(hn.astype(BF16), wq_ref[...]).astype(BF16)
    tm = q.shape[0]
    for hd in range(PEER_HEADS):
        sts, tops = [], []
        for p in range(2):
            hp = 2 * hd + p
            st = _dot_nt(keys_ref[hp], q[:, hp * PEER_HALF:(hp + 1) * PEER_HALF])
            sts.append(st)
            tops.append(_top_sorted(st, PEER_RANKS))
        a, b = tops
        a_lo = jnp.concatenate(a[:8], axis=0)
        a_hi = jnp.concatenate(a[8:] + [jnp.full((7, tm), -jnp.inf, F32)], axis=0)
        top = _top_of_sorted_lists([a_lo + b[r] for r in range(PEER_RANKS)], PEER_RANKS,
                                   extra=a_hi + b[0])
        m = top[0]
        log_z = jnp.log(sum(jnp.exp(t - m) for t in top[:PEER_TOPK]))
        thr = 0.5 * (top[PEER_TOPK - 1] + top[PEER_TOPK])
        st1 = (sts[0] - a[0]) * LOG2E
        s2k = (sts[1] - b[0] - log_z) * LOG2E
        a1_ref[hd] = jnp.exp2(st1)
        th_ref[hd] = jnp.exp2((thr - m - log_z) * LOG2E - st1 - 1.0)
        e2 = jnp.exp2(s2k - 1.0)
        for c in range(tm // LANES):
            e2_ref[hd, c] = e2[:, c * LANES:(c + 1) * LANES]


def _peer_prep(x2, lw, tm):
    T = x2.shape[0]
    sc = pl.BlockSpec((PEER_HEADS, PEER_N_KEYS, tm), lambda i: (0, 0, i))
    return pl.pallas_call(
        _peer_prep_kernel,
        grid=(T // tm,),
        in_specs=[pl.BlockSpec((tm, D_MODEL), lambda i: (i, 0)),
                  pl.BlockSpec((1, D_MODEL), lambda i: (0, 0)),
                  pl.BlockSpec((D_MODEL, 2 * PEER_HEADS * PEER_HALF), lambda i: (0, 0)),
                  pl.BlockSpec((2 * PEER_HEADS, PEER_N_KEYS, PEER_HALF), lambda i: (0, 0, 0))],
        out_specs=[pl.BlockSpec((D_MODEL, tm), lambda i: (0, i)), sc, sc,
                   pl.BlockSpec((PEER_HEADS, tm // LANES, PEER_N_KEYS, LANES),
                                lambda i: (0, i, 0, 0))],
        out_shape=[jax.ShapeDtypeStruct((D_MODEL, T), BF16)]
        + [jax.ShapeDtypeStruct((PEER_HEADS, PEER_N_KEYS, T), F32)] * 2
        + [jax.ShapeDtypeStruct((PEER_HEADS, T // LANES, PEER_N_KEYS, LANES), F32)],
        compiler_params=_cparams(("parallel",)),
        name="peer_prep",
    )(x2, lw["norm_ffn_g"], lw["peer_w_q"], lw["peer_keys"])


def _peer_main_kernel(x_ref, hn_ref, a1_ref, th_ref, e2_ref, u_ref, vt_ref, fg_ref, o_ref,
                      acc_ref, act_ref, w_ref, *, n_i, final_norm):
    e = pl.program_id(1)
    tm = hn_ref.shape[1]

    @pl.when(e == 0)
    def _():
        acc_ref[...] = jnp.zeros_like(acc_ref)

    n_blk = n_i * PEER_N_KEYS // PEER_BLK
    per_blk = PEER_BLK // PEER_N_KEYS

    n_strip = tm // LANES

    def store_activations(p, slot):
        a = _dot(u_ref[p * PEER_BLK:(p + 1) * PEER_BLK, :], hn_ref[...])
        act = a * (1.0 + lax.erf(a * (1.0 / math.sqrt(2.0))))
        for c in range(n_strip):
            act_ref[slot, c] = act[:, c * LANES:(c + 1) * LANES]

    def fold(p, slot):
        w = jnp.concatenate([w_ref[slot, c] for c in range(n_strip)], axis=1)
        acc_ref[...] += _dot(vt_ref[p], w)

    store_activations(0, 0)
    for p in range(n_blk):
        slot = p % 2
        if p + 1 < n_blk:
            store_activations(p + 1, 1 - slot)
        if p > 0:
            fold(p - 1, 1 - slot)
        for ii in range(per_blk):
            i = p * per_blk + ii
            rows = slice(ii * PEER_N_KEYS, (ii + 1) * PEER_N_KEYS)
            for c in range(n_strip):
                ls = slice(c * LANES, (c + 1) * LANES)
                g = None
                for hd in range(PEER_HEADS):
                    e2 = e2_ref[hd, c]
                    t = jnp.where(e2 > th_ref[hd, i:i + 1, ls], e2, 0.0) * a1_ref[hd, i:i + 1, ls]
                    g = t if g is None else g + t
                w_ref[slot, c, rows, :] = (g * act_ref[slot, c, rows, :]).astype(BF16)
    fold(n_blk - 1, (n_blk - 1) % 2)

    @pl.when(e == pl.num_programs(1) - 1)
    def _():
        y = x_ref[...] + acc_ref[...].T
        if final_norm:
            y = _rms(y, fg_ref[...])
        o_ref[...] = y


def _peer_main(x2, hn, a1, th, e2, lw, final_g, tm, n_i, final_norm):
    T = x2.shape[0]
    et = n_i * PEER_N_KEYS
    return pl.pallas_call(
        functools.partial(_peer_main_kernel, n_i=n_i, final_norm=final_norm),
        grid=(T // tm, PEER_N_EXPERTS // et),
        in_specs=[pl.BlockSpec((tm, D_MODEL), lambda t, e: (t, 0)),
                  pl.BlockSpec((D_MODEL, tm), lambda t, e: (0, t)),
                  pl.BlockSpec((PEER_HEADS, n_i, tm), lambda t, e: (0, e, t)),
                  pl.BlockSpec((PEER_HEADS, n_i, tm), lambda t, e: (0, e, t)),
                  pl.BlockSpec((PEER_HEADS, tm // LANES, PEER_N_KEYS, LANES),
                               lambda t, e: (0, t, 0, 0)),
                  pl.BlockSpec((et, D_MODEL), lambda t, e: (e, 0)),
                  pl.BlockSpec((et // PEER_BLK, D_MODEL, PEER_BLK), lambda t, e: (e, 0, 0)),
                  pl.BlockSpec((1, D_MODEL), lambda t, e: (0, 0))],
        out_specs=pl.BlockSpec((tm, D_MODEL), lambda t, e: (t, 0)),
        out_shape=jax.ShapeDtypeStruct((T, D_MODEL), F32),
        scratch_shapes=[pltpu.VMEM((D_MODEL, tm), F32),
                        pltpu.VMEM((2, tm // LANES, PEER_BLK, LANES), F32),
                        pltpu.VMEM((2, tm // LANES, PEER_BLK, LANES), BF16)],
        compiler_params=_cparams(("parallel", "arbitrary")),
        name="peer_main",
    )(x2, hn, a1, th, e2, lw["peer_u"], lw["peer_vt"], final_g)


def _pad_cols(w, left, total):
    return jnp.pad(w, ((0, 0), (left, total - left - w.shape[-1])))


def _head_groups(w, n_heads, width, left=0):
    return jnp.concatenate(
        [_pad_cols(w[:, h * width:(h + 1) * width], left, LANES) for h in range(n_heads)], axis=-1)


def _layer_weights(p, l):
    w_in = p["w_in"][l]
    widths = (MLA_Q_LORA, MLA_KV_LORA, MLA_ROPE, F_WIDTH, GQA_HEADS * GQA_HEAD_DIM,
              GQA_KV_HEADS * GQA_HEAD_DIM, GQA_KV_HEADS * GQA_HEAD_DIM, 3 * D_MODEL)
    offs = np.cumsum((0,) + widths)
    wcq, wckv, wkr, wxf, wgq, wgk, wgv, wgate = (w_in[:, offs[i]:offs[i + 1]] for i in range(8))

    w1 = jnp.concatenate([
        wcq, wckv,
        _pad_cols(wkr, MLA_NOPE, LANES),
        wxf,
        _head_groups(wgq, GQA_HEADS, GQA_HEAD_DIM),
        _head_groups(wgk, GQA_KV_HEADS, GQA_HEAD_DIM),
        _head_groups(wgv, GQA_KV_HEADS, GQA_HEAD_DIM),
        wgate], axis=-1).astype(BF16)
    assert w1.shape[-1] == _W1_COLS

    w_uq = p["mla_w_uq"][l]
    qd = MLA_NOPE + MLA_ROPE
    uq = [_pad_cols(w_uq[:, h * qd:(h + 1) * qd], 0, LANES) for h in range(MLA_HEADS)]
    w_ukv = p["mla_w_ukv"][l]
    kd = MLA_NOPE + MLA_V
    uk = [_pad_cols(w_ukv[:, h * kd:h * kd + MLA_NOPE], 0, LANES) for h in range(MLA_HEADS)]
    uv = [_pad_cols(w_ukv[:, h * kd + MLA_NOPE:(h + 1) * kd], 0, LANES) for h in range(MLA_HEADS)]

    def pad_rows(w, n_heads, width):
        return jnp.concatenate(
            [jnp.pad(w[h * width:(h + 1) * width], ((0, LANES - width), (0, 0)))
             for h in range(n_heads)], axis=0)

    return {
        "norm_mix_g": p["norm_mix_g"][l][None, :],
        "w1": w1,
        "q_norm_g": p["mla_q_norm_g"][l][None, :],
        "w_uq": jnp.concatenate(uq, axis=-1).astype(BF16),
        "kv_norm_g": p["mla_kv_norm_g"][l][None, :],
        "w_ukv": jnp.concatenate(uk + uv, axis=-1).astype(BF16),
        "mla_w_o": pad_rows(p["mla_w_o"][l], MLA_HEADS, MLA_V).astype(BF16),
        "fnet_w": p["fnet_w"][l].astype(BF16),
        "fnet_b": p["fnet_b"][l][None, :],
        "gqa_sink": p["gqa_sink"][l],
        "gqa_w_o": pad_rows(p["gqa_w_o"][l], GQA_HEADS, GQA_HEAD_DIM).astype(BF16),
        "w_out": p["w_out"][l].astype(BF16),
        "norm_ffn_g": p["norm_ffn_g"][l][None, :],
        "peer_w_q": p["peer_w_q"][l].astype(BF16),
        "peer_keys": p["peer_keys"][l].reshape(2 * PEER_HEADS, PEER_N_KEYS, PEER_HALF).astype(BF16),
        "peer_u": p["peer_u"][l].astype(BF16),
        "peer_vt": p["peer_v"][l].astype(BF16).reshape(
            PEER_N_EXPERTS // PEER_BLK, PEER_BLK, D_MODEL).transpose(0, 2, 1),
    }


def _rope_tables(S):
    pos = jnp.arange(S, dtype=F32)[:, None]

    def tables(d, left, passthrough):
        inv = 1.0 / (ROPE_THETA ** (jnp.arange(0, d, 2, dtype=F32) / d))
        ang = pos * inv[None, :]
        cos, sin, zero = jnp.cos(ang), jnp.sin(ang), jnp.zeros_like(ang)
        pad = lambda parts: _pad_cols(jnp.concatenate(parts, axis=-1), left, LANES)
        lead = jnp.pad(jnp.ones((S, passthrough), F32), ((0, 0), (0, LANES - passthrough)))
        return pad([cos, cos]) + lead, pad([-sin, zero]), pad([zero, sin])

    cm, sm_lo, sm_hi = tables(MLA_ROPE, MLA_NOPE, MLA_NOPE)
    cg, sg_lo, sg_hi = tables(GQA_HEAD_DIM, 0, 0)
    return {"cm": cm, "sm_lo": sm_lo, "sm_hi": sm_hi, "cg": cg, "sg_lo": sg_lo, "sg_hi": sg_hi}


def _fourier_tables(S):
    n2 = 128
    n1 = S // n2

    def cos_sin(num, den):
        ang = (2.0 * math.pi / den) * (num % den).astype(F32)
        return jnp.cos(ang), jnp.sin(ang)

    i1 = jnp.arange(n1, dtype=jnp.int32)
    c1, s1 = cos_sin(i1[:, None] * i1[None, :], n1)
    w1 = jnp.concatenate([c1, -s1], axis=0)
    i2 = jnp.arange(n2, dtype=jnp.int32)
    kk = i1[:, None, None] + n1 * i2[None, :, None]
    mr, ms = cos_sin(kk * i2[None, None, :], S)
    m2 = jnp.concatenate([jnp.concatenate([mr, ms], axis=-1),
                          jnp.concatenate([-ms, mr], axis=-1)], axis=1)
    ic = jnp.arange(F_GROUP_DIM, dtype=jnp.int32)
    cc, cs = cos_sin(ic[:, None] * ic[None, :], F_GROUP_DIM)
    eye = jnp.eye(F_GROUPS, dtype=F32)
    return {"n1": n1, "n2": n2, "w1": _split(w1), "m2": _split(m2),
            "cc": _split(jnp.kron(eye, cc)), "cs": _split(jnp.kron(eye, cs))}


def _pick(S, pref):
    t = pref
    while S % t:
        t //= 2
    return t


def _trunk(x, layers, final_g, rope_t, four_t):
    B, S, _ = x.shape
    T = B * S
    x2 = x.reshape(T, D_MODEL)
    tm = _pick(S, 512)
    n_layers = len(layers)
    for l, lw in enumerate(layers):
        q, k, v, xf, qg, kg, vg, gate = _in_proj(x2, lw, rope_t, S, tm)
        attn = _mla_attn(q, k, v, B, S, _pick(S, 512), _pick(S // 2, 512))
        four = _fourier(xf, four_t, B, S)
        og = _gqa(qg, kg, vg, lw["gqa_sink"], B, S, _pick(S, 512))
        x2 = _merge(x2, attn, four, og, gate, lw, tm)
        hn, a1, th, e2 = _peer_prep(x2, lw, _pick(T, 256))
        x2 = _peer_main(x2, hn, a1, th, e2, lw, final_g, _pick(T, 512), 32,
                        final_norm=(l == n_layers - 1))
    return x2.reshape(B, S, D_MODEL)


def kernel(x_prompt, x_sample, norm_mix_g, w_in, mla_q_norm_g, mla_w_uq, mla_kv_norm_g, mla_w_ukv, mla_w_o, fnet_w, fnet_b, gqa_sink, gqa_w_o, w_out, norm_ffn_g, peer_w_q, peer_keys, peer_u, peer_v, final_norm_g):
    p = dict(norm_mix_g=norm_mix_g, w_in=w_in, mla_q_norm_g=mla_q_norm_g, mla_w_uq=mla_w_uq,
             mla_kv_norm_g=mla_kv_norm_g, mla_w_ukv=mla_w_ukv, mla_w_o=mla_w_o, fnet_w=fnet_w,
             fnet_b=fnet_b, gqa_sink=gqa_sink, gqa_w_o=gqa_w_o, w_out=w_out,
             norm_ffn_g=norm_ffn_g, peer_w_q=peer_w_q, peer_keys=peer_keys, peer_u=peer_u,
             peer_v=peer_v)
    layers = [_layer_weights(p, l) for l in range(w_in.shape[0])]
    final_g = final_norm_g[None, :]
    outs = []
    for x in (x_prompt, x_sample):
        S = x.shape[1]
        outs.append(_trunk(x, layers, final_g, _rope_tables(S), _fourier_tables(S)))
    return tuple(outs)
```
